```python
import math
import jax, jax.numpy as jnp
from jax import lax
import numpy as np

D_MODEL = 2048
BATCH = 8
SEQ = 2048
DEPTH = 1

D_MIX = D_MODEL
HEAD_DIM = 64
D_ATTN = D_MIX // 2
N_Q_HEADS = D_ATTN // HEAD_DIM
N_KV_HEADS = 4
Q_PER_KV = N_Q_HEADS // N_KV_HEADS
D_KV = N_KV_HEADS * HEAD_DIM
WINDOW = 128
BLOCK = 128
ROPE_THETA = 10000.0
D_SSM = D_MIX - D_ATTN
SSM_GROUP = 16
N_SSM_GROUPS = D_SSM // SSM_GROUP
SSM_STATE = 64
D_IN = D_ATTN + 2 * D_KV + D_SSM
D_FF = ((8 * D_MODEL // 3 + 255) // 256) * 256
RMS_EPS = 1e-6

kernel_name = 'hymba_swa_sink_s5_sandwich_block'


def rms_norm(x, g):
    xf = x.astype(jnp.float32)
    y = xf * lax.rsqrt(jnp.mean(xf * xf, axis=-1, keepdims=True) + RMS_EPS)
    return (y * g.astype(jnp.float32)).astype(x.dtype)


def rotary(t, positions):
    half = HEAD_DIM // 2
    inv_freq = ROPE_THETA ** (-jnp.arange(half, dtype=jnp.float32) / half)
    ang = positions.astype(jnp.float32)[:, :, None] * inv_freq
    cos = jnp.cos(ang)[:, :, None, :]
    sin = jnp.sin(ang)[:, :, None, :]
    tf = t.astype(jnp.float32)
    t1, t2 = tf[..., :half], tf[..., half:]
    return jnp.concatenate([t1 * cos - t2 * sin, t2 * cos + t1 * sin], axis=-1).astype(t.dtype)


def sliding_window_attention(q, k, v, sinks):
    B, L = q.shape[0], q.shape[1]
    nb = L // BLOCK
    qb = q.reshape(B, nb, BLOCK, N_KV_HEADS, Q_PER_KV, HEAD_DIM)
    kb = k.reshape(B, nb, BLOCK, N_KV_HEADS, HEAD_DIM)
    vb = v.reshape(B, nb, BLOCK, N_KV_HEADS, HEAD_DIM)
    pad = ((0, 0), (1, 0), (0, 0), (0, 0), (0, 0))
    kk = jnp.concatenate([jnp.pad(kb, pad)[:, :-1], kb], axis=2)
    vv = jnp.concatenate([jnp.pad(vb, pad)[:, :-1], vb], axis=2)
    scale = 1.0 / math.sqrt(HEAD_DIM)
    scores = jnp.einsum('bnqkgd,bnskd->bnkgqs', qb, kk).astype(jnp.float32) * scale
    blk = jnp.arange(nb, dtype=jnp.int32)[:, None] * BLOCK
    q_pos = blk + jnp.arange(BLOCK, dtype=jnp.int32)[None, :]
    k_pos = blk - BLOCK + jnp.arange(2 * BLOCK, dtype=jnp.int32)[None, :]
    diff = q_pos[:, :, None] - k_pos[:, None, :]
    mask = (diff >= 0) & (diff < WINDOW) & (k_pos[:, None, :] >= 0)
    scores = jnp.where(mask[None, :, None, None], scores, -jnp.inf)
    sink = sinks.astype(jnp.float32).reshape(N_KV_HEADS, Q_PER_KV)[None, None, :, :, None, None]
    m = jnp.maximum(jnp.max(scores, axis=-1, keepdims=True), sink)
    p = jnp.exp(scores - m)
    probs = p / (jnp.sum(p, axis=-1, keepdims=True) + jnp.exp(sink - m))
    out = jnp.einsum('bnkgqs,bnskd->bnqkgd', probs.astype(v.dtype), vv)
    return out.reshape(B, L, N_Q_HEADS * HEAD_DIM)


def s5_ssm(u, a_re, a_im, log_dt, b_re, b_im, c_re, c_im, d_skip):
    L = u.shape[1]
    uf = u.astype(jnp.float32)
    dt = jnp.exp(log_dt.astype(jnp.float32))[:, None]
    ar = a_re.astype(jnp.float32)
    ai = a_im.astype(jnp.float32)
    mag = jnp.exp(ar * dt)
    lam_re = mag * jnp.cos(ai * dt)
    lam_im = mag * jnp.sin(ai * dt)
    den = ar * ar + ai * ai
    nr = lam_re - 1.0
    ni = lam_im
    f_re = (nr * ar + ni * ai) / den
    f_im = (ni * ar - nr * ai) / den
    br = b_re.astype(jnp.float32)
    bi = b_im.astype(jnp.float32)
    bbar_re = f_re[..., None] * br - f_im[..., None] * bi
    bbar_im = f_re[..., None] * bi + f_im[..., None] * br
    bu_re = jnp.einsum('blgp,gnp->blgn', uf, bbar_re)
    bu_im = jnp.einsum('blgp,gnp->blgn', uf, bbar_im)
    shp = (1, L) + lam_re.shape
    a_seq_re = jnp.broadcast_to(lam_re[None, None], shp)
    a_seq_im = jnp.broadcast_to(lam_im[None, None], shp)

    def combine(earlier, later):
        a1r, a1i, b1r, b1i = earlier
        a2r, a2i, b2r, b2i = later
        return (a2r * a1r - a2i * a1i,
                a2r * a1i + a2i * a1r,
                a2r * b1r - a2i * b1i + b2r,
                a2r * b1i + a2i * b1r + b2i)

    _, _, s_re, s_im = lax.associative_scan(combine, (a_seq_re, a_seq_im, bu_re, bu_im), axis=1)
    y = (jnp.einsum('blgn,gpn->blgp', s_re, c_re.astype(jnp.float32))
         - jnp.einsum('blgn,gpn->blgp', s_im, c_im.astype(jnp.float32))
         + d_skip.astype(jnp.float32) * uf)
    return y.astype(u.dtype)


def hybrid_mixer(xn, positions, w_in, sinks, a_re, a_im, log_dt, b_re, b_im, c_re, c_im,
                 d_skip, w_glu, b_glu, g_attn_out, g_ssm_out, w_o):
    B, L = xn.shape[0], xn.shape[1]
    proj = jnp.einsum('bld,de->ble', xn, w_in)
    q, k, v, u = jnp.split(proj, [D_ATTN, D_ATTN + D_KV, D_ATTN + 2 * D_KV], axis=-1)
    q = rotary(q.reshape(B, L, N_Q_HEADS, HEAD_DIM), positions)
    k = rotary(k.reshape(B, L, N_KV_HEADS, HEAD_DIM), positions)
    v = v.reshape(B, L, N_KV_HEADS, HEAD_DIM)
    attn = sliding_window_attention(q, k, v, sinks)
    y = s5_ssm(u.reshape(B, L, N_SSM_GROUPS, SSM_GROUP), a_re, a_im, log_dt,
               b_re, b_im, c_re, c_im, d_skip).reshape(B, L, D_SSM)
    z = jax.nn.gelu(y)
    ssm = z * jax.nn.sigmoid(jnp.einsum('blc,ce->ble', z, w_glu) + b_glu)
    mixed = jnp.concatenate([rms_norm(attn, g_attn_out), rms_norm(ssm, g_ssm_out)], axis=-1)
    return jnp.einsum('blc,cd->bld', mixed, w_o)


def swiglu(xn, w_gate, w_up, w_down):
    hid = jax.nn.silu(jnp.einsum('bld,df->blf', xn, w_gate)) * jnp.einsum('bld,df->blf', xn, w_up)
    return jnp.einsum('blf,fd->bld', hid, w_down)


def setup_inputs(seed: int = 0) -> dict:
    key = jax.random.key(seed)
    ks = jax.random.split(key, 24)
    f32 = jnp.float32

    def nrm(k, shape, scale):
        return jax.random.normal(k, shape, f32) * scale

    def gain(k, width):
        return 1.0 + nrm(k, (DEPTH, width), 0.05)

    G, N, P = N_SSM_GROUPS, SSM_STATE, SSM_GROUP
    x = nrm(ks[0], (BATCH, SEQ, D_MODEL), 1.0)
    positions = jnp.tile(jnp.arange(SEQ, dtype=jnp.int32)[None, :], (BATCH, 1))
    return {
        'x': x,
        'positions': positions,
        'g_pre_mix': gain(ks[1], D_MODEL),
        'w_in': nrm(ks[2], (DEPTH, D_MODEL, D_IN), D_MODEL ** -0.5),
        'sinks': nrm(ks[3], (DEPTH, N_Q_HEADS), 1.0),
        'a_re': -0.5 + nrm(ks[4], (DEPTH, G, N), 0.01),
        'a_im': math.pi * jnp.arange(N, dtype=f32)[None, None, :] + nrm(ks[5], (DEPTH, G, N), 0.01),
        'log_dt': jax.random.uniform(ks[6], (DEPTH, G), f32, math.log(1e-3), math.log(1e-1)),
        'b_re': nrm(ks[7], (DEPTH, G, N, P), (2 * P) ** -0.5),
        'b_im': nrm(ks[8], (DEPTH, G, N, P), (2 * P) ** -0.5),
        'c_re': nrm(ks[9], (DEPTH, G, P, N), (2 * N) ** -0.5),
        'c_im': nrm(ks[10], (DEPTH, G, P, N), (2 * N) ** -0.5),
        'd_skip': nrm(ks[11], (DEPTH, G, P), 1.0),
        'w_glu': nrm(ks[12], (DEPTH, D_SSM, D_SSM), D_SSM ** -0.5),
        'b_glu': nrm(ks[13], (DEPTH, D_SSM), 0.01),
        'g_attn_out': gain(ks[14], D_ATTN),
        'g_ssm_out': gain(ks[15], D_SSM),
        'w_o': nrm(ks[16], (DEPTH, D_MIX, D_MODEL), D_MIX ** -0.5),
        'g_post_mix': gain(ks[17], D_MODEL),
        'g_pre_ffn': gain(ks[18], D_MODEL),
        'w_gate': nrm(ks[19], (DEPTH, D_MODEL, D_FF), D_MODEL ** -0.5),
        'w_up': nrm(ks[20], (DEPTH, D_MODEL, D_FF), D_MODEL ** -0.5),
        'w_down': nrm(ks[21], (DEPTH, D_FF, D_MODEL), D_FF ** -0.5),
        'g_post_ffn': gain(ks[22], D_MODEL),
    }


def reference(x, positions, g_pre_mix, w_in, sinks, a_re, a_im, log_dt, b_re, b_im, c_re, c_im,
              d_skip, w_glu, b_glu, g_attn_out, g_ssm_out, w_o, g_post_mix, g_pre_ffn,
              w_gate, w_up, w_down, g_post_ffn):
    h = x
    for i in range(DEPTH):
        mix = hybrid_mixer(rms_norm(h, g_pre_mix[i]), positions, w_in[i], sinks[i], a_re[i], a_im[i],
                           log_dt[i], b_re[i], b_im[i], c_re[i], c_im[i], d_skip[i], w_glu[i],
                           b_glu[i], g_attn_out[i], g_ssm_out[i], w_o[i])
        h = h + rms_norm(mix, g_post_mix[i])
        ff = swiglu(rms_norm(h, g_pre_ffn[i]), w_gate[i], w_up[i], w_down[i])
        h = h + rms_norm(ff, g_post_ffn[i])
    return h
```

```python
import functools
import math

import jax
import jax.numpy as jnp
import numpy as np
from jax import lax
from jax.experimental import pallas as pl
from jax.experimental.pallas import tpu as pltpu

D_MODEL = 2048
BATCH = 8
SEQ = 2048
HEAD_DIM = 64
D_ATTN = 1024
N_Q_HEADS = 16
N_KV_HEADS = 4
Q_PER_KV = 4
D_KV = 256
WINDOW = 128
ROPE_THETA = 10000.0
D_SSM = 1024
SSM_GROUP = 16
N_SSM_GROUPS = 64
SSM_STATE = 64
D_IN = D_ATTN + 2 * D_KV + D_SSM
D_FF = 5632
RMS_EPS = 1e-6

LANES = 128
SUBLANES = 8
GROUPS_PER_TILE = 16
N_GROUP_TILES = N_SSM_GROUPS // GROUPS_PER_TILE
TILE_CH = GROUPS_PER_TILE * SSM_GROUP
TILE_ST = GROUPS_PER_TILE * SSM_STATE
N_CHUNKS = TILE_ST // LANES

F32 = jnp.float32
BF16 = jnp.bfloat16


def _rms(x, g):
    ms = jnp.mean(x * x, axis=-1, keepdims=True)
    return x * lax.rsqrt(ms + RMS_EPS) * g


def _ssm_prep_kernel(ar_ref, ai_ref, ldt_ref, br_ref, bi_ref, lr_ref, li_ref, bbr_ref, bbi_ref):
    ar = ar_ref[...]
    ai = ai_ref[...]
    dt = jnp.exp(ldt_ref[...])
    mag = jnp.exp(ar * dt)
    lam_re = mag * jnp.cos(ai * dt)
    lam_im = mag * jnp.sin(ai * dt)
    den = ar * ar + ai * ai
    nr = lam_re - 1.0
    ni = lam_im
    f_re = (nr * ar + ni * ai) / den
    f_im = (ni * ar - nr * ai) / den
    br = br_ref[...]
    bi = bi_ref[...]
    lr_ref[...] = lam_re
    li_ref[...] = lam_im
    bbr_ref[...] = f_re * br - f_im * bi
    bbi_ref[...] = f_re * bi + f_im * br


def _ssm_prep(a_re, a_im, log_dt, b_re, b_im):
    G, N, P = N_SSM_GROUPS, SSM_STATE, SSM_GROUP
    rep = lambda a: jnp.repeat(a, P, axis=0)
    ar = rep(a_re)
    ai = rep(a_im)
    ldt = rep(jnp.broadcast_to(log_dt[:, None], (G, N)))
    brt = jnp.transpose(b_re, (0, 2, 1)).reshape(G * P, N)
    bit = jnp.transpose(b_im, (0, 2, 1)).reshape(G * P, N)
    shp = jax.ShapeDtypeStruct((G * P, N), F32)
    lam_re, lam_im, bb_re, bb_im = pl.pallas_call(
        _ssm_prep_kernel, out_shape=(shp, shp, shp, shp), name="ssm_prep")(ar, ai, ldt, brt, bit)
    return lam_re[::P], lam_im[::P], bb_re, bb_im


def _block_diag_b(bb_re, bb_im):
    eye = jnp.eye(GROUPS_PER_TILE, dtype=F32)[None, :, None, :, None]

    def place(bb):
        b5 = bb.reshape(N_GROUP_TILES, GROUPS_PER_TILE, SSM_GROUP, 1, SSM_STATE)
        return (b5 * eye).reshape(N_GROUP_TILES, TILE_CH, TILE_ST)

    return jnp.concatenate([place(bb_re), place(bb_im)], axis=-1).astype(BF16)


def _block_diag_c(c_re, c_im):
    eye = jnp.eye(GROUPS_PER_TILE, dtype=F32)[None, :, None, :, None]

    def place(c):
        c4 = c.reshape(N_GROUP_TILES, GROUPS_PER_TILE, SSM_GROUP, SSM_STATE)
        c4 = jnp.transpose(c4, (0, 3, 1, 2))
        return (c4[:, None] * eye).reshape(N_GROUP_TILES, TILE_ST, TILE_CH)

    return jnp.concatenate([place(c_re), place(-c_im)], axis=1).astype(BF16)


def _inproj_kernel(x_ref, pos_ref, g_ref, invf_ref, sgn_ref, w_ref, q_ref, k_ref, v_ref, u_ref):
    tm = x_ref.shape[0]
    xn = _rms(x_ref[...], g_ref[...]).astype(BF16)
    ang = pos_ref[...].astype(F32) * invf_ref[...]
    cos = jnp.cos(ang)
    sin = jnp.sin(ang) * sgn_ref[...]
    lane = lax.broadcasted_iota(jnp.int32, (tm, LANES), 1)
    first_half = (lane & (HEAD_DIM // 2)) == 0

    def rotary(t):
        partner = jnp.where(first_half,
                            pltpu.roll(t, LANES - HEAD_DIM // 2, 1),
                            pltpu.roll(t, HEAD_DIM // 2, 1))
        return t * cos + partner * sin

    scale = 1.0 / math.sqrt(HEAD_DIM)
    for j in range(D_ATTN // 256):
        p = jnp.dot(xn, w_ref[:, j * 256:(j + 1) * 256], preferred_element_type=F32)
        for c in range(2):
            col = j * 256 + c * LANES
            q_ref[:, col:col + LANES] = (rotary(p[:, c * LANES:(c + 1) * LANES]) * scale).astype(BF16)
    p = jnp.dot(xn, w_ref[:, D_ATTN:D_ATTN + D_KV], preferred_element_type=F32)
    for c in range(2):
        k_ref[:, c * LANES:(c + 1) * LANES] = rotary(p[:, c * LANES:(c + 1) * LANES]).astype(BF16)
    v_ref[...] = jnp.dot(xn, w_ref[:, D_ATTN + D_KV:D_ATTN + 2 * D_KV],
                         preferred_element_type=F32).astype(BF16)
    for j in range(D_SSM // 256):
        col = D_ATTN + 2 * D_KV + j * 256
        u_ref[:, j * 256:(j + 1) * 256] = jnp.dot(xn, w_ref[:, col:col + 256],
                                                   preferred_element_type=F32)


def _in_proj(x2, pos2, g_pre_mix, w_in_bf, tm):
    T = x2.shape[0]
    nt = SEQ // tm
    half = HEAD_DIM // 2
    invf = ROPE_THETA ** (-np.arange(half, dtype=np.float32) / half)
    invf = np.tile(invf.astype(np.float32), LANES // half)[None, :]
    sgn = np.tile(np.concatenate([-np.ones(half, np.float32), np.ones(half, np.float32)]),
                  LANES // HEAD_DIM)[None, :]
    row = lambda b, i: (b * nt + i, 0)
    const = lambda b, i: (0, 0)
    return pl.pallas_call(
        _inproj_kernel,
        grid=(BATCH, nt),
        in_specs=[
            pl.BlockSpec((tm, D_MODEL), row),
            pl.BlockSpec((tm, 1), row),
            pl.BlockSpec((1, D_MODEL), const),
            pl.BlockSpec((1, LANES), const),
            pl.BlockSpec((1, LANES), const),
            pl.BlockSpec((D_MODEL, D_IN), const),
        ],
        out_specs=[
            pl.BlockSpec((tm, D_ATTN), row),
            pl.BlockSpec((tm, D_KV), row),
            pl.BlockSpec((tm, D_KV), row),
            pl.BlockSpec((tm, D_SSM), lambda b, i: (i, b)),
        ],
        out_shape=[
            jax.ShapeDtypeStruct((T, D_ATTN), BF16),
            jax.ShapeDtypeStruct((T, D_KV), BF16),
            jax.ShapeDtypeStruct((T, D_KV), BF16),
            jax.ShapeDtypeStruct((SEQ, BATCH * D_SSM), F32),
        ],
        compiler_params=pltpu.CompilerParams(
            dimension_semantics=("parallel", "parallel"), vmem_limit_bytes=52 * 2**20),
        name="in_proj",
    )(x2, pos2, g_pre_mix, jnp.asarray(invf), jnp.asarray(sgn), w_in_bf)


def _attn_kernel(sinks_ref, q_ref, k_ref, v_ref, kh_ref, vh_ref, g_ref, o_ref, acc_ref):
    tq = q_ref.shape[0]
    first_tile = pl.program_id(1) == 0
    row = lax.broadcasted_iota(jnp.int32, (WINDOW, 2 * WINDOW), 0)
    col = lax.broadcasted_iota(jnp.int32, (WINDOW, 2 * WINDOW), 1)
    band = (col > row) & (col <= row + WINDOW)
    band_first = band & jnp.logical_or(col >= WINDOW, jnp.logical_not(first_tile))
    nt_dims = (((1,), (1,)), ((), ()))
    for j in range(tq // WINDOW):
        lo, hi = j * WINDOW, (j + 1) * WINDOW
        if j == 0:
            kk = jnp.concatenate([kh_ref[...], k_ref[lo:hi, :]], axis=0)
            vv = jnp.concatenate([vh_ref[...], v_ref[lo:hi, :]], axis=0)
            mask = band_first
        else:
            kk = k_ref[lo - WINDOW:hi, :]
            vv = v_ref[lo - WINDOW:hi, :]
            mask = band
        for kv in range(N_KV_HEADS):
            kkh = kk[:, kv * HEAD_DIM:(kv + 1) * HEAD_DIM]
            vvh = vv[:, kv * HEAD_DIM:(kv + 1) * HEAD_DIM]
            for gq in range(Q_PER_KV):
                h = kv * Q_PER_KV + gq
                qh = q_ref[lo:hi, h * HEAD_DIM:(h + 1) * HEAD_DIM]
                s = lax.dot_general(qh, kkh, nt_dims, preferred_element_type=F32)
                s = jnp.where(mask, s, -jnp.inf)
                sink = sinks_ref[h]
                m = jnp.maximum(jnp.max(s, axis=-1, keepdims=True), sink)
                p = jnp.exp(s - m)
                den = jnp.sum(p, axis=-1, keepdims=True) + jnp.exp(sink - m)
                o = jnp.dot(p.astype(BF16), vvh, preferred_element_type=F32)
                acc_ref[:, h * HEAD_DIM:(h + 1) * HEAD_DIM] = o / den
        o_ref[lo:hi, :] = _rms(acc_ref[...], g_ref[...]).astype(BF16)


def _attention(q, k, v, sinks, g_attn_out, tq):
    T = q.shape[0]
    nt = SEQ // tq
    per = tq // WINDOW
    row = lambda b, i: (b * nt + i, 0)
    halo = lambda b, i: (jnp.maximum(b * (SEQ // WINDOW) + i * per - 1, 0), 0)
    return pl.pallas_call(
        _attn_kernel,
        grid=(BATCH, nt),
        in_specs=[
            pl.BlockSpec(memory_space=pltpu.SMEM),
            pl.BlockSpec((tq, D_ATTN), row),
            pl.BlockSpec((tq, D_KV), row),
            pl.BlockSpec((tq, D_KV), row),
            pl.BlockSpec((WINDOW, D_KV), halo),
            pl.BlockSpec((WINDOW, D_KV), halo),
            pl.BlockSpec((1, D_ATTN), lambda b, i: (0, 0)),
        ],
        out_specs=pl.BlockSpec((tq, D_ATTN), row),
        out_shape=jax.ShapeDtypeStruct((T, D_ATTN), BF16),
        scratch_shapes=[pltpu.VMEM((WINDOW, D_ATTN), F32)],
        compiler_params=pltpu.CompilerParams(dimension_semantics=("parallel", "parallel")),
        name="attention",
    )(sinks, q, k, v, k, v, g_attn_out)


def _ssm_kernel(u_ref, bblk_ref, cblk_ref, lre_ref, lim_ref, dskip_ref, wglu_ref, bglu_ref, g_ref,
                o_ref, s_ref, y_ref, carry_ref):
    tl = u_ref.shape[0]
    rows = tl * BATCH

    @pl.when(pl.program_id(0) == 0)
    def _():
        carry_ref[...] = jnp.zeros_like(carry_ref)

    u = u_ref[...].reshape(rows, D_SSM)
    ub = u.astype(BF16)
    for gt in range(N_GROUP_TILES):
        ch = slice(gt * TILE_CH, (gt + 1) * TILE_CH)
        s_ref[...] = jnp.dot(ub[:, ch], bblk_ref[gt], preferred_element_type=F32)
        lam = [(lre_ref[gt, :, c * LANES:(c + 1) * LANES], lim_ref[gt, :, c * LANES:(c + 1) * LANES])
               for c in range(N_CHUNKS)]
        init = tuple((carry_ref[gt, :, c * LANES:(c + 1) * LANES],
                      carry_ref[gt, :, TILE_ST + c * LANES:TILE_ST + (c + 1) * LANES])
                     for c in range(N_CHUNKS))

        def step(t, state):
            r0 = pl.multiple_of(t * BATCH, SUBLANES)
            new = []
            for c in range(N_CHUNKS):
                sr, si = state[c]
                lr, li = lam[c]
                re_cols = slice(c * LANES, (c + 1) * LANES)
                im_cols = slice(TILE_ST + c * LANES, TILE_ST + (c + 1) * LANES)
                nr = lr * sr - li * si + s_ref[pl.ds(r0, BATCH), re_cols]
                ni = lr * si + li * sr + s_ref[pl.ds(r0, BATCH), im_cols]
                s_ref[pl.ds(r0, BATCH), re_cols] = nr
                s_ref[pl.ds(r0, BATCH), im_cols] = ni
                new.append((nr, ni))
            return tuple(new)

        final = lax.fori_loop(0, tl, step, init)
        for c in range(N_CHUNKS):
            carry_ref[gt, :, c * LANES:(c + 1) * LANES] = final[c][0]
            carry_ref[gt, :, TILE_ST + c * LANES:TILE_ST + (c + 1) * LANES] = final[c][1]
        y = jnp.dot(s_ref[...].astype(BF16), cblk_ref[gt], preferred_element_type=F32)
        y_ref[:, ch] = y + dskip_ref[:, ch] * u[:, ch]
    z = jax.nn.gelu(y_ref[...])
    gate = jax.nn.sigmoid(jnp.dot(z.astype(BF16), wglu_ref[...], preferred_element_type=F32)
                          + bglu_ref[...])
    o_ref[...] = _rms(z * gate, g_ref[...]).reshape(tl, BATCH, D_SSM)


def _ssm(u3, bblk, cblk, lam_re, lam_im, d_skip, w_glu_bf, b_glu, g_ssm_out, tl):
    rows = tl * BATCH
    c2 = lambda i: (0, 0)
    c3 = lambda i: (0, 0, 0)
    return pl.pallas_call(
        _ssm_kernel,
        grid=(SEQ // tl,),
        in_specs=[
            pl.BlockSpec((tl, BATCH, D_SSM), lambda i: (i, 0, 0)),
            pl.BlockSpec((N_GROUP_TILES, TILE_CH, 2 * TILE_ST), c3),
            pl.BlockSpec((N_GROUP_TILES, 2 * TILE_ST, TILE_CH), c3),
            pl.BlockSpec((N_GROUP_TILES, BATCH, TILE_ST), c3),
            pl.BlockSpec((N_GROUP_TILES, BATCH, TILE_ST), c3),
            pl.BlockSpec((1, D_SSM), c2),
            pl.BlockSpec((D_SSM, D_SSM), c2),
            pl.BlockSpec((1, D_SSM), c2),
            pl.BlockSpec((1, D_SSM), c2),
        ],
        out_specs=pl.BlockSpec((tl, BATCH, D_SSM), lambda i: (i, 0, 0)),
        out_shape=jax.ShapeDtypeStruct((SEQ, BATCH, D_SSM), F32),
        scratch_shapes=[
            pltpu.VMEM((rows, 2 * TILE_ST), F32),
            pltpu.VMEM((rows, D_SSM), F32),
            pltpu.VMEM((N_GROUP_TILES, BATCH, 2 * TILE_ST), F32),
        ],
        compiler_params=pltpu.CompilerParams(
            dimension_semantics=("arbitrary",), vmem_limit_bytes=52 * 2**20),
        name="ssm",
    )(u3, bblk, cblk, lam_re, lam_im, d_skip, w_glu_bf, b_glu, g_ssm_out)


def _outproj_kernel(attn_ref, ssm_ref, x_ref, wo_ref, gpost_ref, gpre_ref, h_ref, hn_ref):
    mix = jnp.dot(attn_ref[...], wo_ref[:D_ATTN, :], preferred_element_type=F32)
    mix = mix + jnp.dot(ssm_ref[...].astype(BF16), wo_ref[D_ATTN:, :], preferred_element_type=F32)
    h = x_ref[...] + _rms(mix, gpost_ref[...])
    h_ref[...] = h
    hn_ref[...] = _rms(h, gpre_ref[...]).astype(BF16)


def _out_proj(attn_n, ssm2, x2, w_o_bf, g_post_mix, g_pre_ffn, tm):
    T = x2.shape[0]
    nt = SEQ // tm
    row = lambda b, i: (b * nt + i, 0)
    const = lambda b, i: (0, 0)
    return pl.pallas_call(
        _outproj_kernel,
        grid=(BATCH, nt),
        in_specs=[
            pl.BlockSpec((tm, D_ATTN), row),
            pl.BlockSpec((tm, D_SSM), lambda b, i: (i, b)),
            pl.BlockSpec((tm, D_MODEL), row),
            pl.BlockSpec((D_MODEL, D_MODEL), const),
            pl.BlockSpec((1, D_MODEL), const),
            pl.BlockSpec((1, D_MODEL), const),
        ],
        out_specs=[pl.BlockSpec((tm, D_MODEL), row), pl.BlockSpec((tm, D_MODEL), row)],
        out_shape=[jax.ShapeDtypeStruct((T, D_MODEL), F32), jax.ShapeDtypeStruct((T, D_MODEL), BF16)],
        compiler_params=pltpu.CompilerParams(
            dimension_semantics=("parallel", "parallel"), vmem_limit_bytes=52 * 2**20),
        name="out_proj",
    )(attn_n, ssm2, x2, w_o_bf, g_post_mix, g_pre_ffn)


def _ffn_kernel(hn_ref, h_ref, wg_ref, wu_ref, wd_ref, g_ref, o_ref, acc_ref):
    j = pl.program_id(1)
    hn = hn_ref[...]
    gate = jnp.dot(hn, wg_ref[...], preferred_element_type=F32)
    up = jnp.dot(hn, wu_ref[...], preferred_element_type=F32)
    hid = (jax.nn.silu(gate) * up).astype(BF16)
    part = jnp.dot(hid, wd_ref[...], preferred_element_type=F32)

    @pl.when(j == 0)
    def _():
        acc_ref[...] = part

    @pl.when(j > 0)
    def _():
        acc_ref[...] += part

    @pl.when(j == pl.num_programs(1) - 1)
    def _():
        o_ref[...] = h_ref[...] + _rms(acc_ref[...], g_ref[...])


def _ffn(hn, h, wg_bf, wu_bf, wd_bf, g_post_ffn, tm, tf):
    T = h.shape[0]
    row = lambda i, j: (i, 0)
    return pl.pallas_call(
        _ffn_kernel,
        grid=(T // tm, D_FF // tf),
        in_specs=[
            pl.BlockSpec((tm, D_MODEL), row),
            pl.BlockSpec((tm, D_MODEL), row),
            pl.BlockSpec((D_MODEL, tf), lambda i, j: (0, j)),
            pl.BlockSpec((D_MODEL, tf), lambda i, j: (0, j)),
            pl.BlockSpec((tf, D_MODEL), lambda i, j: (j, 0)),
            pl.BlockSpec((1, D_MODEL), lambda i, j: (0, 0)),
        ],
        out_specs=pl.BlockSpec((tm, D_MODEL), row),
        out_shape=jax.ShapeDtypeStruct((T, D_MODEL), F32),
        scratch_shapes=[pltpu.VMEM((tm, D_MODEL), F32)],
        compiler_params=pltpu.CompilerParams(
            dimension_semantics=("parallel", "arbitrary"), vmem_limit_bytes=52 * 2**20),
        name="ffn",
    )(hn, h, wg_bf, wu_bf, wd_bf, g_post_ffn)


def kernel(x, positions, g_pre_mix, w_in, sinks, a_re, a_im, log_dt, b_re, b_im, c_re, c_im, d_skip,
           w_glu, b_glu, g_attn_out, g_ssm_out, w_o, g_post_mix, g_pre_ffn, w_gate, w_up, w_down,
           g_post_ffn):
    depth = w_in.shape[0]
    T = BATCH * SEQ
    h = x.reshape(T, D_MODEL)
    pos2 = positions.reshape(T, 1)
    for i in range(depth):
        lam_re, lam_im, bb_re, bb_im = _ssm_prep(a_re[i], a_im[i], log_dt[i], b_re[i], b_im[i])
        bblk = _block_diag_b(bb_re, bb_im)
        cblk = _block_diag_c(c_re[i], c_im[i])
        lam_shape = (N_GROUP_TILES, BATCH, TILE_ST)
        lam_re_t = jnp.broadcast_to(lam_re.reshape(N_GROUP_TILES, 1, TILE_ST), lam_shape)
        lam_im_t = jnp.broadcast_to(lam_im.reshape(N_GROUP_TILES, 1, TILE_ST), lam_shape)

        q, k, v, u = _in_proj(h, pos2, g_pre_mix[i][None, :], w_in[i].astype(BF16), tm=512)
        attn_n = _attention(q, k, v, sinks[i], g_attn_out[i][None, :], tq=256)
        ssm_n = _ssm(u.reshape(SEQ, BATCH, D_SSM), bblk, cblk, lam_re_t, lam_im_t,
                     d_skip[i].reshape(1, D_SSM), w_glu[i].astype(BF16), b_glu[i][None, :],
                     g_ssm_out[i][None, :], tl=64)
        h, hn = _out_proj(attn_n, ssm_n.reshape(SEQ, BATCH * D_SSM), h, w_o[i].astype(BF16),
                          g_post_mix[i][None, :], g_pre_ffn[i][None, :], tm=512)
        h = _ffn(hn, h, w_gate[i].astype(BF16), w_up[i].astype(BF16), w_down[i].astype(BF16),
                 g_post_ffn[i][None, :], tm=512, tf=512)
    return h.reshape(BATCH, SEQ, D_MODEL)
```

```python
import math

import jax
import jax.numpy as jnp
import numpy as np
from jax import lax
from jax.experimental import pallas as pl
from jax.experimental.pallas import tpu as pltpu

D_MODEL = 2048
BATCH = 8
SEQ = 2048
HEAD_DIM = 64
D_ATTN = 1024
N_Q_HEADS = 16
N_KV_HEADS = 4
Q_PER_KV = 4
D_KV = 256
WINDOW = 128
ROPE_THETA = 10000.0
D_SSM = 1024
SSM_GROUP = 16
N_SSM_GROUPS = 64
SSM_STATE = 64
D_IN = D_ATTN + 2 * D_KV + D_SSM
D_FF = 5632
RMS_EPS = 1e-6

LANES = 128
SUBLANES = 8
GROUPS_PER_TILE = 16
N_GROUP_TILES = N_SSM_GROUPS // GROUPS_PER_TILE
TILE_CH = GROUPS_PER_TILE * SSM_GROUP
TILE_ST = GROUPS_PER_TILE * SSM_STATE
N_CHUNKS = TILE_ST // LANES
D_KV_DUP = 2 * N_KV_HEADS * HEAD_DIM

F32 = jnp.float32
BF16 = jnp.bfloat16
NT_DIMS = (((1,), (1,)), ((), ()))


def _rms(x, g):
    ms = jnp.mean(x * x, axis=-1, keepdims=True)
    return x * lax.rsqrt(ms + RMS_EPS) * g


def _split3(x):
    x1 = x.astype(BF16)
    r1 = x - x1.astype(F32)
    x2 = r1.astype(BF16)
    x3 = (r1 - x2.astype(F32)).astype(BF16)
    return x1, x2, x3


def _ssm_prep_kernel(ar_ref, ai_ref, ldt_ref, br_ref, bi_ref, cr_ref, ci_ref, ep_ref, en_ref,
                     lre_ref, lim_ref, bblk_ref, cblk_ref):
    ar = ar_ref[0]
    ai = ai_ref[0]
    dt = jnp.exp(ldt_ref[0])
    mag = jnp.exp(ar * dt)
    lam_re = mag * jnp.cos(ai * dt)
    lam_im = mag * jnp.sin(ai * dt)
    den = ar * ar + ai * ai
    nr = lam_re - 1.0
    ni = lam_im
    f_re = (nr * ar + ni * ai) / den
    f_im = (ni * ar - nr * ai) / den
    lre_ref[0] = jnp.broadcast_to(lam_re, (BATCH, TILE_ST))
    lim_ref[0] = jnp.broadcast_to(lam_im, (BATCH, TILE_ST))

    ep = ep_ref[...]

    def spread_b(b):
        return sum(lax.dot_general(ep, piece, NT_DIMS, preferred_element_type=F32)
                   for piece in _split3(b))

    pb_re = spread_b(br_ref[0])
    pb_im = spread_b(bi_ref[0])
    row = lax.broadcasted_iota(jnp.int32, (TILE_CH, TILE_ST), 0)
    col = lax.broadcasted_iota(jnp.int32, (TILE_CH, TILE_ST), 1)
    diag = (row // SSM_GROUP) == (col // SSM_STATE)
    bblk_ref[0, :, :TILE_ST] = jnp.where(diag, f_re * pb_re - f_im * pb_im, 0.0).astype(BF16)
    bblk_ref[0, :, TILE_ST:] = jnp.where(diag, f_re * pb_im + f_im * pb_re, 0.0).astype(BF16)

    en = en_ref[...]
    row = lax.broadcasted_iota(jnp.int32, (TILE_ST, TILE_CH), 0)
    col = lax.broadcasted_iota(jnp.int32, (TILE_ST, TILE_CH), 1)
    diag = (row // SSM_STATE) == (col // SSM_GROUP)
    pc_re = lax.dot_general(en, cr_ref[0].astype(BF16), NT_DIMS, preferred_element_type=F32)
    pc_im = lax.dot_general(en, ci_ref[0].astype(BF16), NT_DIMS, preferred_element_type=F32)
    cblk_ref[0, :TILE_ST, :] = jnp.where(diag, pc_re, 0.0).astype(BF16)
    cblk_ref[0, TILE_ST:, :] = jnp.where(diag, -pc_im, 0.0).astype(BF16)


def _ssm_prep(a_re, a_im, log_dt, b_re, b_im, c_re, c_im):
    nt = N_GROUP_TILES
    lanes3 = lambda a: a.reshape(nt, 1, TILE_ST)
    ldt = jnp.broadcast_to(log_dt[:, None], (N_SSM_GROUPS, SSM_STATE))
    ep = jnp.asarray(np.tile(np.eye(SSM_GROUP, dtype=np.float32), (GROUPS_PER_TILE, 1)), BF16)
    en = jnp.asarray(np.tile(np.eye(SSM_STATE, dtype=np.float32), (GROUPS_PER_TILE, 1)), BF16)
    t3 = lambda t: (t, 0, 0)
    c2 = lambda t: (0, 0)
    return pl.pallas_call(
        _ssm_prep_kernel,
        grid=(nt,),
        in_specs=[
            pl.BlockSpec((1, 1, TILE_ST), t3),
            pl.BlockSpec((1, 1, TILE_ST), t3),
            pl.BlockSpec((1, 1, TILE_ST), t3),
            pl.BlockSpec((1, TILE_ST, SSM_GROUP), t3),
            pl.BlockSpec((1, TILE_ST, SSM_GROUP), t3),
            pl.BlockSpec((1, TILE_CH, SSM_STATE), t3),
            pl.BlockSpec((1, TILE_CH, SSM_STATE), t3),
            pl.BlockSpec((TILE_CH, SSM_GROUP), c2),
            pl.BlockSpec((TILE_ST, SSM_STATE), c2),
        ],
        out_specs=[
            pl.BlockSpec((1, BATCH, TILE_ST), t3),
            pl.BlockSpec((1, BATCH, TILE_ST), t3),
            pl.BlockSpec((1, TILE_CH, 2 * TILE_ST), t3),
            pl.BlockSpec((1, 2 * TILE_ST, TILE_CH), t3),
        ],
        out_shape=[
            jax.ShapeDtypeStruct((nt, BATCH, TILE_ST), F32),
            jax.ShapeDtypeStruct((nt, BATCH, TILE_ST), F32),
            jax.ShapeDtypeStruct((nt, TILE_CH, 2 * TILE_ST), BF16),
            jax.ShapeDtypeStruct((nt, 2 * TILE_ST, TILE_CH), BF16),
        ],
        compiler_params=pltpu.CompilerParams(dimension_semantics=("parallel",)),
        name="ssm_prep",
    )(lanes3(a_re), lanes3(a_im), lanes3(ldt),
      b_re.reshape(nt, TILE_ST, SSM_GROUP), b_im.reshape(nt, TILE_ST, SSM_GROUP),
      c_re.reshape(nt, TILE_CH, SSM_STATE), c_im.reshape(nt, TILE_CH, SSM_STATE), ep, en)


def _inproj_kernel(x_ref, pos_ref, g_ref, invf_ref, sgn_ref, w_ref, q_ref, k_ref, v_ref, u_ref):
    tm = x_ref.shape[0]
    xn = _rms(x_ref[...], g_ref[...]).astype(BF16)
    ang = pos_ref[...].astype(F32) * invf_ref[...]
    cos = jnp.cos(ang)
    sin = jnp.sin(ang) * sgn_ref[...]
    lane = lax.broadcasted_iota(jnp.int32, (tm, LANES), 1)
    first_half = (lane & (HEAD_DIM // 2)) == 0
    first_head = lane < HEAD_DIM

    def rotary(t):
        partner = jnp.where(first_half,
                            pltpu.roll(t, LANES - HEAD_DIM // 2, 1),
                            pltpu.roll(t, HEAD_DIM // 2, 1))
        return t * cos + partner * sin

    def store_dup(ref, c, t):
        swapped = pltpu.roll(t, HEAD_DIM, 1)
        ref[:, (2 * c) * LANES:(2 * c + 1) * LANES] = jnp.where(first_head, t, swapped).astype(BF16)
        ref[:, (2 * c + 1) * LANES:(2 * c + 2) * LANES] = jnp.where(first_head, swapped, t).astype(BF16)

    scale = 1.0 / math.sqrt(HEAD_DIM)
    for j in range(D_ATTN // 256):
        p = jnp.dot(xn, w_ref[:, j * 256:(j + 1) * 256], preferred_element_type=F32)
        for c in range(2):
            col = j * 256 + c * LANES
            q_ref[:, col:col + LANES] = (rotary(p[:, c * LANES:(c + 1) * LANES]) * scale).astype(BF16)
    p = jnp.dot(xn, w_ref[:, D_ATTN:D_ATTN + D_KV], preferred_element_type=F32)
    for c in range(2):
        store_dup(k_ref, c, rotary(p[:, c * LANES:(c + 1) * LANES]))
    p = jnp.dot(xn, w_ref[:, D_ATTN + D_KV:D_ATTN + 2 * D_KV], preferred_element_type=F32)
    for c in range(2):
        store_dup(v_ref, c, p[:, c * LANES:(c + 1) * LANES])
    for j in range(D_SSM // 256):
        col = D_ATTN + 2 * D_KV + j * 256
        u_ref[:, j * 256:(j + 1) * 256] = jnp.dot(xn, w_ref[:, col:col + 256],
                                                   preferred_element_type=F32)


def _in_proj(x2, pos2, g_pre_mix, w_in_bf, tm):
    T = x2.shape[0]
    nt = SEQ // tm
    half = HEAD_DIM // 2
    invf = ROPE_THETA ** (-np.arange(half, dtype=np.float32) / half)
    invf = np.tile(invf.astype(np.float32), LANES // half)[None, :]
    sgn = np.tile(np.concatenate([-np.ones(half, np.float32), np.ones(half, np.float32)]),
                  LANES // HEAD_DIM)[None, :]
    row = lambda b, i: (b * nt + i, 0)
    const = lambda b, i: (0, 0)
    return pl.pallas_call(
        _inproj_kernel,
        grid=(BATCH, nt),
        in_specs=[
            pl.BlockSpec((tm, D_MODEL), row),
            pl.BlockSpec((tm, 1), row),
            pl.BlockSpec((1, D_MODEL), const),
            pl.BlockSpec((1, LANES), const),
            pl.BlockSpec((1, LANES), const),
            pl.BlockSpec((D_MODEL, D_IN), const),
        ],
        out_specs=[
            pl.BlockSpec((tm, D_ATTN), row),
            pl.BlockSpec((tm, D_KV_DUP), row),
            pl.BlockSpec((tm, D_KV_DUP), row),
            pl.BlockSpec((tm, D_SSM), lambda b, i: (i, b)),
        ],
        out_shape=[
            jax.ShapeDtypeStruct((T, D_ATTN), BF16),
            jax.ShapeDtypeStruct((T, D_KV_DUP), BF16),
            jax.ShapeDtypeStruct((T, D_KV_DUP), BF16),
            jax.ShapeDtypeStruct((SEQ, BATCH * D_SSM), F32),
        ],
        compiler_params=pltpu.CompilerParams(
            dimension_semantics=("parallel", "parallel"), vmem_limit_bytes=52 * 2**20),
        name="in_proj",
    )(x2, pos2, g_pre_mix, jnp.asarray(invf), jnp.asarray(sgn), w_in_bf)


def _attn_kernel(sinks_ref, q_ref, k_ref, v_ref, kh_ref, vh_ref, g_ref, o_ref, acc_ref):
    tq = q_ref.shape[0]
    first_tile = pl.program_id(1) == 0
    two = 2 * WINDOW
    row = lax.broadcasted_iota(jnp.int32, (two, WINDOW), 0) % WINDOW
    col = lax.broadcasted_iota(jnp.int32, (two, WINDOW), 1)
    use_prev = col > row
    top = lax.broadcasted_iota(jnp.int32, (two, 1), 0) < WINDOW
    low_lanes = lax.broadcasted_iota(jnp.int32, (two, LANES), 1) < HEAD_DIM
    zero = jnp.zeros((), BF16)
    for j in range(tq // WINDOW):
        lo, hi = j * WINDOW, (j + 1) * WINDOW
        if j == 0:
            kk = jnp.concatenate([kh_ref[...], k_ref[lo:hi, :]], axis=0)
            vv = jnp.concatenate([vh_ref[...], v_ref[lo:hi, :]], axis=0)
        else:
            kk = k_ref[lo - WINDOW:hi, :]
            vv = v_ref[lo - WINDOW:hi, :]
        for kv in range(N_KV_HEADS):
            kd = kk[:, kv * LANES:(kv + 1) * LANES]
            vd = vv[:, kv * LANES:(kv + 1) * LANES]
            c0 = kv * Q_PER_KV * HEAD_DIM
            q2 = jnp.concatenate([q_ref[lo:hi, c0:c0 + LANES], q_ref[lo:hi, c0 + LANES:c0 + 2 * LANES]],
                                 axis=0)
            out = None
            for side in range(2):
                keep = low_lanes if side == 0 else jnp.logical_not(low_lanes)
                ks = jnp.where(keep, kd, zero)
                vs = jnp.where(keep, vd, zero)
                s = lax.dot_general(q2, ks, NT_DIMS, preferred_element_type=F32)
                s_prev = s[:, :WINDOW]
                if j == 0:
                    s_prev = jnp.where(first_tile, -jnp.inf, s_prev)
                sf = jnp.where(use_prev, s_prev, s[:, WINDOW:])
                sink = jnp.where(top, sinks_ref[kv * Q_PER_KV + side], sinks_ref[kv * Q_PER_KV + 2 + side])
                m = jnp.maximum(jnp.max(sf, axis=-1, keepdims=True), sink)
                p = jnp.exp(sf - m)
                den = jnp.sum(p, axis=-1, keepdims=True) + jnp.exp(sink - m)
                pn = p / den
                pcat = jnp.concatenate([jnp.where(use_prev, pn, 0.0), jnp.where(use_prev, 0.0, pn)],
                                       axis=1).astype(BF16)
                o = jnp.dot(pcat, vs, preferred_element_type=F32)
                out = o if out is None else out + o
            acc_ref[:, c0:c0 + LANES] = out[:WINDOW]
            acc_ref[:, c0 + LANES:c0 + 2 * LANES] = out[WINDOW:]
        o_ref[lo:hi, :] = _rms(acc_ref[...], g_ref[...]).astype(BF16)


def _attention(q, k, v, sinks, g_attn_out, tq):
    T = q.shape[0]
    nt = SEQ // tq
    per = tq // WINDOW
    row = lambda b, i: (b * nt + i, 0)
    halo = lambda b, i: (jnp.maximum(b * (SEQ // WINDOW) + i * per - 1, 0), 0)
    return pl.pallas_call(
        _attn_kernel,
        grid=(BATCH, nt),
        in_specs=[
            pl.BlockSpec(memory_space=pltpu.SMEM),
            pl.BlockSpec((tq, D_ATTN), row),
            pl.BlockSpec((tq, D_KV_DUP), row),
            pl.BlockSpec((tq, D_KV_DUP), row),
            pl.BlockSpec((WINDOW, D_KV_DUP), halo),
            pl.BlockSpec((WINDOW, D_KV_DUP), halo),
            pl.BlockSpec((1, D_ATTN), lambda b, i: (0, 0)),
        ],
        out_specs=pl.BlockSpec((tq, D_ATTN), row),
        out_shape=jax.ShapeDtypeStruct((T, D_ATTN), BF16),
        scratch_shapes=[pltpu.VMEM((WINDOW, D_ATTN), F32)],
        compiler_params=pltpu.CompilerParams(dimension_semantics=("parallel", "parallel")),
        name="attention",
    )(sinks, q, k, v, k, v, g_attn_out)


def _ssm_kernel(u_ref, bblk_ref, cblk_ref, lre_ref, lim_ref, dskip_ref, wglu_ref, bglu_ref, g_ref,
                o_ref, s_ref, y_ref, carry_ref):
    tl = u_ref.shape[0]
    rows = tl * BATCH

    @pl.when(pl.program_id(0) == 0)
    def _():
        carry_ref[...] = jnp.zeros_like(carry_ref)

    u = u_ref[...].reshape(rows, D_SSM)
    ub = u.astype(BF16)
    for gt in range(N_GROUP_TILES):
        buf = gt % 2
        ch = slice(gt * TILE_CH, (gt + 1) * TILE_CH)
        s_ref[buf] = jnp.dot(ub[:, ch], bblk_ref[gt], preferred_element_type=F32)
        for c in range(N_CHUNKS):
            re_cols = slice(c * LANES, (c + 1) * LANES)
            im_cols = slice(TILE_ST + c * LANES, TILE_ST + (c + 1) * LANES)
            lr = lre_ref[gt, :, re_cols]
            li = lim_ref[gt, :, re_cols]
            sr = carry_ref[gt, :, re_cols]
            si = carry_ref[gt, :, im_cols]
            for t in range(tl):
                r = slice(t * BATCH, (t + 1) * BATCH)
                nr = lr * sr - li * si + s_ref[buf, r, re_cols]
                ni = lr * si + li * sr + s_ref[buf, r, im_cols]
                s_ref[buf, r, re_cols] = nr
                s_ref[buf, r, im_cols] = ni
                sr, si = nr, ni
            carry_ref[gt, :, re_cols] = sr
            carry_ref[gt, :, im_cols] = si
        y = jnp.dot(s_ref[buf].astype(BF16), cblk_ref[gt], preferred_element_type=F32)
        y_ref[:, ch] = y + dskip_ref[:, ch] * u[:, ch]
    z = jax.nn.gelu(y_ref[...])
    gate = jax.nn.sigmoid(jnp.dot(z.astype(BF16), wglu_ref[...], preferred_element_type=F32)
                          + bglu_ref[...])
    o_ref[...] = _rms(z * gate, g_ref[...]).reshape(tl, BATCH, D_SSM)


def _ssm(u3, bblk, cblk, lam_re, lam_im, d_skip, w_glu_bf, b_glu, g_ssm_out, tl):
    rows = tl * BATCH
    c2 = lambda i: (0, 0)
    c3 = lambda i: (0, 0, 0)
    return pl.pallas_call(
        _ssm_kernel,
        grid=(SEQ // tl,),
        in_specs=[
            pl.BlockSpec((tl, BATCH, D_SSM), lambda i: (i, 0, 0)),
            pl.BlockSpec((N_GROUP_TILES, TILE_CH, 2 * TILE_ST), c3),
            pl.BlockSpec((N_GROUP_TILES, 2 * TILE_ST, TILE_CH), c3),
            pl.BlockSpec((N_GROUP_TILES, BATCH, TILE_ST), c3),
            pl.BlockSpec((N_GROUP_TILES, BATCH, TILE_ST), c3),
            pl.BlockSpec((1, D_SSM), c2),
            pl.BlockSpec((D_SSM, D_SSM), c2),
            pl.BlockSpec((1, D_SSM), c2),
            pl.BlockSpec((1, D_SSM), c2),
        ],
        out_specs=pl.BlockSpec((tl, BATCH, D_SSM), lambda i: (i, 0, 0)),
        out_shape=jax.ShapeDtypeStruct((SEQ, BATCH, D_SSM), F32),
        scratch_shapes=[
            pltpu.VMEM((2, rows, 2 * TILE_ST), F32),
            pltpu.VMEM((rows, D_SSM), F32),
            pltpu.VMEM((N_GROUP_TILES, BATCH, 2 * TILE_ST), F32),
        ],
        compiler_params=pltpu.CompilerParams(
            dimension_semantics=("arbitrary",), vmem_limit_bytes=52 * 2**20),
        name="ssm",
    )(u3, bblk, cblk, lam_re, lam_im, d_skip, w_glu_bf, b_glu, g_ssm_out)


def _outproj_kernel(attn_ref, ssm_ref, x_ref, wo_ref, gpost_ref, gpre_ref, h_ref, hn_ref):
    mix = jnp.dot(attn_ref[...], wo_ref[:D_ATTN, :], preferred_element_type=F32)
    mix = mix + jnp.dot(ssm_ref[...].astype(BF16), wo_ref[D_ATTN:, :], preferred_element_type=F32)
    h = x_ref[...] + _rms(mix, gpost_ref[...])
    h_ref[...] = h
    hn_ref[...] = _rms(h, gpre_ref[...]).astype(BF16)


def _out_proj(attn_n, ssm2, x2, w_o_bf, g_post_mix, g_pre_ffn, tm):
    T = x2.shape[0]
    nt = SEQ // tm
    row = lambda b, i: (b * nt + i, 0)
    const = lambda b, i: (0, 0)
    return pl.pallas_call(
        _outproj_kernel,
        grid=(BATCH, nt),
        in_specs=[
            pl.BlockSpec((tm, D_ATTN), row),
            pl.BlockSpec((tm, D_SSM), lambda b, i: (i, b)),
            pl.BlockSpec((tm, D_MODEL), row),
            pl.BlockSpec((D_MODEL, D_MODEL), const),
            pl.BlockSpec((1, D_MODEL), const),
            pl.BlockSpec((1, D_MODEL), const),
        ],
        out_specs=[pl.BlockSpec((tm, D_MODEL), row), pl.BlockSpec((tm, D_MODEL), row)],
        out_shape=[jax.ShapeDtypeStruct((T, D_MODEL), F32), jax.ShapeDtypeStruct((T, D_MODEL), BF16)],
        compiler_params=pltpu.CompilerParams(
            dimension_semantics=("parallel", "parallel"), vmem_limit_bytes=52 * 2**20),
        name="out_proj",
    )(attn_n, ssm2, x2, w_o_bf, g_post_mix, g_pre_ffn)


def _ffn_kernel(hn_ref, h_ref, wg_ref, wu_ref, wd_ref, g_ref, o_ref, acc_ref):
    j = pl.program_id(1)

    @pl.when(j == 0)
    def _():
        acc_ref[...] = jnp.zeros_like(acc_ref)

    hn = hn_ref[...]
    gate = jnp.dot(hn, wg_ref[...], preferred_element_type=F32)
    up = jnp.dot(hn, wu_ref[...], preferred_element_type=F32)
    hid = (jax.nn.silu(gate) * up).astype(BF16)
    acc_ref[...] += jnp.dot(hid, wd_ref[...], preferred_element_type=F32)

    @pl.when(j == pl.num_programs(1) - 1)
    def _():
        o_ref[...] = h_ref[...] + _rms(acc_ref[...], g_ref[...])


def _ffn(hn, h, wg_bf, wu_bf, wd_bf, g_post_ffn, tm, tf):
    T = h.shape[0]
    row = lambda i, j: (i, 0)
    return pl.pallas_call(
        _ffn_kernel,
        grid=(T // tm, D_FF // tf),
        in_specs=[
            pl.BlockSpec((tm, D_MODEL), row),
            pl.BlockSpec((tm, D_MODEL), row),
            pl.BlockSpec((D_MODEL, tf), lambda i, j: (0, j)),
            pl.BlockSpec((D_MODEL, tf), lambda i, j: (0, j)),
            pl.BlockSpec((tf, D_MODEL), lambda i, j: (j, 0)),
            pl.BlockSpec((1, D_MODEL), lambda i, j: (0, 0)),
        ],
        out_specs=pl.BlockSpec((tm, D_MODEL), row),
        out_shape=jax.ShapeDtypeStruct((T, D_MODEL), F32),
        scratch_shapes=[pltpu.VMEM((tm, D_MODEL), F32)],
        compiler_params=pltpu.CompilerParams(
            dimension_semantics=("parallel", "arbitrary"), vmem_limit_bytes=52 * 2**20),
        name="ffn",
    )(hn, h, wg_bf, wu_bf, wd_bf, g_post_ffn)


def kernel(x, positions, g_pre_mix, w_in, sinks, a_re, a_im, log_dt, b_re, b_im, c_re, c_im, d_skip,
           w_glu, b_glu, g_attn_out, g_ssm_out, w_o, g_post_mix, g_pre_ffn, w_gate, w_up, w_down,
           g_post_ffn):
    depth = w_in.shape[0]
    T = BATCH * SEQ
    h = x.reshape(T, D_MODEL)
    pos2 = positions.reshape(T, 1)
    for i in range(depth):
        lam_re, lam_im, bblk, cblk = _ssm_prep(a_re[i], a_im[i], log_dt[i], b_re[i], b_im[i],
                                               c_re[i], c_im[i])
        q, k, v, u = _in_proj(h, pos2, g_pre_mix[i][None, :], w_in[i].astype(BF16), tm=512)
        attn_n = _attention(q, k, v, sinks[i], g_attn_out[i][None, :], tq=256)
        ssm_n = _ssm(u.reshape(SEQ, BATCH, D_SSM), bblk, cblk, lam_re, lam_im,
                     d_skip[i].reshape(1, D_SSM), w_glu[i].astype(BF16), b_glu[i][None, :],
                     g_ssm_out[i][None, :], tl=64)
        h, hn = _out_proj(attn_n, ssm_n.reshape(SEQ, BATCH * D_SSM), h, w_o[i].astype(BF16),
                          g_post_mix[i][None, :], g_pre_ffn[i][None, :], tm=512)
        h = _ffn(hn, h, w_gate[i].astype(BF16), w_up[i].astype(BF16), w_down[i].astype(BF16),
                 g_post_ffn[i][None, :], tm=512, tf=512)
    return h.reshape(BATCH, SEQ, D_MODEL)
```

```python
import functools
import math

import jax
import jax.numpy as jnp
import numpy as np
from jax import lax
from jax.experimental import pallas as pl
from jax.experimental.pallas import tpu as pltpu

D_MODEL = 2048
BATCH = 8
SEQ = 2048
HEAD_DIM = 64
D_ATTN = 1024
N_Q_HEADS = 16
N_KV_HEADS = 4
Q_PER_KV = 4
D_KV = 256
WINDOW = 128
ROPE_THETA = 10000.0
D_SSM = 1024
SSM_GROUP = 16
N_SSM_GROUPS = 64
SSM_STATE = 64
D_IN = D_ATTN + 2 * D_KV + D_SSM
D_FF = 5632
RMS_EPS = 1e-6

LANES = 128
SUBLANES = 8
GROUPS_PER_TILE = 16
N_GROUP_TILES = N_SSM_GROUPS // GROUPS_PER_TILE
TILE_CH = GROUPS_PER_TILE * SSM_GROUP
TILE_ST = GROUPS_PER_TILE * SSM_STATE
N_CHUNKS = TILE_ST // LANES
D_KV_DUP = 2 * N_KV_HEADS * HEAD_DIM
ROW_SUB = 256
N_CAST_STEPS = 32

F32 = jnp.float32
BF16 = jnp.bfloat16
NT_DIMS = (((1,), (1,)), ((), ()))


def _rms(x, g):
    ms = jnp.mean(x * x, axis=-1, keepdims=True)
    return x * lax.rsqrt(ms + RMS_EPS) * g


def _split3(x):
    x1 = x.astype(BF16)
    r1 = x - x1.astype(F32)
    x2 = r1.astype(BF16)
    x3 = (r1 - x2.astype(F32)).astype(BF16)
    return x1, x2, x3


def _ssm_prep_kernel(ar_ref, ai_ref, ldt_ref, br_ref, bi_ref, cr_ref, ci_ref, ep_ref, en_ref,
                     lre_ref, lim_ref, bblk_ref, cblk_ref):
    ar = ar_ref[0]
    ai = ai_ref[0]
    dt = jnp.exp(ldt_ref[0])
    mag = jnp.exp(ar * dt)
    lam_re = mag * jnp.cos(ai * dt)
    lam_im = mag * jnp.sin(ai * dt)
    den = ar * ar + ai * ai
    nr = lam_re - 1.0
    ni = lam_im
    f_re = (nr * ar + ni * ai) / den
    f_im = (ni * ar - nr * ai) / den
    lre_ref[0] = jnp.broadcast_to(lam_re, (BATCH, TILE_ST))
    lim_ref[0] = jnp.broadcast_to(lam_im, (BATCH, TILE_ST))

    ep = ep_ref[...]

    def spread_b(b):
        return sum(lax.dot_general(ep, piece, NT_DIMS, preferred_element_type=F32)
                   for piece in _split3(b))

    pb_re = spread_b(br_ref[0])
    pb_im = spread_b(bi_ref[0])
    row = lax.broadcasted_iota(jnp.int32, (TILE_CH, TILE_ST), 0)
    col = lax.broadcasted_iota(jnp.int32, (TILE_CH, TILE_ST), 1)
    diag = (row // SSM_GROUP) == (col // SSM_STATE)
    bblk_ref[0, :, :TILE_ST] = jnp.where(diag, f_re * pb_re - f_im * pb_im, 0.0).astype(BF16)
    bblk_ref[0, :, TILE_ST:] = jnp.where(diag, f_re * pb_im + f_im * pb_re, 0.0).astype(BF16)

    en = en_ref[...]
    row = lax.broadcasted_iota(jnp.int32, (TILE_ST, TILE_CH), 0)
    col = lax.broadcasted_iota(jnp.int32, (TILE_ST, TILE_CH), 1)
    diag = (row // SSM_STATE) == (col // SSM_GROUP)
    pc_re = lax.dot_general(en, cr_ref[0].astype(BF16), NT_DIMS, preferred_element_type=F32)
    pc_im = lax.dot_general(en, ci_ref[0].astype(BF16), NT_DIMS, preferred_element_type=F32)
    cblk_ref[0, :TILE_ST, :] = jnp.where(diag, pc_re, 0.0).astype(BF16)
    cblk_ref[0, TILE_ST:, :] = jnp.where(diag, -pc_im, 0.0).astype(BF16)


def _ssm_prep(a_re, a_im, log_dt, b_re, b_im, c_re, c_im):
    nt = N_GROUP_TILES
    lanes3 = lambda a: a.reshape(nt, 1, TILE_ST)
    ldt = jnp.broadcast_to(log_dt[:, None], (N_SSM_GROUPS, SSM_STATE))
    ep = jnp.asarray(np.tile(np.eye(SSM_GROUP, dtype=np.float32), (GROUPS_PER_TILE, 1)), BF16)
    en = jnp.asarray(np.tile(np.eye(SSM_STATE, dtype=np.float32), (GROUPS_PER_TILE, 1)), BF16)
    t3 = lambda t: (t, 0, 0)
    c2 = lambda t: (0, 0)
    return pl.pallas_call(
        _ssm_prep_kernel,
        grid=(nt,),
        in_specs=[
            pl.BlockSpec((1, 1, TILE_ST), t3),
            pl.BlockSpec((1, 1, TILE_ST), t3),
            pl.BlockSpec((1, 1, TILE_ST), t3),
            pl.BlockSpec((1, TILE_ST, SSM_GROUP), t3),
            pl.BlockSpec((1, TILE_ST, SSM_GROUP), t3),
            pl.BlockSpec((1, TILE_CH, SSM_STATE), t3),
            pl.BlockSpec((1, TILE_CH, SSM_STATE), t3),
            pl.BlockSpec((TILE_CH, SSM_GROUP), c2),
            pl.BlockSpec((TILE_ST, SSM_STATE), c2),
        ],
        out_specs=[
            pl.BlockSpec((1, BATCH, TILE_ST), t3),
            pl.BlockSpec((1, BATCH, TILE_ST), t3),
            pl.BlockSpec((1, TILE_CH, 2 * TILE_ST), t3),
            pl.BlockSpec((1, 2 * TILE_ST, TILE_CH), t3),
        ],
        out_shape=[
            jax.ShapeDtypeStruct((nt, BATCH, TILE_ST), F32),
            jax.ShapeDtypeStruct((nt, BATCH, TILE_ST), F32),
            jax.ShapeDtypeStruct((nt, TILE_CH, 2 * TILE_ST), BF16),
            jax.ShapeDtypeStruct((nt, 2 * TILE_ST, TILE_CH), BF16),
        ],
        compiler_params=pltpu.CompilerParams(dimension_semantics=("parallel",)),
        name="ssm_prep",
    )(lanes3(a_re), lanes3(a_im), lanes3(ldt),
      b_re.reshape(nt, TILE_ST, SSM_GROUP), b_im.reshape(nt, TILE_ST, SSM_GROUP),
      c_re.reshape(nt, TILE_CH, SSM_STATE), c_im.reshape(nt, TILE_CH, SSM_STATE), ep, en)


def _inproj_kernel(x_ref, pos_ref, g_ref, invf_ref, sgn_ref, w_ref, wo_ref, wglu_ref,
                   q_ref, k_ref, v_ref, u_ref, wo_bf_ref, wglu_bf_ref):
    tm = x_ref.shape[0]
    wo_bf_ref[...] = wo_ref[...].astype(BF16)
    wglu_bf_ref[...] = wglu_ref[...].astype(BF16)

    lane = lax.broadcasted_iota(jnp.int32, (ROW_SUB, LANES), 1)
    first_half = (lane & (HEAD_DIM // 2)) == 0
    first_head = lane < HEAD_DIM
    scale = 1.0 / math.sqrt(HEAD_DIM)
    for r in range(tm // ROW_SUB):
        rs = slice(r * ROW_SUB, (r + 1) * ROW_SUB)
        xn = _rms(x_ref[rs, :], g_ref[...]).astype(BF16)
        ang = pos_ref[rs, :].astype(F32) * invf_ref[...]
        cos = jnp.cos(ang)
        sin = jnp.sin(ang) * sgn_ref[...]

        def rotary(t):
            partner = jnp.where(first_half,
                                pltpu.roll(t, LANES - HEAD_DIM // 2, 1),
                                pltpu.roll(t, HEAD_DIM // 2, 1))
            return t * cos + partner * sin

        def store_dup(ref, c, t):
            swapped = pltpu.roll(t, HEAD_DIM, 1)
            ref[rs, (2 * c) * LANES:(2 * c + 1) * LANES] = jnp.where(first_head, t, swapped).astype(BF16)
            ref[rs, (2 * c + 1) * LANES:(2 * c + 2) * LANES] = jnp.where(first_head, swapped, t).astype(BF16)

        for j in range(D_ATTN // 256):
            p = jnp.dot(xn, w_ref[:, j * 256:(j + 1) * 256], preferred_element_type=F32)
            for c in range(2):
                col = j * 256 + c * LANES
                q_ref[rs, col:col + LANES] = (rotary(p[:, c * LANES:(c + 1) * LANES]) * scale).astype(BF16)
        p = jnp.dot(xn, w_ref[:, D_ATTN:D_ATTN + D_KV], preferred_element_type=F32)
        for c in range(2):
            store_dup(k_ref, c, rotary(p[:, c * LANES:(c + 1) * LANES]))
        p = jnp.dot(xn, w_ref[:, D_ATTN + D_KV:D_ATTN + 2 * D_KV], preferred_element_type=F32)
        for c in range(2):
            store_dup(v_ref, c, p[:, c * LANES:(c + 1) * LANES])
        for j in range(D_SSM // 256):
            col = D_ATTN + 2 * D_KV + j * 256
            u_ref[rs, j * 256:(j + 1) * 256] = jnp.dot(xn, w_ref[:, col:col + 256],
                                                        preferred_element_type=F32)


def _in_proj(x2, pos2, g_pre_mix, w_in_bf, w_o, w_glu, tm):
    T = x2.shape[0]
    nt = SEQ // tm
    assert BATCH * nt == N_CAST_STEPS
    wo_slab = (D_MODEL // N_CAST_STEPS, D_MODEL)
    wglu_slab = (D_SSM // N_CAST_STEPS, D_SSM)
    half = HEAD_DIM // 2
    invf = ROPE_THETA ** (-np.arange(half, dtype=np.float32) / half)
    invf = np.tile(invf.astype(np.float32), LANES // half)[None, :]
    sgn = np.tile(np.concatenate([-np.ones(half, np.float32), np.ones(half, np.float32)]),
                  LANES // HEAD_DIM)[None, :]
    row = lambda b, i: (b * nt + i, 0)
    const = lambda b, i: (0, 0)
    return pl.pallas_call(
        _inproj_kernel,
        grid=(BATCH, nt),
        in_specs=[
            pl.BlockSpec((tm, D_MODEL), row),
            pl.BlockSpec((tm, 1), row),
            pl.BlockSpec((1, D_MODEL), const),
            pl.BlockSpec((1, LANES), const),
            pl.BlockSpec((1, LANES), const),
            pl.BlockSpec((D_MODEL, D_IN), const),
            pl.BlockSpec(wo_slab, row),
            pl.BlockSpec(wglu_slab, row),
        ],
        out_specs=[
            pl.BlockSpec((tm, D_ATTN), row),
            pl.BlockSpec((tm, D_KV_DUP), row),
            pl.BlockSpec((tm, D_KV_DUP), row),
            pl.BlockSpec((tm, D_SSM), lambda b, i: (i, b)),
            pl.BlockSpec(wo_slab, row),
            pl.BlockSpec(wglu_slab, row),
        ],
        out_shape=[
            jax.ShapeDtypeStruct((T, D_ATTN), BF16),
            jax.ShapeDtypeStruct((T, D_KV_DUP), BF16),
            jax.ShapeDtypeStruct((T, D_KV_DUP), BF16),
            jax.ShapeDtypeStruct((SEQ, BATCH * D_SSM), F32),
            jax.ShapeDtypeStruct((D_MODEL, D_MODEL), BF16),
            jax.ShapeDtypeStruct((D_SSM, D_SSM), BF16),
        ],
        compiler_params=pltpu.CompilerParams(
            dimension_semantics=("parallel", "parallel"), vmem_limit_bytes=52 * 2**20),
        name="in_proj",
    )(x2, pos2, g_pre_mix, jnp.asarray(invf), jnp.asarray(sgn), w_in_bf, w_o, w_glu)


def _attn_kernel(sinks_ref, q_ref, k_ref, v_ref, kh_ref, vh_ref, g_ref, o_ref, acc_ref):
    tq = q_ref.shape[0]
    first_tile = pl.program_id(1) == 0
    two = 2 * WINDOW
    row = lax.broadcasted_iota(jnp.int32, (two, WINDOW), 0) % WINDOW
    col = lax.broadcasted_iota(jnp.int32, (two, WINDOW), 1)
    use_prev = col > row
    top = lax.broadcasted_iota(jnp.int32, (two, 1), 0) < WINDOW
    low_lanes = lax.broadcasted_iota(jnp.int32, (two, LANES), 1) < HEAD_DIM
    zero = jnp.zeros((), BF16)
    for j in range(tq // WINDOW):
        lo, hi = j * WINDOW, (j + 1) * WINDOW
        if j == 0:
            kk = jnp.concatenate([kh_ref[...], k_ref[lo:hi, :]], axis=0)
            vv = jnp.concatenate([vh_ref[...], v_ref[lo:hi, :]], axis=0)
        else:
            kk = k_ref[lo - WINDOW:hi, :]
            vv = v_ref[lo - WINDOW:hi, :]
        for kv in range(N_KV_HEADS):
            kd = kk[:, kv * LANES:(kv + 1) * LANES]
            vd = vv[:, kv * LANES:(kv + 1) * LANES]
            c0 = kv * Q_PER_KV * HEAD_DIM
            q2 = jnp.concatenate([q_ref[lo:hi, c0:c0 + LANES], q_ref[lo:hi, c0 + LANES:c0 + 2 * LANES]],
                                 axis=0)
            out = None
            for side in range(2):
                keep = low_lanes if side == 0 else jnp.logical_not(low_lanes)
                ks = jnp.where(keep, kd, zero)
                vs = jnp.where(keep, vd, zero)
                s = lax.dot_general(q2, ks, NT_DIMS, preferred_element_type=F32)
                s_prev = s[:, :WINDOW]
                if j == 0:
                    s_prev = jnp.where(first_tile, -jnp.inf, s_prev)
                sf = jnp.where(use_prev, s_prev, s[:, WINDOW:])
                sink = jnp.where(top, sinks_ref[kv * Q_PER_KV + side], sinks_ref[kv * Q_PER_KV + 2 + side])
                m = jnp.maximum(jnp.max(sf, axis=-1, keepdims=True), sink)
                p = jnp.exp(sf - m)
                den = jnp.sum(p, axis=-1, keepdims=True) + jnp.exp(sink - m)
                pn = p / den
                pcat = jnp.concatenate([jnp.where(use_prev, pn, 0.0), jnp.where(use_prev, 0.0, pn)],
                                       axis=1).astype(BF16)
                o = jnp.dot(pcat, vs, preferred_element_type=F32)
                out = o if out is None else out + o
            acc_ref[:, c0:c0 + LANES] = out[:WINDOW]
            acc_ref[:, c0 + LANES:c0 + 2 * LANES] = out[WINDOW:]
        o_ref[lo:hi, :] = _rms(acc_ref[...], g_ref[...]).astype(BF16)


def _attention(q, k, v, sinks, g_attn_out, tq):
    T = q.shape[0]
    nt = SEQ // tq
    per = tq // WINDOW
    row = lambda b, i: (b * nt + i, 0)
    halo = lambda b, i: (jnp.maximum(b * (SEQ // WINDOW) + i * per - 1, 0), 0)
    return pl.pallas_call(
        _attn_kernel,
        grid=(BATCH, nt),
        in_specs=[
            pl.BlockSpec(memory_space=pltpu.SMEM),
            pl.BlockSpec((tq, D_ATTN), row),
            pl.BlockSpec((tq, D_KV_DUP), row),
            pl.BlockSpec((tq, D_KV_DUP), row),
            pl.BlockSpec((WINDOW, D_KV_DUP), halo),
            pl.BlockSpec((WINDOW, D_KV_DUP), halo),
            pl.BlockSpec((1, D_ATTN), lambda b, i: (0, 0)),
        ],
        out_specs=pl.BlockSpec((tq, D_ATTN), row),
        out_shape=jax.ShapeDtypeStruct((T, D_ATTN), BF16),
        scratch_shapes=[pltpu.VMEM((WINDOW, D_ATTN), F32)],
        compiler_params=pltpu.CompilerParams(dimension_semantics=("parallel", "parallel")),
        name="attention",
    )(sinks, q, k, v, k, v, g_attn_out)


def _ssm_kernel(u_ref, bblk_ref, cblk_ref, lre_ref, lim_ref, dskip_ref, wglu_ref, bglu_ref, g_ref,
                wg_ref, wu_ref, wd_ref, o_ref, wg_bf_ref, wu_bf_ref, wd_bf_ref, s_ref, y_ref, carry_ref):
    tl = u_ref.shape[0]
    rows = tl * BATCH
    wg_bf_ref[...] = wg_ref[...].astype(BF16)
    wu_bf_ref[...] = wu_ref[...].astype(BF16)
    wd_bf_ref[...] = wd_ref[...].astype(BF16)

    @pl.when(pl.program_id(0) == 0)
    def _():
        carry_ref[...] = jnp.zeros_like(carry_ref)

    u = u_ref[...].reshape(rows, D_SSM)
    ub = u.astype(BF16)
    for gt in range(N_GROUP_TILES):
        buf = gt % 2
        ch = slice(gt * TILE_CH, (gt + 1) * TILE_CH)
        s_ref[buf] = jnp.dot(ub[:, ch], bblk_ref[gt], preferred_element_type=F32)
        for c in range(N_CHUNKS):
            re_cols = slice(c * LANES, (c + 1) * LANES)
            im_cols = slice(TILE_ST + c * LANES, TILE_ST + (c + 1) * LANES)
            lr = lre_ref[gt, :, re_cols]
            li = lim_ref[gt, :, re_cols]
            sr = carry_ref[gt, :, re_cols]
            si = carry_ref[gt, :, im_cols]
            for t in range(tl):
                r = slice(t * BATCH, (t + 1) * BATCH)
                nr = lr * sr - li * si + s_ref[buf, r, re_cols]
                ni = lr * si + li * sr + s_ref[buf, r, im_cols]
                s_ref[buf, r, re_cols] = nr
                s_ref[buf, r, im_cols] = ni
                sr, si = nr, ni
            carry_ref[gt, :, re_cols] = sr
            carry_ref[gt, :, im_cols] = si
        y = jnp.dot(s_ref[buf].astype(BF16), cblk_ref[gt], preferred_element_type=F32)
        y_ref[:, ch] = y + dskip_ref[:, ch] * u[:, ch]
    z = jax.nn.gelu(y_ref[...])
    gate = jax.nn.sigmoid(jnp.dot(z.astype(BF16), wglu_ref[...], preferred_element_type=F32)
                          + bglu_ref[...])
    o_ref[...] = _rms(z * gate, g_ref[...]).reshape(tl, BATCH, D_SSM)


def _ssm(u3, bblk, cblk, lam_re, lam_im, d_skip, w_glu_bf, b_glu, g_ssm_out, w_gate, w_up, w_down, tl):
    rows = tl * BATCH
    assert SEQ // tl == N_CAST_STEPS
    c2 = lambda i: (0, 0)
    c3 = lambda i: (0, 0, 0)
    slab = lambda i: (i, 0)
    up_slab = (D_MODEL // N_CAST_STEPS, D_FF)
    down_slab = (D_FF // N_CAST_STEPS, D_MODEL)
    return pl.pallas_call(
        _ssm_kernel,
        grid=(SEQ // tl,),
        in_specs=[
            pl.BlockSpec((tl, BATCH, D_SSM), lambda i: (i, 0, 0)),
            pl.BlockSpec((N_GROUP_TILES, TILE_CH, 2 * TILE_ST), c3),
            pl.BlockSpec((N_GROUP_TILES, 2 * TILE_ST, TILE_CH), c3),
            pl.BlockSpec((N_GROUP_TILES, BATCH, TILE_ST), c3),
            pl.BlockSpec((N_GROUP_TILES, BATCH, TILE_ST), c3),
            pl.BlockSpec((1, D_SSM), c2),
            pl.BlockSpec((D_SSM, D_SSM), c2),
            pl.BlockSpec((1, D_SSM), c2),
            pl.BlockSpec((1, D_SSM), c2),
            pl.BlockSpec(up_slab, slab),
            pl.BlockSpec(up_slab, slab),
            pl.BlockSpec(down_slab, slab),
        ],
        out_specs=[
            pl.BlockSpec((tl, BATCH, D_SSM), lambda i: (i, 0, 0)),
            pl.BlockSpec(up_slab, slab),
            pl.BlockSpec(up_slab, slab),
            pl.BlockSpec(down_slab, slab),
        ],
        out_shape=[
            jax.ShapeDtypeStruct((SEQ, BATCH, D_SSM), F32),
            jax.ShapeDtypeStruct((D_MODEL, D_FF), BF16),
            jax.ShapeDtypeStruct((D_MODEL, D_FF), BF16),
            jax.ShapeDtypeStruct((D_FF, D_MODEL), BF16),
        ],
        scratch_shapes=[
            pltpu.VMEM((2, rows, 2 * TILE_ST), F32),
            pltpu.VMEM((rows, D_SSM), F32),
            pltpu.VMEM((N_GROUP_TILES, BATCH, 2 * TILE_ST), F32),
        ],
        compiler_params=pltpu.CompilerParams(
            dimension_semantics=("arbitrary",), vmem_limit_bytes=52 * 2**20),
        name="ssm",
    )(u3, bblk, cblk, lam_re, lam_im, d_skip, w_glu_bf, b_glu, g_ssm_out, w_gate, w_up, w_down)


def _outproj_kernel(attn_ref, ssm_ref, x_ref, wo_ref, gpost_ref, gpre_ref, h_ref, hn_ref):
    tm = x_ref.shape[0]
    for r in range(tm // ROW_SUB):
        rs = slice(r * ROW_SUB, (r + 1) * ROW_SUB)
        mix = jnp.dot(attn_ref[rs, :], wo_ref[:D_ATTN, :], preferred_element_type=F32)
        mix = mix + jnp.dot(ssm_ref[rs, :].astype(BF16), wo_ref[D_ATTN:, :],
                            preferred_element_type=F32)
        h = x_ref[rs, :] + _rms(mix, gpost_ref[...])
        h_ref[rs, :] = h
        hn_ref[rs, :] = _rms(h, gpre_ref[...]).astype(BF16)


def _out_proj(attn_n, ssm2, x2, w_o_bf, g_post_mix, g_pre_ffn, tm):
    T = x2.shape[0]
    nt = SEQ // tm
    row = lambda b, i: (b * nt + i, 0)
    const = lambda b, i: (0, 0)
    return pl.pallas_call(
        _outproj_kernel,
        grid=(BATCH, nt),
        in_specs=[
            pl.BlockSpec((tm, D_ATTN), row),
            pl.BlockSpec((tm, D_SSM), lambda b, i: (i, b)),
            pl.BlockSpec((tm, D_MODEL), row),
            pl.BlockSpec((D_MODEL, D_MODEL), const),
            pl.BlockSpec((1, D_MODEL), const),
            pl.BlockSpec((1, D_MODEL), const),
        ],
        out_specs=[pl.BlockSpec((tm, D_MODEL), row), pl.BlockSpec((tm, D_MODEL), row)],
        out_shape=[jax.ShapeDtypeStruct((T, D_MODEL), F32), jax.ShapeDtypeStruct((T, D_MODEL), BF16)],
        compiler_params=pltpu.CompilerParams(
            dimension_semantics=("parallel", "parallel"), vmem_limit_bytes=52 * 2**20),
        name="out_proj",
    )(attn_n, ssm2, x2, w_o_bf, g_post_mix, g_pre_ffn)


def _ffn_kernel(hn_ref, h_ref, wg_ref, wu_ref, wd_ref, g_ref, o_ref, acc_ref, *, nf):
    j = pl.program_id(1)
    te = o_ref.shape[0]

    def partial():
        hn = hn_ref[...]
        gate = jnp.dot(hn, wg_ref[...], preferred_element_type=F32)
        up = jnp.dot(hn, wu_ref[...], preferred_element_type=F32)
        hid = (jax.nn.silu(gate) * up).astype(BF16)
        return jnp.dot(hid, wd_ref[...], preferred_element_type=F32)

    @pl.when(j == 0)
    def _():
        acc_ref[...] = partial()

    @pl.when(jnp.logical_and(j > 0, j < nf))
    def _():
        acc_ref[...] += partial()

    @pl.when(j >= nf)
    def _():
        r = pl.multiple_of((j - nf) * te, te)
        o_ref[...] = h_ref[...] + _rms(acc_ref[pl.ds(r, te), :], g_ref[...])


def _ffn(hn, h, wg_bf, wu_bf, wd_bf, g_post_ffn, tm, tf, te):
    T = h.shape[0]
    nf = D_FF // tf
    sub = tm // te
    chunk = lambda j: jnp.minimum(j, nf - 1)
    out_row = lambda i, j: (i * sub + jnp.clip(j - nf, 0, sub - 1), 0)
    return pl.pallas_call(
        functools.partial(_ffn_kernel, nf=nf),
        grid=(T // tm, nf + sub),
        in_specs=[
            pl.BlockSpec((tm, D_MODEL), lambda i, j: (i, 0)),
            pl.BlockSpec((te, D_MODEL), out_row),
            pl.BlockSpec((D_MODEL, tf), lambda i, j: (0, chunk(j))),
            pl.BlockSpec((D_MODEL, tf), lambda i, j: (0, chunk(j))),
            pl.BlockSpec((tf, D_MODEL), lambda i, j: (chunk(j), 0)),
            pl.BlockSpec((1, D_MODEL), lambda i, j: (0, 0)),
        ],
        out_specs=pl.BlockSpec((te, D_MODEL), out_row),
        out_shape=jax.ShapeDtypeStruct((T, D_MODEL), F32),
        scratch_shapes=[pltpu.VMEM((tm, D_MODEL), F32)],
        compiler_params=pltpu.CompilerParams(
            dimension_semantics=("parallel", "arbitrary"), vmem_limit_bytes=58 * 2**20),
        name="ffn",
    )(hn, h, wg_bf, wu_bf, wd_bf, g_post_ffn)


def kernel(x, positions, g_pre_mix, w_in, sinks, a_re, a_im, log_dt, b_re, b_im, c_re, c_im, d_skip,
           w_glu, b_glu, g_attn_out, g_ssm_out, w_o, g_post_mix, g_pre_ffn, w_gate, w_up, w_down,
           g_post_ffn):
    depth = w_in.shape[0]
    T = BATCH * SEQ
    h = x.reshape(T, D_MODEL)
    pos2 = positions.reshape(T, 1)
    for i in range(depth):
        lam_re, lam_im, bblk, cblk = _ssm_prep(a_re[i], a_im[i], log_dt[i], b_re[i], b_im[i],
                                               c_re[i], c_im[i])
        q, k, v, u, w_o_bf, w_glu_bf = _in_proj(h, pos2, g_pre_mix[i][None, :], w_in[i].astype(BF16),
                                                w_o[i], w_glu[i], tm=512)
        attn_n = _attention(q, k, v, sinks[i], g_attn_out[i][None, :], tq=256)
        ssm_n, wg_bf, wu_bf, wd_bf = _ssm(u.reshape(SEQ, BATCH, D_SSM), bblk, cblk, lam_re, lam_im,
                                          d_skip[i].reshape(1, D_SSM), w_glu_bf, b_glu[i][None, :],
                                          g_ssm_out[i][None, :], w_gate[i], w_up[i], w_down[i], tl=64)
        h, hn = _out_proj(attn_n, ssm_n.reshape(SEQ, BATCH * D_SSM), h, w_o_bf,
                          g_post_mix[i][None, :], g_pre_ffn[i][None, :], tm=512)
        h = _ffn(hn, h, wg_bf, wu_bf, wd_bf, g_post_ffn[i][None, :], tm=1024, tf=512, te=512)
    return h.reshape(BATCH, SEQ, D_MODEL)
```

```python
import functools
import math

import jax
import jax.numpy as jnp
import numpy as np
from jax import lax
from jax.experimental import pallas as pl
from jax.experimental.pallas import tpu as pltpu

D_MODEL = 2048
BATCH = 8
SEQ = 2048
HEAD_DIM = 64
D_ATTN = 1024
N_Q_HEADS = 16
N_KV_HEADS = 4
Q_PER_KV = 4
D_KV = 256
WINDOW = 128
ROPE_THETA = 10000.0
D_SSM = 1024
SSM_GROUP = 16
N_SSM_GROUPS = 64
SSM_STATE = 64
D_IN = D_ATTN + 2 * D_KV + D_SSM
D_FF = 5632
RMS_EPS = 1e-6

LANES = 128
SUBLANES = 8
GROUPS_PER_TILE = 16
N_GROUP_TILES = N_SSM_GROUPS // GROUPS_PER_TILE
TILE_CH = GROUPS_PER_TILE * SSM_GROUP
TILE_ST = GROUPS_PER_TILE * SSM_STATE
N_CHUNKS = TILE_ST // LANES
D_KV_DUP = 2 * N_KV_HEADS * HEAD_DIM
ROW_SUB = 256
N_CAST_STEPS = 32

F32 = jnp.float32
BF16 = jnp.bfloat16
NT_DIMS = (((1,), (1,)), ((), ()))


def _re_cols(c):
    return slice(2 * c * LANES, (2 * c + 1) * LANES)


def _im_cols(c):
    return slice((2 * c + 1) * LANES, (2 * c + 2) * LANES)


def _rms(x, g):
    ms = jnp.mean(x * x, axis=-1, keepdims=True)
    return x * lax.rsqrt(ms + RMS_EPS) * g


def _split3(x):
    x1 = x.astype(BF16)
    r1 = x - x1.astype(F32)
    x2 = r1.astype(BF16)
    x3 = (r1 - x2.astype(F32)).astype(BF16)
    return x1, x2, x3


def _ssm_prep_kernel(ar_ref, ai_ref, ldt_ref, br_ref, bi_ref, cr_ref, ci_ref, ep_ref, en_ref,
                     lre_ref, lim_ref, bblk_ref, cblk_ref):
    ar = ar_ref[0]
    ai = ai_ref[0]
    dt = jnp.exp(ldt_ref[0])
    mag = jnp.exp(ar * dt)
    lam_re = mag * jnp.cos(ai * dt)
    lam_im = mag * jnp.sin(ai * dt)
    den = ar * ar + ai * ai
    nr = lam_re - 1.0
    ni = lam_im
    f_re = (nr * ar + ni * ai) / den
    f_im = (ni * ar - nr * ai) / den
    lre_ref[0] = jnp.broadcast_to(lam_re, (BATCH, TILE_ST))
    lim_ref[0] = jnp.broadcast_to(lam_im, (BATCH, TILE_ST))

    ep = ep_ref[...]

    def spread_b(b):
        return sum(lax.dot_general(ep, piece, NT_DIMS, preferred_element_type=F32)
                   for piece in _split3(b))

    pb_re = spread_b(br_ref[0])
    pb_im = spread_b(bi_ref[0])
    row = lax.broadcasted_iota(jnp.int32, (TILE_CH, TILE_ST), 0)
    col = lax.broadcasted_iota(jnp.int32, (TILE_CH, TILE_ST), 1)
    diag = (row // SSM_GROUP) == (col // SSM_STATE)
    bb_re = jnp.where(diag, f_re * pb_re - f_im * pb_im, 0.0).astype(BF16)
    bb_im = jnp.where(diag, f_re * pb_im + f_im * pb_re, 0.0).astype(BF16)
    for c in range(N_CHUNKS):
        bblk_ref[0, :, _re_cols(c)] = bb_re[:, c * LANES:(c + 1) * LANES]
        bblk_ref[0, :, _im_cols(c)] = bb_im[:, c * LANES:(c + 1) * LANES]

    en = en_ref[...]
    row = lax.broadcasted_iota(jnp.int32, (TILE_ST, TILE_CH), 0)
    col = lax.broadcasted_iota(jnp.int32, (TILE_ST, TILE_CH), 1)
    diag = (row // SSM_STATE) == (col // SSM_GROUP)
    pc_re = lax.dot_general(en, cr_ref[0].astype(BF16), NT_DIMS, preferred_element_type=F32)
    pc_im = lax.dot_general(en, ci_ref[0].astype(BF16), NT_DIMS, preferred_element_type=F32)
    cc_re = jnp.where(diag, pc_re, 0.0).astype(BF16)
    cc_im = jnp.where(diag, -pc_im, 0.0).astype(BF16)
    for c in range(N_CHUNKS):
        cblk_ref[0, _re_cols(c), :] = cc_re[c * LANES:(c + 1) * LANES, :]
        cblk_ref[0, _im_cols(c), :] = cc_im[c * LANES:(c + 1) * LANES, :]


def _ssm_prep(a_re, a_im, log_dt, b_re, b_im, c_re, c_im):
    nt = N_GROUP_TILES
    lanes3 = lambda a: a.reshape(nt, 1, TILE_ST)
    ldt = jnp.broadcast_to(log_dt[:, None], (N_SSM_GROUPS, SSM_STATE))
    ep = jnp.asarray(np.tile(np.eye(SSM_GROUP, dtype=np.float32), (GROUPS_PER_TILE, 1)), BF16)
    en = jnp.asarray(np.tile(np.eye(SSM_STATE, dtype=np.float32), (GROUPS_PER_TILE, 1)), BF16)
    t3 = lambda t: (t, 0, 0)
    c2 = lambda t: (0, 0)
    return pl.pallas_call(
        _ssm_prep_kernel,
        grid=(nt,),
        in_specs=[
            pl.BlockSpec((1, 1, TILE_ST), t3),
            pl.BlockSpec((1, 1, TILE_ST), t3),
            pl.BlockSpec((1, 1, TILE_ST), t3),
            pl.BlockSpec((1, TILE_ST, SSM_GROUP), t3),
            pl.BlockSpec((1, TILE_ST, SSM_GROUP), t3),
            pl.BlockSpec((1, TILE_CH, SSM_STATE), t3),
            pl.BlockSpec((1, TILE_CH, SSM_STATE), t3),
            pl.BlockSpec((TILE_CH, SSM_GROUP), c2),
            pl.BlockSpec((TILE_ST, SSM_STATE), c2),
        ],
        out_specs=[
            pl.BlockSpec((1, BATCH, TILE_ST), t3),
            pl.BlockSpec((1, BATCH, TILE_ST), t3),
            pl.BlockSpec((1, TILE_CH, 2 * TILE_ST), t3),
            pl.BlockSpec((1, 2 * TILE_ST, TILE_CH), t3),
        ],
        out_shape=[
            jax.ShapeDtypeStruct((nt, BATCH, TILE_ST), F32),
            jax.ShapeDtypeStruct((nt, BATCH, TILE_ST), F32),
            jax.ShapeDtypeStruct((nt, TILE_CH, 2 * TILE_ST), BF16),
            jax.ShapeDtypeStruct((nt, 2 * TILE_ST, TILE_CH), BF16),
        ],
        compiler_params=pltpu.CompilerParams(dimension_semantics=("parallel",)),
        name="ssm_prep",
    )(lanes3(a_re), lanes3(a_im), lanes3(ldt),
      b_re.reshape(nt, TILE_ST, SSM_GROUP), b_im.reshape(nt, TILE_ST, SSM_GROUP),
      c_re.reshape(nt, TILE_CH, SSM_STATE), c_im.reshape(nt, TILE_CH, SSM_STATE), ep, en)


def _inproj_kernel(x_ref, pos_ref, g_ref, invf_ref, sgn_ref, w_ref, wo_ref, wglu_ref,
                   q_ref, k_ref, v_ref, u_ref, wo_bf_ref, wglu_bf_ref):
    tm = x_ref.shape[0]
    wo_bf_ref[...] = wo_ref[...].astype(BF16)
    wglu_bf_ref[...] = wglu_ref[...].astype(BF16)

    lane = lax.broadcasted_iota(jnp.int32, (ROW_SUB, LANES), 1)
    first_half = (lane & (HEAD_DIM // 2)) == 0
    first_head = lane < HEAD_DIM
    scale = 1.0 / math.sqrt(HEAD_DIM)
    for r in range(tm // ROW_SUB):
        rs = slice(r * ROW_SUB, (r + 1) * ROW_SUB)
        xn = _rms(x_ref[rs, :], g_ref[...]).astype(BF16)
        ang = pos_ref[rs, :].astype(F32) * invf_ref[...]
        cos = jnp.cos(ang)
        sin = jnp.sin(ang) * sgn_ref[...]

        def rotary(t):
            partner = jnp.where(first_half,
                                pltpu.roll(t, LANES - HEAD_DIM // 2, 1),
                                pltpu.roll(t, HEAD_DIM // 2, 1))
            return t * cos + partner * sin

        def store_dup(ref, c, t):
            swapped = pltpu.roll(t, HEAD_DIM, 1)
            ref[rs, (2 * c) * LANES:(2 * c + 1) * LANES] = jnp.where(first_head, t, swapped).astype(BF16)
            ref[rs, (2 * c + 1) * LANES:(2 * c + 2) * LANES] = jnp.where(first_head, swapped, t).astype(BF16)

        for j in range(D_ATTN // 256):
            p = jnp.dot(xn, w_ref[:, j * 256:(j + 1) * 256], preferred_element_type=F32)
            for c in range(2):
                col = j * 256 + c * LANES
                q_ref[rs, col:col + LANES] = (rotary(p[:, c * LANES:(c + 1) * LANES]) * scale).astype(BF16)
        p = jnp.dot(xn, w_ref[:, D_ATTN:D_ATTN + D_KV], preferred_element_type=F32)
        for c in range(2):
            store_dup(k_ref, c, rotary(p[:, c * LANES:(c + 1) * LANES]))
        p = jnp.dot(xn, w_ref[:, D_ATTN + D_KV:D_ATTN + 2 * D_KV], preferred_element_type=F32)
        for c in range(2):
            store_dup(v_ref, c, p[:, c * LANES:(c + 1) * LANES])
        for j in range(D_SSM // 256):
            col = D_ATTN + 2 * D_KV + j * 256
            u_ref[rs, j * 256:(j + 1) * 256] = jnp.dot(xn, w_ref[:, col:col + 256],
                                                        preferred_element_type=F32)


def _in_proj(x2, pos2, g_pre_mix, w_in_bf, w_o, w_glu, tm):
    T = x2.shape[0]
    nt = SEQ // tm
    assert BATCH * nt == N_CAST_STEPS
    wo_slab = (D_MODEL // N_CAST_STEPS, D_MODEL)
    wglu_slab = (D_SSM // N_CAST_STEPS, D_SSM)
    half = HEAD_DIM // 2
    invf = ROPE_THETA ** (-np.arange(half, dtype=np.float32) / half)
    invf = np.tile(invf.astype(np.float32), LANES // half)[None, :]
    sgn = np.tile(np.concatenate([-np.ones(half, np.float32), np.ones(half, np.float32)]),
                  LANES // HEAD_DIM)[None, :]
    row = lambda b, i: (b * nt + i, 0)
    const = lambda b, i: (0, 0)
    return pl.pallas_call(
        _inproj_kernel,
        grid=(BATCH, nt),
        in_specs=[
            pl.BlockSpec((tm, D_MODEL), row),
            pl.BlockSpec((tm, 1), row),
            pl.BlockSpec((1, D_MODEL), const),
            pl.BlockSpec((1, LANES), const),
            pl.BlockSpec((1, LANES), const),
            pl.BlockSpec((D_MODEL, D_IN), const),
            pl.BlockSpec(wo_slab, row),
            pl.BlockSpec(wglu_slab, row),
        ],
        out_specs=[
            pl.BlockSpec((tm, D_ATTN), row),
            pl.BlockSpec((tm, D_KV_DUP), row),
            pl.BlockSpec((tm, D_KV_DUP), row),
            pl.BlockSpec((tm, D_SSM), lambda b, i: (i, b)),
            pl.BlockSpec(wo_slab, row),
            pl.BlockSpec(wglu_slab, row),
        ],
        out_shape=[
            jax.ShapeDtypeStruct((T, D_ATTN), BF16),
            jax.ShapeDtypeStruct((T, D_KV_DUP), BF16),
            jax.ShapeDtypeStruct((T, D_KV_DUP), BF16),
            jax.ShapeDtypeStruct((SEQ, BATCH * D_SSM), F32),
            jax.ShapeDtypeStruct((D_MODEL, D_MODEL), BF16),
            jax.ShapeDtypeStruct((D_SSM, D_SSM), BF16),
        ],
        compiler_params=pltpu.CompilerParams(
            dimension_semantics=("parallel", "parallel"), vmem_limit_bytes=52 * 2**20),
        name="in_proj",
    )(x2, pos2, g_pre_mix, jnp.asarray(invf), jnp.asarray(sgn), w_in_bf, w_o, w_glu)


def _attn_kernel(sinks_ref, q_ref, k_ref, v_ref, kh_ref, vh_ref, g_ref, o_ref, acc_ref):
    tq = q_ref.shape[0]
    first_tile = pl.program_id(1) == 0
    two = 2 * WINDOW
    row = lax.broadcasted_iota(jnp.int32, (two, WINDOW), 0) % WINDOW
    col = lax.broadcasted_iota(jnp.int32, (two, WINDOW), 1)
    use_prev = col > row
    top = lax.broadcasted_iota(jnp.int32, (two, 1), 0) < WINDOW
    low_lanes = lax.broadcasted_iota(jnp.int32, (two, LANES), 1) < HEAD_DIM
    zero = jnp.zeros((), BF16)
    for j in range(tq // WINDOW):
        lo, hi = j * WINDOW, (j + 1) * WINDOW
        if j == 0:
            kk = jnp.concatenate([kh_ref[...], k_ref[lo:hi, :]], axis=0)
            vv = jnp.concatenate([vh_ref[...], v_ref[lo:hi, :]], axis=0)
        else:
            kk = k_ref[lo - WINDOW:hi, :]
            vv = v_ref[lo - WINDOW:hi, :]
        for kv in range(N_KV_HEADS):
            kd = kk[:, kv * LANES:(kv + 1) * LANES]
            vd = vv[:, kv * LANES:(kv + 1) * LANES]
            c0 = kv * Q_PER_KV * HEAD_DIM
            q2 = jnp.concatenate([q_ref[lo:hi, c0:c0 + LANES], q_ref[lo:hi, c0 + LANES:c0 + 2 * LANES]],
                                 axis=0)
            out = None
            for side in range(2):
                keep = low_lanes if side == 0 else jnp.logical_not(low_lanes)
                ks = jnp.where(keep, kd, zero)
                vs = jnp.where(keep, vd, zero)
                s = lax.dot_general(q2, ks, NT_DIMS, preferred_element_type=F32)
                s_prev = s[:, :WINDOW]
                if j == 0:
                    s_prev = jnp.where(first_tile, -jnp.inf, s_prev)
                sf = jnp.where(use_prev, s_prev, s[:, WINDOW:])
                sink = jnp.where(top, sinks_ref[kv * Q_PER_KV + side], sinks_ref[kv * Q_PER_KV + 2 + side])
                m = jnp.maximum(jnp.max(sf, axis=-1, keepdims=True), sink)
                p = jnp.exp(sf - m)
                den = jnp.sum(p, axis=-1, keepdims=True) + jnp.exp(sink - m)
                pn = p / den
                pcat = jnp.concatenate([jnp.where(use_prev, pn, 0.0), jnp.where(use_prev, 0.0, pn)],
                                       axis=1).astype(BF16)
                o = jnp.dot(pcat, vs, preferred_element_type=F32)
                out = o if out is None else out + o
            acc_ref[:, c0:c0 + LANES] = out[:WINDOW]
            acc_ref[:, c0 + LANES:c0 + 2 * LANES] = out[WINDOW:]
        o_ref[lo:hi, :] = _rms(acc_ref[...], g_ref[...]).astype(BF16)


def _attention(q, k, v, sinks, g_attn_out, tq):
    T = q.shape[0]
    nt = SEQ // tq
    per = tq // WINDOW
    row = lambda b, i: (b * nt + i, 0)
    halo = lambda b, i: (jnp.maximum(b * (SEQ // WINDOW) + i * per - 1, 0), 0)
    return pl.pallas_call(
        _attn_kernel,
        grid=(BATCH, nt),
        in_specs=[
            pl.BlockSpec(memory_space=pltpu.SMEM),
            pl.BlockSpec((tq, D_ATTN), row),
            pl.BlockSpec((tq, D_KV_DUP), row),
            pl.BlockSpec((tq, D_KV_DUP), row),
            pl.BlockSpec((WINDOW, D_KV_DUP), halo),
            pl.BlockSpec((WINDOW, D_KV_DUP), halo),
            pl.BlockSpec((1, D_ATTN), lambda b, i: (0, 0)),
        ],
        out_specs=pl.BlockSpec((tq, D_ATTN), row),
        out_shape=jax.ShapeDtypeStruct((T, D_ATTN), BF16),
        scratch_shapes=[pltpu.VMEM((WINDOW, D_ATTN), F32)],
        compiler_params=pltpu.CompilerParams(dimension_semantics=("parallel", "parallel")),
        name="attention",
    )(sinks, q, k, v, k, v, g_attn_out)


def _ssm_kernel(u_ref, bblk_ref, cblk_ref, lre_ref, lim_ref, dskip_ref, wglu_ref, bglu_ref, g_ref,
                wg_ref, wu_ref, wd_ref, o_ref, wg_bf_ref, wu_bf_ref, wd_bf_ref, s_ref, y_ref, carry_ref):
    tl = u_ref.shape[0]
    rows = tl * BATCH
    wg_bf_ref[...] = wg_ref[...].astype(BF16)
    wu_bf_ref[...] = wu_ref[...].astype(BF16)
    wd_bf_ref[...] = wd_ref[...].astype(BF16)

    @pl.when(pl.program_id(0) == 0)
    def _():
        carry_ref[...] = jnp.zeros_like(carry_ref)

    u = u_ref[...].reshape(rows, D_SSM)
    ub = u.astype(BF16)
    for gt in range(N_GROUP_TILES):
        buf = gt % 2
        ch = slice(gt * TILE_CH, (gt + 1) * TILE_CH)
        s_ref[buf] = jnp.dot(ub[:, ch], bblk_ref[gt], preferred_element_type=F32)
        for c in range(N_CHUNKS):
            lanes = slice(c * LANES, (c + 1) * LANES)
            re_cols, im_cols = _re_cols(c), _im_cols(c)
            lr = lre_ref[gt, :, lanes]
            li = lim_ref[gt, :, lanes]
            sr = carry_ref[gt, :, re_cols]
            si = carry_ref[gt, :, im_cols]
            for t in range(tl):
                r = slice(t * BATCH, (t + 1) * BATCH)
                nr = lr * sr - li * si + s_ref[buf, r, re_cols]
                ni = lr * si + li * sr + s_ref[buf, r, im_cols]
                s_ref[buf, r, re_cols] = nr
                s_ref[buf, r, im_cols] = ni
                sr, si = nr, ni
            carry_ref[gt, :, re_cols] = sr
            carry_ref[gt, :, im_cols] = si
        y = jnp.dot(s_ref[buf].astype(BF16), cblk_ref[gt], preferred_element_type=F32)
        y_ref[:, ch] = y + dskip_ref[:, ch] * u[:, ch]
    z = jax.nn.gelu(y_ref[...])
    gate = jax.nn.sigmoid(jnp.dot(z.astype(BF16), wglu_ref[...], preferred_element_type=F32)
                          + bglu_ref[...])
    o_ref[...] = _rms(z * gate, g_ref[...]).reshape(tl, BATCH, D_SSM)


def _ssm(u3, bblk, cblk, lam_re, lam_im, d_skip, w_glu_bf, b_glu, g_ssm_out, w_gate, w_up, w_down, tl):
    rows = tl * BATCH
    n_tiles = SEQ // tl
    assert n_tiles == N_CAST_STEPS
    c2 = lambda i: (0, 0)
    c3 = lambda i: (0, 0, 0)
    cur = lambda i: i
    slab = lambda i: (cur(i), 0)
    up_slab = (D_MODEL // N_CAST_STEPS, D_FF)
    down_slab = (D_FF // N_CAST_STEPS, D_MODEL)
    return pl.pallas_call(
        _ssm_kernel,
        grid=(n_tiles,),
        in_specs=[
            pl.BlockSpec((tl, BATCH, D_SSM), lambda i: (cur(i), 0, 0)),
            pl.BlockSpec((N_GROUP_TILES, TILE_CH, 2 * TILE_ST), c3),
            pl.BlockSpec((N_GROUP_TILES, 2 * TILE_ST, TILE_CH), c3),
            pl.BlockSpec((N_GROUP_TILES, BATCH, TILE_ST), c3),
            pl.BlockSpec((N_GROUP_TILES, BATCH, TILE_ST), c3),
            pl.BlockSpec((1, D_SSM), c2),
            pl.BlockSpec((D_SSM, D_SSM), c2),
            pl.BlockSpec((1, D_SSM), c2),
            pl.BlockSpec((1, D_SSM), c2),
            pl.BlockSpec(up_slab, slab),
            pl.BlockSpec(up_slab, slab),
            pl.BlockSpec(down_slab, slab),
        ],
        out_specs=[
            pl.BlockSpec((tl, BATCH, D_SSM), lambda i: (i, 0, 0)),
            pl.BlockSpec(up_slab, slab),
            pl.BlockSpec(up_slab, slab),
            pl.BlockSpec(down_slab, slab),
        ],
        out_shape=[
            jax.ShapeDtypeStruct((SEQ, BATCH, D_SSM), F32),
            jax.ShapeDtypeStruct((D_MODEL, D_FF), BF16),
            jax.ShapeDtypeStruct((D_MODEL, D_FF), BF16),
            jax.ShapeDtypeStruct((D_FF, D_MODEL), BF16),
        ],
        scratch_shapes=[
            pltpu.VMEM((2, rows, 2 * TILE_ST), F32),
            pltpu.VMEM((rows, D_SSM), F32),
            pltpu.VMEM((N_GROUP_TILES, BATCH, 2 * TILE_ST), F32),
        ],
        compiler_params=pltpu.CompilerParams(
            dimension_semantics=("arbitrary",), vmem_limit_bytes=52 * 2**20),
        name="ssm",
    )(u3, bblk, cblk, lam_re, lam_im, d_skip, w_glu_bf, b_glu, g_ssm_out, w_gate, w_up, w_down)


def _outproj_kernel(attn_ref, ssm_ref, x_ref, wo_ref, gpost_ref, gpre_ref, h_ref, hn_ref):
    tm = x_ref.shape[0]
    for r in range(tm // ROW_SUB):
        rs = slice(r * ROW_SUB, (r + 1) * ROW_SUB)
        mix = jnp.dot(attn_ref[rs, :], wo_ref[:D_ATTN, :], preferred_element_type=F32)
        mix = mix + jnp.dot(ssm_ref[rs, :].astype(BF16), wo_ref[D_ATTN:, :],
                            preferred_element_type=F32)
        h = x_ref[rs, :] + _rms(mix, gpost_ref[...])
        h_ref[rs, :] = h
        hn_ref[rs, :] = _rms(h, gpre_ref[...]).astype(BF16)


def _out_proj(attn_n, ssm2, x2, w_o_bf, g_post_mix, g_pre_ffn, tm):
    T = x2.shape[0]
    nt = SEQ // tm
    row = lambda b, i: (b * nt + i, 0)
    const = lambda b, i: (0, 0)
    return pl.pallas_call(
        _outproj_kernel,
        grid=(BATCH, nt),
        in_specs=[
            pl.BlockSpec((tm, D_ATTN), row),
            pl.BlockSpec((tm, D_SSM), lambda b, i: (i, b)),
            pl.BlockSpec((tm, D_MODEL), row),
            pl.BlockSpec((D_MODEL, D_MODEL), const),
            pl.BlockSpec((1, D_MODEL), const),
            pl.BlockSpec((1, D_MODEL), const),
        ],
        out_specs=[pl.BlockSpec((tm, D_MODEL), row), pl.BlockSpec((tm, D_MODEL), row)],
        out_shape=[jax.ShapeDtypeStruct((T, D_MODEL), F32), jax.ShapeDtypeStruct((T, D_MODEL), BF16)],
        compiler_params=pltpu.CompilerParams(
            dimension_semantics=("parallel", "parallel"), vmem_limit_bytes=52 * 2**20),
        name="out_proj",
    )(attn_n, ssm2, x2, w_o_bf, g_post_mix, g_pre_ffn)


def _ffn_kernel(hn_ref, h_ref, wg_ref, wu_ref, wd_ref, g_ref, o_ref, acc_ref, *, nf):
    j = pl.program_id(1)
    te = o_ref.shape[0]

    def partial():
        hn = hn_ref[...]
        gate = jnp.dot(hn, wg_ref[...], preferred_element_type=F32)
        up = jnp.dot(hn, wu_ref[...], preferred_element_type=F32)
        hid = (jax.nn.silu(gate) * up).astype(BF16)
        return jnp.dot(hid, wd_ref[...], preferred_element_type=F32)

    @pl.when(j == 0)
    def _():
        acc_ref[...] = partial()

    @pl.when(jnp.logical_and(j > 0, j < nf))
    def _():
        acc_ref[...] += partial()

    @pl.when(j >= nf)
    def _():
        r = pl.multiple_of((j - nf) * te, te)
        o_ref[...] = h_ref[...] + _rms(acc_ref[pl.ds(r, te), :], g_ref[...])


def _ffn(hn, h, wg_bf, wu_bf, wd_bf, g_post_ffn, tm, tf, te):
    T = h.shape[0]
    nf = D_FF // tf
    sub = tm // te
    n_tiles = T // tm
    chunk = lambda j: jnp.where(j < nf, j, 0)
    hn_row = lambda i, j: (jnp.minimum(i + (j >= nf).astype(jnp.int32), n_tiles - 1), 0)
    out_row = lambda i, j: (i * sub + jnp.clip(j - nf, 0, sub - 1), 0)
    h_row = lambda i, j: (jnp.where(j == 0, jnp.maximum(i * sub - 1, 0),
                                    i * sub + jnp.clip(j - nf, 0, sub - 1)), 0)
    return pl.pallas_call(
        functools.partial(_ffn_kernel, nf=nf),
        grid=(n_tiles, nf + sub),
        in_specs=[
            pl.BlockSpec((tm, D_MODEL), hn_row),
            pl.BlockSpec((te, D_MODEL), h_row),
            pl.BlockSpec((D_MODEL, tf), lambda i, j: (0, chunk(j))),
            pl.BlockSpec((D_MODEL, tf), lambda i, j: (0, chunk(j))),
            pl.BlockSpec((tf, D_MODEL), lambda i, j: (chunk(j), 0)),
            pl.BlockSpec((1, D_MODEL), lambda i, j: (0, 0)),
        ],
        out_specs=pl.BlockSpec((te, D_MODEL), out_row),
        out_shape=jax.ShapeDtypeStruct((T, D_MODEL), F32),
        scratch_shapes=[pltpu.VMEM((tm, D_MODEL), F32)],
        compiler_params=pltpu.CompilerParams(
            dimension_semantics=("parallel", "arbitrary"), vmem_limit_bytes=58 * 2**20),
        name="ffn",
    )(hn, h, wg_bf, wu_bf, wd_bf, g_post_ffn)


def kernel(x, positions, g_pre_mix, w_in, sinks, a_re, a_im, log_dt, b_re, b_im, c_re, c_im, d_skip,
           w_glu, b_glu, g_attn_out, g_ssm_out, w_o, g_post_mix, g_pre_ffn, w_gate, w_up, w_down,
           g_post_ffn):
    depth = w_in.shape[0]
    T = BATCH * SEQ
    h = x.reshape(T, D_MODEL)
    pos2 = positions.reshape(T, 1)
    for i in range(depth):
        lam_re, lam_im, bblk, cblk = _ssm_prep(a_re[i], a_im[i], log_dt[i], b_re[i], b_im[i],
                                               c_re[i], c_im[i])
        q, k, v, u, w_o_bf, w_glu_bf = _in_proj(h, pos2, g_pre_mix[i][None, :], w_in[i].astype(BF16),
                                                w_o[i], w_glu[i], tm=512)
        attn_n = _attention(q, k, v, sinks[i], g_attn_out[i][None, :], tq=512)
        ssm_n, wg_bf, wu_bf, wd_bf = _ssm(u.reshape(SEQ, BATCH, D_SSM), bblk, cblk, lam_re, lam_im,
                                          d_skip[i].reshape(1, D_SSM), w_glu_bf, b_glu[i][None, :],
                                          g_ssm_out[i][None, :], w_gate[i], w_up[i], w_down[i], tl=64)
        h, hn = _out_proj(attn_n, ssm_n.reshape(SEQ, BATCH * D_SSM), h, w_o_bf,
                          g_post_mix[i][None, :], g_pre_ffn[i][None, :], tm=512)
        h = _ffn(hn, h, wg_bf, wu_bf, wd_bf, g_post_ffn[i][None, :], tm=1024, tf=512, te=512)
    return h.reshape(BATCH, SEQ, D_MODEL)
```

```python
import functools
import math

import jax
import jax.numpy as jnp
import numpy as np
from jax import lax
from jax.experimental import pallas as pl
from jax.experimental.pallas import tpu as pltpu

D_MODEL = 2048
BATCH = 8
SEQ = 2048
HEAD_DIM = 64
D_ATTN = 1024
N_Q_HEADS = 16
N_KV_HEADS = 4
Q_PER_KV = 4
D_KV = 256
WINDOW = 128
ROPE_THETA = 10000.0
D_SSM = 1024
SSM_GROUP = 16
N_SSM_GROUPS = 64
SSM_STATE = 64
D_IN = D_ATTN + 2 * D_KV + D_SSM
D_FF = 5632
RMS_EPS = 1e-6

LANES = 128
SUBLANES = 8
GROUPS_PER_TILE = 16
N_GROUP_TILES = N_SSM_GROUPS // GROUPS_PER_TILE
TILE_CH = GROUPS_PER_TILE * SSM_GROUP
TILE_ST = GROUPS_PER_TILE * SSM_STATE
N_CHUNKS = TILE_ST // LANES
D_KV_DUP = 2 * N_KV_HEADS * HEAD_DIM
ROW_SUB = 256
N_CAST_STEPS = 32
SCORE_AHEAD = 8

F32 = jnp.float32
BF16 = jnp.bfloat16
NT_DIMS = (((1,), (1,)), ((), ()))


def _re_cols(c):
    return slice(2 * c * LANES, (2 * c + 1) * LANES)


def _im_cols(c):
    return slice((2 * c + 1) * LANES, (2 * c + 2) * LANES)


def _rms(x, g):
    ms = jnp.mean(x * x, axis=-1, keepdims=True)
    return x * lax.rsqrt(ms + RMS_EPS) * g


def _split3(x):
    x1 = x.astype(BF16)
    r1 = x - x1.astype(F32)
    x2 = r1.astype(BF16)
    x3 = (r1 - x2.astype(F32)).astype(BF16)
    return x1, x2, x3


def _ssm_prep_kernel(ar_ref, ai_ref, ldt_ref, br_ref, bi_ref, cr_ref, ci_ref, ep_ref, en_ref,
                     lre_ref, lim_ref, bblk_ref, cblk_ref):
    ar = ar_ref[0]
    ai = ai_ref[0]
    dt = jnp.exp(ldt_ref[0])
    mag = jnp.exp(ar * dt)
    lam_re = mag * jnp.cos(ai * dt)
    lam_im = mag * jnp.sin(ai * dt)
    den = ar * ar + ai * ai
    nr = lam_re - 1.0
    ni = lam_im
    f_re = (nr * ar + ni * ai) / den
    f_im = (ni * ar - nr * ai) / den
    lre_ref[0] = jnp.broadcast_to(lam_re, (BATCH, TILE_ST))
    lim_ref[0] = jnp.broadcast_to(lam_im, (BATCH, TILE_ST))

    ep = ep_ref[...]

    def spread_b(b):
        return sum(lax.dot_general(ep, piece, NT_DIMS, preferred_element_type=F32)
                   for piece in _split3(b))

    pb_re = spread_b(br_ref[0])
    pb_im = spread_b(bi_ref[0])
    row = lax.broadcasted_iota(jnp.int32, (TILE_CH, TILE_ST), 0)
    col = lax.broadcasted_iota(jnp.int32, (TILE_CH, TILE_ST), 1)
    diag = (row // SSM_GROUP) == (col // SSM_STATE)
    bb_re = jnp.where(diag, f_re * pb_re - f_im * pb_im, 0.0).astype(BF16)
    bb_im = jnp.where(diag, f_re * pb_im + f_im * pb_re, 0.0).astype(BF16)
    for c in range(N_CHUNKS):
        bblk_ref[0, :, _re_cols(c)] = bb_re[:, c * LANES:(c + 1) * LANES]
        bblk_ref[0, :, _im_cols(c)] = bb_im[:, c * LANES:(c + 1) * LANES]

    en = en_ref[...]
    row = lax.broadcasted_iota(jnp.int32, (TILE_ST, TILE_CH), 0)
    col = lax.broadcasted_iota(jnp.int32, (TILE_ST, TILE_CH), 1)
    diag = (row // SSM_STATE) == (col // SSM_GROUP)
    pc_re = lax.dot_general(en, cr_ref[0].astype(BF16), NT_DIMS, preferred_element_type=F32)
    pc_im = lax.dot_general(en, ci_ref[0].astype(BF16), NT_DIMS, preferred_element_type=F32)
    cc_re = jnp.where(diag, pc_re, 0.0).astype(BF16)
    cc_im = jnp.where(diag, -pc_im, 0.0).astype(BF16)
    for c in range(N_CHUNKS):
        cblk_ref[0, _re_cols(c), :] = cc_re[c * LANES:(c + 1) * LANES, :]
        cblk_ref[0, _im_cols(c), :] = cc_im[c * LANES:(c + 1) * LANES, :]


def _ssm_prep(a_re, a_im, log_dt, b_re, b_im, c_re, c_im):
    nt = N_GROUP_TILES
    lanes3 = lambda a: a.reshape(nt, 1, TILE_ST)
    ldt = jnp.broadcast_to(log_dt[:, None], (N_SSM_GROUPS, SSM_STATE))
    ep = jnp.asarray(np.tile(np.eye(SSM_GROUP, dtype=np.float32), (GROUPS_PER_TILE, 1)), BF16)
    en = jnp.asarray(np.tile(np.eye(SSM_STATE, dtype=np.float32), (GROUPS_PER_TILE, 1)), BF16)
    t3 = lambda t: (t, 0, 0)
    c2 = lambda t: (0, 0)
    return pl.pallas_call(
        _ssm_prep_kernel,
        grid=(nt,),
        in_specs=[
            pl.BlockSpec((1, 1, TILE_ST), t3),
            pl.BlockSpec((1, 1, TILE_ST), t3),
            pl.BlockSpec((1, 1, TILE_ST), t3),
            pl.BlockSpec((1, TILE_ST, SSM_GROUP), t3),
            pl.BlockSpec((1, TILE_ST, SSM_GROUP), t3),
            pl.BlockSpec((1, TILE_CH, SSM_STATE), t3),
            pl.BlockSpec((1, TILE_CH, SSM_STATE), t3),
            pl.BlockSpec((TILE_CH, SSM_GROUP), c2),
            pl.BlockSpec((TILE_ST, SSM_STATE), c2),
        ],
        out_specs=[
            pl.BlockSpec((1, BATCH, TILE_ST), t3),
            pl.BlockSpec((1, BATCH, TILE_ST), t3),
            pl.BlockSpec((1, TILE_CH, 2 * TILE_ST), t3),
            pl.BlockSpec((1, 2 * TILE_ST, TILE_CH), t3),
        ],
        out_shape=[
            jax.ShapeDtypeStruct((nt, BATCH, TILE_ST), F32),
            jax.ShapeDtypeStruct((nt, BATCH, TILE_ST), F32),
            jax.ShapeDtypeStruct((nt, TILE_CH, 2 * TILE_ST), BF16),
            jax.ShapeDtypeStruct((nt, 2 * TILE_ST, TILE_CH), BF16),
        ],
        compiler_params=pltpu.CompilerParams(dimension_semantics=("parallel",)),
        name="ssm_prep",
    )(lanes3(a_re), lanes3(a_im), lanes3(ldt),
      b_re.reshape(nt, TILE_ST, SSM_GROUP), b_im.reshape(nt, TILE_ST, SSM_GROUP),
      c_re.reshape(nt, TILE_CH, SSM_STATE), c_im.reshape(nt, TILE_CH, SSM_STATE), ep, en)


def _inproj_kernel(x_ref, pos_ref, g_ref, invf_ref, sgn_ref, w_ref, wvt_ref, wo_ref, wglu_ref,
                   q_ref, k_ref, vt_ref, u_ref, wo_bf_ref, wglu_bf_ref):
    tm = x_ref.shape[0]
    wo_bf_ref[...] = wo_ref[...].astype(BF16)
    wglu_bf_ref[...] = wglu_ref[...].astype(BF16)

    lane = lax.broadcasted_iota(jnp.int32, (ROW_SUB, LANES), 1)
    first_half = (lane & (HEAD_DIM // 2)) == 0
    first_head = lane < HEAD_DIM
    scale = 1.0 / math.sqrt(HEAD_DIM)
    for r in range(tm // ROW_SUB):
        rs = slice(r * ROW_SUB, (r + 1) * ROW_SUB)
        xn = _rms(x_ref[rs, :], g_ref[...]).astype(BF16)
        ang = pos_ref[rs, :].astype(F32) * invf_ref[...]
        cos = jnp.cos(ang)
        sin = jnp.sin(ang) * sgn_ref[...]

        def rotary(t):
            partner = jnp.where(first_half,
                                pltpu.roll(t, LANES - HEAD_DIM // 2, 1),
                                pltpu.roll(t, HEAD_DIM // 2, 1))
            return t * cos + partner * sin

        def store_dup(ref, c, t):
            swapped = pltpu.roll(t, HEAD_DIM, 1)
            ref[rs, (2 * c) * LANES:(2 * c + 1) * LANES] = jnp.where(first_head, t, swapped).astype(BF16)
            ref[rs, (2 * c + 1) * LANES:(2 * c + 2) * LANES] = jnp.where(first_head, swapped, t).astype(BF16)

        for j in range(D_ATTN // 256):
            p = jnp.dot(xn, w_ref[:, j * 256:(j + 1) * 256], preferred_element_type=F32)
            for c in range(2):
                col = j * 256 + c * LANES
                q_ref[rs, col:col + LANES] = (rotary(p[:, c * LANES:(c + 1) * LANES]) * scale).astype(BF16)
        p = jnp.dot(xn, w_ref[:, D_ATTN:D_ATTN + D_KV], preferred_element_type=F32)
        for c in range(2):
            store_dup(k_ref, c, rotary(p[:, c * LANES:(c + 1) * LANES]))
        vt_ref[:, rs] = lax.dot_general(wvt_ref[...], xn, NT_DIMS,
                                        preferred_element_type=F32).astype(BF16)
        for j in range(D_SSM // 256):
            col = D_ATTN + 2 * D_KV + j * 256
            u_ref[rs, j * 256:(j + 1) * 256] = jnp.dot(xn, w_ref[:, col:col + 256],
                                                        preferred_element_type=F32)


def _in_proj(x2, pos2, g_pre_mix, w_in_bf, wvt_bf, w_o, w_glu, tm):
    T = x2.shape[0]
    nt = SEQ // tm
    assert BATCH * nt == N_CAST_STEPS
    wo_slab = (D_MODEL // N_CAST_STEPS, D_MODEL)
    wglu_slab = (D_SSM // N_CAST_STEPS, D_SSM)
    half = HEAD_DIM // 2
    invf = ROPE_THETA ** (-np.arange(half, dtype=np.float32) / half)
    invf = np.tile(invf.astype(np.float32), LANES // half)[None, :]
    sgn = np.tile(np.concatenate([-np.ones(half, np.float32), np.ones(half, np.float32)]),
                  LANES // HEAD_DIM)[None, :]
    row = lambda b, i: (b * nt + i, 0)
    const = lambda b, i: (0, 0)
    return pl.pallas_call(
        _inproj_kernel,
        grid=(BATCH, nt),
        in_specs=[
            pl.BlockSpec((tm, D_MODEL), row),
            pl.BlockSpec((tm, 1), row),
            pl.BlockSpec((1, D_MODEL), const),
            pl.BlockSpec((1, LANES), const),
            pl.BlockSpec((1, LANES), const),
            pl.BlockSpec((D_MODEL, D_IN), const),
            pl.BlockSpec((D_KV, D_MODEL), const),
            pl.BlockSpec(wo_slab, row),
            pl.BlockSpec(wglu_slab, row),
        ],
        out_specs=[
            pl.BlockSpec((tm, D_ATTN), row),
            pl.BlockSpec((tm, D_KV_DUP), row),
            pl.BlockSpec((D_KV, tm), lambda b, i: (0, b * nt + i)),
            pl.BlockSpec((tm, D_SSM), lambda b, i: (i, b)),
            pl.BlockSpec(wo_slab, row),
            pl.BlockSpec(wglu_slab, row),
        ],
        out_shape=[
            jax.ShapeDtypeStruct((T, D_ATTN), BF16),
            jax.ShapeDtypeStruct((T, D_KV_DUP), BF16),
            jax.ShapeDtypeStruct((D_KV, T), BF16),
            jax.ShapeDtypeStruct((SEQ, BATCH * D_SSM), F32),
            jax.ShapeDtypeStruct((D_MODEL, D_MODEL), BF16),
            jax.ShapeDtypeStruct((D_SSM, D_SSM), BF16),
        ],
        compiler_params=pltpu.CompilerParams(
            dimension_semantics=("parallel", "parallel"), vmem_limit_bytes=52 * 2**20),
        name="in_proj",
    )(x2, pos2, g_pre_mix, jnp.asarray(invf), jnp.asarray(sgn), w_in_bf, wvt_bf, w_o, w_glu)


def _attn_kernel(sinks_ref, q_ref, k_ref, vt_ref, kh_ref, vth_ref, g_ref, o_ref, acc_ref):
    tq = q_ref.shape[0]
    first_tile = pl.program_id(1) == 0
    two = 2 * WINDOW
    key = lax.broadcasted_iota(jnp.int32, (WINDOW, two), 0)
    qry = lax.broadcasted_iota(jnp.int32, (WINDOW, two), 1) % WINDOW
    use_prev = key > qry
    pair0 = lax.broadcasted_iota(jnp.int32, (1, two), 1) < WINDOW
    low_lanes = lax.broadcasted_iota(jnp.int32, (two, LANES), 1) < HEAD_DIM
    zero = jnp.zeros((), BF16)
    no_values = jnp.zeros((HEAD_DIM, two), BF16)

    def keys_of(j):
        lo, hi = j * WINDOW, (j + 1) * WINDOW
        if j == 0:
            return (jnp.concatenate([kh_ref[...], k_ref[lo:hi, :]], axis=0),
                    jnp.concatenate([vth_ref[...], vt_ref[:, lo:hi]], axis=1))
        return k_ref[lo - WINDOW:hi, :], vt_ref[:, lo - WINDOW:hi]

    def scores(j, kv, side):
        lo, hi = j * WINDOW, (j + 1) * WINDOW
        c0 = kv * Q_PER_KV * HEAD_DIM
        kd = keys_of(j)[0][:, kv * LANES:(kv + 1) * LANES]
        ks = jnp.where(low_lanes if side == 0 else jnp.logical_not(low_lanes), kd, zero)
        q2 = jnp.concatenate([q_ref[lo:hi, c0:c0 + LANES], q_ref[lo:hi, c0 + LANES:c0 + 2 * LANES]],
                             axis=0)
        return lax.dot_general(ks, q2, NT_DIMS, preferred_element_type=F32)

    def probs(j, kv, side, st):
        s_prev = st[:WINDOW, :]
        if j == 0:
            s_prev = jnp.where(first_tile, -jnp.inf, s_prev)
        sf = jnp.where(use_prev, s_prev, st[WINDOW:, :])
        sink = jnp.where(pair0, sinks_ref[kv * Q_PER_KV + side], sinks_ref[kv * Q_PER_KV + 2 + side])
        m = jnp.maximum(jnp.max(sf, axis=0, keepdims=True), sink)
        p = jnp.exp(sf - m)
        den = jnp.sum(p, axis=0, keepdims=True) + jnp.exp(sink - m)
        pn = p * (1.0 / den)
        return jnp.concatenate([jnp.where(use_prev, pn, 0.0), jnp.where(use_prev, 0.0, pn)],
                               axis=0).astype(BF16)

    def weighted_values(j, kv, side, pcat):
        vdt = keys_of(j)[1][kv * HEAD_DIM:(kv + 1) * HEAD_DIM, :]
        vst = jnp.concatenate([vdt, no_values] if side == 0 else [no_values, vdt], axis=0)
        return jnp.dot(vst, pcat, preferred_element_type=F32)

    tasks = [(j, kv, side) for j in range(tq // WINDOW) for kv in range(N_KV_HEADS) for side in range(2)]
    pending = {}
    out = None
    for n in range(len(tasks) + SCORE_AHEAD):
        if n < len(tasks):
            pending[n] = scores(*tasks[n])
        if n < SCORE_AHEAD:
            continue
        j, kv, side = tasks[n - SCORE_AHEAD]
        o = weighted_values(j, kv, side, probs(j, kv, side, pending.pop(n - SCORE_AHEAD)))
        if side == 0:
            out = o
            continue
        out = out + o
        c0 = kv * Q_PER_KV * HEAD_DIM
        acc_ref[c0:c0 + LANES, :] = out[:, :WINDOW]
        acc_ref[c0 + LANES:c0 + 2 * LANES, :] = out[:, WINDOW:]
        if kv == N_KV_HEADS - 1:
            a = acc_ref[...]
            ms = jnp.mean(a * a, axis=0, keepdims=True)
            o_ref[j * WINDOW:(j + 1) * WINDOW, :] = (
                (a * lax.rsqrt(ms + RMS_EPS)).T * g_ref[...]).astype(BF16)


def _attention(q, k, vt, sinks, g_attn_out, tq):
    T = q.shape[0]
    nt = SEQ // tq
    per = tq // WINDOW
    row = lambda b, i: (b * nt + i, 0)
    col = lambda b, i: (0, b * nt + i)
    halo = lambda b, i: jnp.maximum(b * (SEQ // WINDOW) + i * per - 1, 0)
    return pl.pallas_call(
        _attn_kernel,
        grid=(BATCH, nt),
        in_specs=[
            pl.BlockSpec(memory_space=pltpu.SMEM),
            pl.BlockSpec((tq, D_ATTN), row),
            pl.BlockSpec((tq, D_KV_DUP), row),
            pl.BlockSpec((D_KV, tq), col),
            pl.BlockSpec((WINDOW, D_KV_DUP), lambda b, i: (halo(b, i), 0)),
            pl.BlockSpec((D_KV, WINDOW), lambda b, i: (0, halo(b, i))),
            pl.BlockSpec((1, D_ATTN), lambda b, i: (0, 0)),
        ],
        out_specs=pl.BlockSpec((tq, D_ATTN), row),
        out_shape=jax.ShapeDtypeStruct((T, D_ATTN), BF16),
        scratch_shapes=[pltpu.VMEM((D_ATTN, WINDOW), F32)],
        compiler_params=pltpu.CompilerParams(dimension_semantics=("parallel", "parallel")),
        name="attention",
    )(sinks, q, k, vt, k, vt, g_attn_out)


def _ssm_kernel(u_ref, bblk_ref, cblk_ref, lre_ref, lim_ref, dskip_ref, wglu_ref, bglu_ref, g_ref,
                wg_ref, wu_ref, wd_ref, o_ref, wg_bf_ref, wu_bf_ref, wd_bf_ref, s_ref, y_ref, carry_ref):
    tl = u_ref.shape[0]
    rows = tl * BATCH
    wg_bf_ref[...] = wg_ref[...].astype(BF16)
    wu_bf_ref[...] = wu_ref[...].astype(BF16)
    wd_bf_ref[...] = wd_ref[...].astype(BF16)

    @pl.when(pl.program_id(0) == 0)
    def _():
        carry_ref[...] = jnp.zeros_like(carry_ref)

    u = u_ref[...].reshape(rows, D_SSM)
    ub = u.astype(BF16)

    def project_in(gt):
        ch = slice(gt * TILE_CH, (gt + 1) * TILE_CH)
        s_ref[gt % 2] = jnp.dot(ub[:, ch], bblk_ref[gt], preferred_element_type=F32)

    project_in(0)
    for gt in range(N_GROUP_TILES):
        buf = gt % 2
        ch = slice(gt * TILE_CH, (gt + 1) * TILE_CH)
        if gt + 1 < N_GROUP_TILES:
            project_in(gt + 1)
        for c in range(N_CHUNKS):
            lanes = slice(c * LANES, (c + 1) * LANES)
            re_cols, im_cols = _re_cols(c), _im_cols(c)
            lr = lre_ref[gt, :, lanes]
            li = lim_ref[gt, :, lanes]
            sr = carry_ref[gt, :, re_cols]
            si = carry_ref[gt, :, im_cols]
            for t in range(tl):
                r = slice(t * BATCH, (t + 1) * BATCH)
                nr = lr * sr - li * si + s_ref[buf, r, re_cols]
                ni = lr * si + li * sr + s_ref[buf, r, im_cols]
                s_ref[buf, r, re_cols] = nr
                s_ref[buf, r, im_cols] = ni
                sr, si = nr, ni
            carry_ref[gt, :, re_cols] = sr
            carry_ref[gt, :, im_cols] = si
        y = jnp.dot(s_ref[buf].astype(BF16), cblk_ref[gt], preferred_element_type=F32)
        y_ref[:, ch] = y + dskip_ref[:, ch] * u[:, ch]
    z = jax.nn.gelu(y_ref[...])
    gate = jax.nn.sigmoid(jnp.dot(z.astype(BF16), wglu_ref[...], preferred_element_type=F32)
                          + bglu_ref[...])
    o_ref[...] = _rms(z * gate, g_ref[...]).reshape(tl, BATCH, D_SSM)


def _ssm(u3, bblk, cblk, lam_re, lam_im, d_skip, w_glu_bf, b_glu, g_ssm_out, w_gate, w_up, w_down, tl):
    rows = tl * BATCH
    n_tiles = SEQ // tl
    assert n_tiles == N_CAST_STEPS
    c2 = lambda i: (0, 0)
    c3 = lambda i: (0, 0, 0)
    cur = lambda i: i
    slab = lambda i: (cur(i), 0)
    up_slab = (D_MODEL // N_CAST_STEPS, D_FF)
    down_slab = (D_FF // N_CAST_STEPS, D_MODEL)
    return pl.pallas_call(
        _ssm_kernel,
        grid=(n_tiles,),
        in_specs=[
            pl.BlockSpec((tl, BATCH, D_SSM), lambda i: (cur(i), 0, 0)),
            pl.BlockSpec((N_GROUP_TILES, TILE_CH, 2 * TILE_ST), c3),
            pl.BlockSpec((N_GROUP_TILES, 2 * TILE_ST, TILE_CH), c3),
            pl.BlockSpec((N_GROUP_TILES, BATCH, TILE_ST), c3),
            pl.BlockSpec((N_GROUP_TILES, BATCH, TILE_ST), c3),
            pl.BlockSpec((1, D_SSM), c2),
            pl.BlockSpec((D_SSM, D_SSM), c2),
            pl.BlockSpec((1, D_SSM), c2),
            pl.BlockSpec((1, D_SSM), c2),
            pl.BlockSpec(up_slab, slab),
            pl.BlockSpec(up_slab, slab),
            pl.BlockSpec(down_slab, slab),
        ],
        out_specs=[
            pl.BlockSpec((tl, BATCH, D_SSM), lambda i: (i, 0, 0)),
            pl.BlockSpec(up_slab, slab),
            pl.BlockSpec(up_slab, slab),
            pl.BlockSpec(down_slab, slab),
        ],
        out_shape=[
            jax.ShapeDtypeStruct((SEQ, BATCH, D_SSM), F32),
            jax.ShapeDtypeStruct((D_MODEL, D_FF), BF16),
            jax.ShapeDtypeStruct((D_MODEL, D_FF), BF16),
            jax.ShapeDtypeStruct((D_FF, D_MODEL), BF16),
        ],
        scratch_shapes=[
            pltpu.VMEM((2, rows, 2 * TILE_ST), F32),
            pltpu.VMEM((rows, D_SSM), F32),
            pltpu.VMEM((N_GROUP_TILES, BATCH, 2 * TILE_ST), F32),
        ],
        compiler_params=pltpu.CompilerParams(
            dimension_semantics=("arbitrary",), vmem_limit_bytes=52 * 2**20),
        name="ssm",
    )(u3, bblk, cblk, lam_re, lam_im, d_skip, w_glu_bf, b_glu, g_ssm_out, w_gate, w_up, w_down)


def _outproj_kernel(attn_ref, ssm_ref, x_ref, wo_ref, gpost_ref, gpre_ref, h_ref, hn_ref):
    tm = x_ref.shape[0]
    for r in range(tm // ROW_SUB):
        rs = slice(r * ROW_SUB, (r + 1) * ROW_SUB)
        mix = jnp.dot(attn_ref[rs, :], wo_ref[:D_ATTN, :], preferred_element_type=F32)
        mix = mix + jnp.dot(ssm_ref[rs, :].astype(BF16), wo_ref[D_ATTN:, :],
                            preferred_element_type=F32)
        h = x_ref[rs, :] + _rms(mix, gpost_ref[...])
        h_ref[rs, :] = h
        hn_ref[rs, :] = _rms(h, gpre_ref[...]).astype(BF16)


def _out_proj(attn_n, ssm2, x2, w_o_bf, g_post_mix, g_pre_ffn, tm):
    T = x2.shape[0]
    nt = SEQ // tm
    row = lambda b, i: (b * nt + i, 0)
    const = lambda b, i: (0, 0)
    return pl.pallas_call(
        _outproj_kernel,
        grid=(BATCH, nt),
        in_specs=[
            pl.BlockSpec((tm, D_ATTN), row),
            pl.BlockSpec((tm, D_SSM), lambda b, i: (i, b)),
            pl.BlockSpec((tm, D_MODEL), row),
            pl.BlockSpec((D_MODEL, D_MODEL), const),
            pl.BlockSpec((1, D_MODEL), const),
            pl.BlockSpec((1, D_MODEL), const),
        ],
        out_specs=[pl.BlockSpec((tm, D_MODEL), row), pl.BlockSpec((tm, D_MODEL), row)],
        out_shape=[jax.ShapeDtypeStruct((T, D_MODEL), F32), jax.ShapeDtypeStruct((T, D_MODEL), BF16)],
        compiler_params=pltpu.CompilerParams(
            dimension_semantics=("parallel", "parallel"), vmem_limit_bytes=52 * 2**20),
        name="out_proj",
    )(attn_n, ssm2, x2, w_o_bf, g_post_mix, g_pre_ffn)


def _ffn_kernel(hn_ref, h_ref, wg_ref, wu_ref, wd_ref, g_ref, o_ref, acc_ref, *, nf):
    j = pl.program_id(1)
    te = o_ref.shape[0]

    def reduce_step(first):
        n_sub = hn_ref.shape[0] // ROW_SUB
        rows = lambda r: slice(r * ROW_SUB, (r + 1) * ROW_SUB)

        def gate_up(r):
            hn = hn_ref[rows(r), :]
            return (jnp.dot(hn, wg_ref[...], preferred_element_type=F32),
                    jnp.dot(hn, wu_ref[...], preferred_element_type=F32))

        ahead = gate_up(0)
        for r in range(n_sub):
            gate, up = ahead
            if r + 1 < n_sub:
                ahead = gate_up(r + 1)
            hid = (jax.nn.silu(gate) * up).astype(BF16)
            part = jnp.dot(hid, wd_ref[...], preferred_element_type=F32)
            if first:
                acc_ref[rows(r), :] = part
            else:
                acc_ref[rows(r), :] += part

    @pl.when(j == 0)
    def _():
        reduce_step(True)

    @pl.when(jnp.logical_and(j > 0, j < nf))
    def _():
        reduce_step(False)

    @pl.when(j >= nf)
    def _():
        r = pl.multiple_of((j - nf) * te, te)
        o_ref[...] = h_ref[...] + _rms(acc_ref[pl.ds(r, te), :], g_ref[...])


def _ffn(hn, h, wg_bf, wu_bf, wd_bf, g_post_ffn, tm, tf, te):
    T = h.shape[0]
    nf = D_FF // tf
    sub = tm // te
    n_tiles = T // tm
    chunk = lambda j: jnp.where(j < nf, j, 0)
    hn_row = lambda i, j: (jnp.minimum(i + (j >= nf).astype(jnp.int32), n_tiles - 1), 0)
    out_row = lambda i, j: (i * sub + jnp.clip(j - nf, 0, sub - 1), 0)
    h_row = lambda i, j: (jnp.where(j == 0, jnp.maximum(i * sub - 1, 0),
                                    i * sub + jnp.clip(j - nf, 0, sub - 1)), 0)
    return pl.pallas_call(
        functools.partial(_ffn_kernel, nf=nf),
        grid=(n_tiles, nf + sub),
        in_specs=[
            pl.BlockSpec((tm, D_MODEL), hn_row),
            pl.BlockSpec((te, D_MODEL), h_row),
            pl.BlockSpec((D_MODEL, tf), lambda i, j: (0, chunk(j))),
            pl.BlockSpec((D_MODEL, tf), lambda i, j: (0, chunk(j))),
            pl.BlockSpec((tf, D_MODEL), lambda i, j: (chunk(j), 0)),
            pl.BlockSpec((1, D_MODEL), lambda i, j: (0, 0)),
        ],
        out_specs=pl.BlockSpec((te, D_MODEL), out_row),
        out_shape=jax.ShapeDtypeStruct((T, D_MODEL), F32),
        scratch_shapes=[pltpu.VMEM((tm, D_MODEL), F32)],
        compiler_params=pltpu.CompilerParams(
            dimension_semantics=("parallel", "arbitrary"), vmem_limit_bytes=58 * 2**20),
        name="ffn",
    )(hn, h, wg_bf, wu_bf, wd_bf, g_post_ffn)


def kernel(x, positions, g_pre_mix, w_in, sinks, a_re, a_im, log_dt, b_re, b_im, c_re, c_im, d_skip,
           w_glu, b_glu, g_attn_out, g_ssm_out, w_o, g_post_mix, g_pre_ffn, w_gate, w_up, w_down,
           g_post_ffn):
    depth = w_in.shape[0]
    T = BATCH * SEQ
    h = x.reshape(T, D_MODEL)
    pos2 = positions.reshape(T, 1)
    for i in range(depth):
        lam_re, lam_im, bblk, cblk = _ssm_prep(a_re[i], a_im[i], log_dt[i], b_re[i], b_im[i],
                                               c_re[i], c_im[i])
        w_in_bf = w_in[i].astype(BF16)
        wvt_bf = w_in_bf[:, D_ATTN + D_KV:D_ATTN + 2 * D_KV].T
        q, k, vt, u, w_o_bf, w_glu_bf = _in_proj(h, pos2, g_pre_mix[i][None, :], w_in_bf, wvt_bf,
                                                 w_o[i], w_glu[i], tm=512)
        attn_n = _attention(q, k, vt, sinks[i], g_attn_out[i][None, :], tq=512)
        ssm_n, wg_bf, wu_bf, wd_bf = _ssm(u.reshape(SEQ, BATCH, D_SSM), bblk, cblk, lam_re, lam_im,
                                          d_skip[i].reshape(1, D_SSM), w_glu_bf, b_glu[i][None, :],
                                          g_ssm_out[i][None, :], w_gate[i], w_up[i], w_down[i], tl=64)
        h, hn = _out_proj(attn_n, ssm_n.reshape(SEQ, BATCH * D_SSM), h, w_o_bf,
                          g_post_mix[i][None, :], g_pre_ffn[i][None, :], tm=512)
        h = _ffn(hn, h, wg_bf, wu_bf, wd_bf, g_post_ffn[i][None, :], tm=1024, tf=512, te=512)
    return h.reshape(BATCH, SEQ, D_MODEL)
```

```python
import functools
import math

import jax
import jax.numpy as jnp
import numpy as np
from jax import lax
from jax.experimental import pallas as pl
from jax.experimental.pallas import tpu as pltpu

D_MODEL = 2048
BATCH = 8
SEQ = 2048
HEAD_DIM = 64
D_ATTN = 1024
N_Q_HEADS = 16
N_KV_HEADS = 4
Q_PER_KV = 4
D_KV = 256
WINDOW = 128
ROPE_THETA = 10000.0
D_SSM = 1024
SSM_GROUP = 16
N_SSM_GROUPS = 64
SSM_STATE = 64
D_IN = D_ATTN + 2 * D_KV + D_SSM
D_FF = 5632
RMS_EPS = 1e-6

LANES = 128
SUBLANES = 8
GROUPS_PER_TILE = 16
N_GROUP_TILES = N_SSM_GROUPS // GROUPS_PER_TILE
TILE_CH = GROUPS_PER_TILE * SSM_GROUP
TILE_ST = GROUPS_PER_TILE * SSM_STATE
N_CHUNKS = TILE_ST // LANES
D_KV_DUP = 2 * N_KV_HEADS * HEAD_DIM
ROW_SUB = 256
MXU_COLS = 256
N_CAST_STEPS = 32
SCORE_AHEAD = 8

F32 = jnp.float32
BF16 = jnp.bfloat16
NT_DIMS = (((1,), (1,)), ((), ()))


def _re_cols(c):
    return slice(2 * c * LANES, (2 * c + 1) * LANES)


def _im_cols(c):
    return slice((2 * c + 1) * LANES, (2 * c + 2) * LANES)


def _rms(x, g):
    ms = jnp.mean(x * x, axis=-1, keepdims=True)
    return x * lax.rsqrt(ms + RMS_EPS) * g


def _split3(x):
    x1 = x.astype(BF16)
    r1 = x - x1.astype(F32)
    x2 = r1.astype(BF16)
    x3 = (r1 - x2.astype(F32)).astype(BF16)
    return x1, x2, x3


def _ssm_prep_kernel(ar_ref, ai_ref, ldt_ref, br_ref, bi_ref, cr_ref, ci_ref, ep_ref, en_ref,
                     lre_ref, lim_ref, bblk_ref, cblk_ref):
    ar = ar_ref[0]
    ai = ai_ref[0]
    dt = jnp.exp(ldt_ref[0])
    mag = jnp.exp(ar * dt)
    lam_re = mag * jnp.cos(ai * dt)
    lam_im = mag * jnp.sin(ai * dt)
    den = ar * ar + ai * ai
    nr = lam_re - 1.0
    ni = lam_im
    f_re = (nr * ar + ni * ai) / den
    f_im = (ni * ar - nr * ai) / den
    lre_ref[0] = jnp.broadcast_to(lam_re, (BATCH, TILE_ST))
    lim_ref[0] = jnp.broadcast_to(lam_im, (BATCH, TILE_ST))

    ep = ep_ref[...]

    def spread_b(b):
        return sum(lax.dot_general(ep, piece, NT_DIMS, preferred_element_type=F32)
                   for piece in _split3(b))

    pb_re = spread_b(br_ref[0])
    pb_im = spread_b(bi_ref[0])
    row = lax.broadcasted_iota(jnp.int32, (TILE_CH, TILE_ST), 0)
    col = lax.broadcasted_iota(jnp.int32, (TILE_CH, TILE_ST), 1)
    diag = (row // SSM_GROUP) == (col // SSM_STATE)
    bb_re = jnp.where(diag, f_re * pb_re - f_im * pb_im, 0.0).astype(BF16)
    bb_im = jnp.where(diag, f_re * pb_im + f_im * pb_re, 0.0).astype(BF16)
    for c in range(N_CHUNKS):
        bblk_ref[0, :, _re_cols(c)] = bb_re[:, c * LANES:(c + 1) * LANES]
        bblk_ref[0, :, _im_cols(c)] = bb_im[:, c * LANES:(c + 1) * LANES]

    en = en_ref[...]
    row = lax.broadcasted_iota(jnp.int32, (TILE_ST, TILE_CH), 0)
    col = lax.broadcasted_iota(jnp.int32, (TILE_ST, TILE_CH), 1)
    diag = (row // SSM_STATE) == (col // SSM_GROUP)
    pc_re = lax.dot_general(en, cr_ref[0].astype(BF16), NT_DIMS, preferred_element_type=F32)
    pc_im = lax.dot_general(en, ci_ref[0].astype(BF16), NT_DIMS, preferred_element_type=F32)
    cc_re = jnp.where(diag, pc_re, 0.0).astype(BF16)
    cc_im = jnp.where(diag, -pc_im, 0.0).astype(BF16)
    for c in range(N_CHUNKS):
        cblk_ref[0, _re_cols(c), :] = cc_re[c * LANES:(c + 1) * LANES, :]
        cblk_ref[0, _im_cols(c), :] = cc_im[c * LANES:(c + 1) * LANES, :]


def _ssm_prep(a_re, a_im, log_dt, b_re, b_im, c_re, c_im):
    nt = N_GROUP_TILES
    lanes3 = lambda a: a.reshape(nt, 1, TILE_ST)
    ldt = jnp.broadcast_to(log_dt[:, None], (N_SSM_GROUPS, SSM_STATE))
    ep = jnp.asarray(np.tile(np.eye(SSM_GROUP, dtype=np.float32), (GROUPS_PER_TILE, 1)), BF16)
    en = jnp.asarray(np.tile(np.eye(SSM_STATE, dtype=np.float32), (GROUPS_PER_TILE, 1)), BF16)
    t3 = lambda t: (t, 0, 0)
    c2 = lambda t: (0, 0)
    return pl.pallas_call(
        _ssm_prep_kernel,
        grid=(nt,),
        in_specs=[
            pl.BlockSpec((1, 1, TILE_ST), t3),
            pl.BlockSpec((1, 1, TILE_ST), t3),
            pl.BlockSpec((1, 1, TILE_ST), t3),
            pl.BlockSpec((1, TILE_ST, SSM_GROUP), t3),
            pl.BlockSpec((1, TILE_ST, SSM_GROUP), t3),
            pl.BlockSpec((1, TILE_CH, SSM_STATE), t3),
            pl.BlockSpec((1, TILE_CH, SSM_STATE), t3),
            pl.BlockSpec((TILE_CH, SSM_GROUP), c2),
            pl.BlockSpec((TILE_ST, SSM_STATE), c2),
        ],
        out_specs=[
            pl.BlockSpec((1, BATCH, TILE_ST), t3),
            pl.BlockSpec((1, BATCH, TILE_ST), t3),
            pl.BlockSpec((1, TILE_CH, 2 * TILE_ST), t3),
            pl.BlockSpec((1, 2 * TILE_ST, TILE_CH), t3),
        ],
        out_shape=[
            jax.ShapeDtypeStruct((nt, BATCH, TILE_ST), F32),
            jax.ShapeDtypeStruct((nt, BATCH, TILE_ST), F32),
            jax.ShapeDtypeStruct((nt, TILE_CH, 2 * TILE_ST), BF16),
            jax.ShapeDtypeStruct((nt, 2 * TILE_ST, TILE_CH), BF16),
        ],
        compiler_params=pltpu.CompilerParams(dimension_semantics=("parallel",)),
        name="ssm_prep",
    )(lanes3(a_re), lanes3(a_im), lanes3(ldt),
      b_re.reshape(nt, TILE_ST, SSM_GROUP), b_im.reshape(nt, TILE_ST, SSM_GROUP),
      c_re.reshape(nt, TILE_CH, SSM_STATE), c_im.reshape(nt, TILE_CH, SSM_STATE), ep, en)


def _inproj_kernel(x_ref, pos_ref, g_ref, invf_ref, sgn_ref, w_ref, wvt_ref, wo_ref, wglu_ref,
                   q_ref, k_ref, vt_ref, u_ref, wo_bf_ref, wglu_bf_ref):
    tm = x_ref.shape[0]
    wo_bf_ref[...] = wo_ref[...].astype(BF16)
    wglu_bf_ref[...] = wglu_ref[...].astype(BF16)

    lane = lax.broadcasted_iota(jnp.int32, (ROW_SUB, LANES), 1)
    first_half = (lane & (HEAD_DIM // 2)) == 0
    first_head = lane < HEAD_DIM
    scale = 1.0 / math.sqrt(HEAD_DIM)
    for r in range(tm // ROW_SUB):
        rs = slice(r * ROW_SUB, (r + 1) * ROW_SUB)
        xn = _rms(x_ref[rs, :], g_ref[...]).astype(BF16)
        ang = pos_ref[rs, :].astype(F32) * invf_ref[...]
        cos = jnp.cos(ang)
        sin = jnp.sin(ang) * sgn_ref[...]

        def rotary(t):
            partner = jnp.where(first_half,
                                pltpu.roll(t, LANES - HEAD_DIM // 2, 1),
                                pltpu.roll(t, HEAD_DIM // 2, 1))
            return t * cos + partner * sin

        def store_dup(ref, c, t):
            swapped = pltpu.roll(t, HEAD_DIM, 1)
            ref[rs, (2 * c) * LANES:(2 * c + 1) * LANES] = jnp.where(first_head, t, swapped).astype(BF16)
            ref[rs, (2 * c + 1) * LANES:(2 * c + 2) * LANES] = jnp.where(first_head, swapped, t).astype(BF16)

        for j in range(D_ATTN // 256):
            p = jnp.dot(xn, w_ref[:, j * 256:(j + 1) * 256], preferred_element_type=F32)
            for c in range(2):
                col = j * 256 + c * LANES
                q_ref[rs, col:col + LANES] = (rotary(p[:, c * LANES:(c + 1) * LANES]) * scale).astype(BF16)
        p = jnp.dot(xn, w_ref[:, D_ATTN:D_ATTN + D_KV], preferred_element_type=F32)
        for c in range(2):
            store_dup(k_ref, c, rotary(p[:, c * LANES:(c + 1) * LANES]))
        vt_ref[:, rs] = lax.dot_general(wvt_ref[...], xn, NT_DIMS,
                                        preferred_element_type=F32).astype(BF16)
        for j in range(D_SSM // 256):
            col = D_ATTN + 2 * D_KV + j * 256
            u_ref[rs, j * 256:(j + 1) * 256] = jnp.dot(xn, w_ref[:, col:col + 256],
                                                        preferred_element_type=F32)


def _in_proj(x2, pos2, g_pre_mix, w_in_bf, wvt_bf, w_o, w_glu, tm):
    T = x2.shape[0]
    nt = SEQ // tm
    assert BATCH * nt == N_CAST_STEPS
    wo_slab = (D_MODEL // N_CAST_STEPS, D_MODEL)
    wglu_slab = (D_SSM // N_CAST_STEPS, D_SSM)
    half = HEAD_DIM // 2
    invf = ROPE_THETA ** (-np.arange(half, dtype=np.float32) / half)
    invf = np.tile(invf.astype(np.float32), LANES // half)[None, :]
    sgn = np.tile(np.concatenate([-np.ones(half, np.float32), np.ones(half, np.float32)]),
                  LANES // HEAD_DIM)[None, :]
    row = lambda b, i: (b * nt + i, 0)
    const = lambda b, i: (0, 0)
    return pl.pallas_call(
        _inproj_kernel,
        grid=(BATCH, nt),
        in_specs=[
            pl.BlockSpec((tm, D_MODEL), row),
            pl.BlockSpec((tm, 1), row),
            pl.BlockSpec((1, D_MODEL), const),
            pl.BlockSpec((1, LANES), const),
            pl.BlockSpec((1, LANES), const),
            pl.BlockSpec((D_MODEL, D_IN), const),
            pl.BlockSpec((D_KV, D_MODEL), const),
            pl.BlockSpec(wo_slab, row),
            pl.BlockSpec(wglu_slab, row),
        ],
        out_specs=[
            pl.BlockSpec((tm, D_ATTN), row),
            pl.BlockSpec((tm, D_KV_DUP), row),
            pl.BlockSpec((D_KV, tm), lambda b, i: (0, b * nt + i)),
            pl.BlockSpec((tm, D_SSM), lambda b, i: (i, b)),
            pl.BlockSpec(wo_slab, row),
            pl.BlockSpec(wglu_slab, row),
        ],
        out_shape=[
            jax.ShapeDtypeStruct((T, D_ATTN), BF16),
            jax.ShapeDtypeStruct((T, D_KV_DUP), BF16),
            jax.ShapeDtypeStruct((D_KV, T), BF16),
            jax.ShapeDtypeStruct((SEQ, BATCH * D_SSM), F32),
            jax.ShapeDtypeStruct((D_MODEL, D_MODEL), BF16),
            jax.ShapeDtypeStruct((D_SSM, D_SSM), BF16),
        ],
        compiler_params=pltpu.CompilerParams(
            dimension_semantics=("parallel", "parallel"), vmem_limit_bytes=52 * 2**20),
        name="in_proj",
    )(x2, pos2, g_pre_mix, jnp.asarray(invf), jnp.asarray(sgn), w_in_bf, wvt_bf, w_o, w_glu)


def _attn_kernel(sinks_ref, q_ref, k_ref, vt_ref, kh_ref, vth_ref, g_ref, o_ref, acc_ref):
    tq = q_ref.shape[0]
    first_tile = pl.program_id(1) == 0
    two = 2 * WINDOW
    key = lax.broadcasted_iota(jnp.int32, (WINDOW, two), 0)
    qry = lax.broadcasted_iota(jnp.int32, (WINDOW, two), 1) % WINDOW
    use_prev = key > qry
    pair0 = lax.broadcasted_iota(jnp.int32, (1, two), 1) < WINDOW
    low_lanes = lax.broadcasted_iota(jnp.int32, (two, LANES), 1) < HEAD_DIM
    zero = jnp.zeros((), BF16)
    no_values = jnp.zeros((HEAD_DIM, two), BF16)

    def keys_of(j):
        lo, hi = j * WINDOW, (j + 1) * WINDOW
        if j == 0:
            return (jnp.concatenate([kh_ref[...], k_ref[lo:hi, :]], axis=0),
                    jnp.concatenate([vth_ref[...], vt_ref[:, lo:hi]], axis=1))
        return k_ref[lo - WINDOW:hi, :], vt_ref[:, lo - WINDOW:hi]

    def scores(j, kv, side):
        lo, hi = j * WINDOW, (j + 1) * WINDOW
        c0 = kv * Q_PER_KV * HEAD_DIM
        kd = keys_of(j)[0][:, kv * LANES:(kv + 1) * LANES]
        ks = jnp.where(low_lanes if side == 0 else jnp.logical_not(low_lanes), kd, zero)
        q2 = jnp.concatenate([q_ref[lo:hi, c0:c0 + LANES], q_ref[lo:hi, c0 + LANES:c0 + 2 * LANES]],
                             axis=0)
        return lax.dot_general(ks, q2, NT_DIMS, preferred_element_type=F32)

    def probs(j, kv, side, st):
        s_prev = st[:WINDOW, :]
        if j == 0:
            s_prev = jnp.where(first_tile, -jnp.inf, s_prev)
        sf = jnp.where(use_prev, s_prev, st[WINDOW:, :])
        sink = jnp.where(pair0, sinks_ref[kv * Q_PER_KV + side], sinks_ref[kv * Q_PER_KV + 2 + side])
        m = jnp.maximum(jnp.max(sf, axis=0, keepdims=True), sink)
        p = jnp.exp(sf - m)
        den = jnp.sum(p, axis=0, keepdims=True) + jnp.exp(sink - m)
        pn = p * (1.0 / den)
        return jnp.concatenate([jnp.where(use_prev, pn, 0.0), jnp.where(use_prev, 0.0, pn)],
                               axis=0).astype(BF16)

    def weighted_values(j, kv, side, pcat):
        vdt = keys_of(j)[1][kv * HEAD_DIM:(kv + 1) * HEAD_DIM, :]
        vst = jnp.concatenate([vdt, no_values] if side == 0 else [no_values, vdt], axis=0)
        return jnp.dot(vst, pcat, preferred_element_type=F32)

    tasks = [(j, kv, side) for j in range(tq // WINDOW) for kv in range(N_KV_HEADS) for side in range(2)]
    pending = {}
    out = None
    for n in range(len(tasks) + SCORE_AHEAD):
        if n < len(tasks):
            pending[n] = scores(*tasks[n])
        if n < SCORE_AHEAD:
            continue
        j, kv, side = tasks[n - SCORE_AHEAD]
        o = weighted_values(j, kv, side, probs(j, kv, side, pending.pop(n - SCORE_AHEAD)))
        if side == 0:
            out = o
            continue
        out = out + o
        c0 = kv * Q_PER_KV * HEAD_DIM
        acc_ref[c0:c0 + LANES, :] = out[:, :WINDOW]
        acc_ref[c0 + LANES:c0 + 2 * LANES, :] = out[:, WINDOW:]
        if kv == N_KV_HEADS - 1:
            a = acc_ref[...]
            ms = jnp.mean(a * a, axis=0, keepdims=True)
            o_ref[j * WINDOW:(j + 1) * WINDOW, :] = (
                (a * lax.rsqrt(ms + RMS_EPS)).T * g_ref[...]).astype(BF16)


def _attention(q, k, vt, sinks, g_attn_out, tq, batches):
    nt = SEQ // tq
    per = tq // WINDOW
    b0, b1 = batches
    row = lambda b, i: ((b0 + b) * nt + i, 0)
    col = lambda b, i: (0, (b0 + b) * nt + i)
    halo = lambda b, i: jnp.maximum((b0 + b) * (SEQ // WINDOW) + i * per - 1, 0)
    return pl.pallas_call(
        _attn_kernel,
        grid=(b1 - b0, nt),
        in_specs=[
            pl.BlockSpec(memory_space=pltpu.SMEM),
            pl.BlockSpec((tq, D_ATTN), row),
            pl.BlockSpec((tq, D_KV_DUP), row),
            pl.BlockSpec((D_KV, tq), col),
            pl.BlockSpec((WINDOW, D_KV_DUP), lambda b, i: (halo(b, i), 0)),
            pl.BlockSpec((D_KV, WINDOW), lambda b, i: (0, halo(b, i))),
            pl.BlockSpec((1, D_ATTN), lambda b, i: (0, 0)),
        ],
        out_specs=pl.BlockSpec((tq, D_ATTN), lambda b, i: (b * nt + i, 0)),
        out_shape=jax.ShapeDtypeStruct(((b1 - b0) * SEQ, D_ATTN), BF16),
        scratch_shapes=[pltpu.VMEM((D_ATTN, WINDOW), F32)],
        compiler_params=pltpu.CompilerParams(dimension_semantics=("parallel", "parallel")),
        name="attention",
    )(sinks, q, k, vt, k, vt, g_attn_out)


def _ssm_kernel(u_ref, bblk_ref, cblk_ref, lre_ref, lim_ref, dskip_ref, wglu_ref, bglu_ref, g_ref,
                wg_ref, wu_ref, wd_ref, o_ref, wgu_bf_ref, wd_bf_ref, s_ref, y_ref, carry_ref):
    tl = u_ref.shape[0]
    rows = tl * BATCH
    for c in range(D_FF // MXU_COLS):
        src = slice(c * MXU_COLS, (c + 1) * MXU_COLS)
        wgu_bf_ref[:, 2 * c * MXU_COLS:(2 * c + 1) * MXU_COLS] = wg_ref[:, src].astype(BF16)
        wgu_bf_ref[:, (2 * c + 1) * MXU_COLS:(2 * c + 2) * MXU_COLS] = wu_ref[:, src].astype(BF16)
    wd_bf_ref[...] = wd_ref[...].astype(BF16)

    @pl.when(pl.program_id(0) == 0)
    def _():
        carry_ref[...] = jnp.zeros_like(carry_ref)

    u = u_ref[...].reshape(rows, D_SSM)
    ub = u.astype(BF16)

    def project_in(gt):
        ch = slice(gt * TILE_CH, (gt + 1) * TILE_CH)
        s_ref[gt % 2] = jnp.dot(ub[:, ch], bblk_ref[gt], preferred_element_type=F32)

    project_in(0)
    for gt in range(N_GROUP_TILES):
        buf = gt % 2
        ch = slice(gt * TILE_CH, (gt + 1) * TILE_CH)
        if gt + 1 < N_GROUP_TILES:
            project_in(gt + 1)
        for c in range(N_CHUNKS):
            lanes = slice(c * LANES, (c + 1) * LANES)
            re_cols, im_cols = _re_cols(c), _im_cols(c)
            lr = lre_ref[gt, :, lanes]
            li = lim_ref[gt, :, lanes]
            sr = carry_ref[gt, :, re_cols]
            si = carry_ref[gt, :, im_cols]
            for t in range(tl):
                r = slice(t * BATCH, (t + 1) * BATCH)
                nr = lr * sr - li * si + s_ref[buf, r, re_cols]
                ni = lr * si + li * sr + s_ref[buf, r, im_cols]
                s_ref[buf, r, re_cols] = nr
                s_ref[buf, r, im_cols] = ni
                sr, si = nr, ni
            carry_ref[gt, :, re_cols] = sr
            carry_ref[gt, :, im_cols] = si
        y = jnp.dot(s_ref[buf].astype(BF16), cblk_ref[gt], preferred_element_type=F32)
        y_ref[:, ch] = y + dskip_ref[:, ch] * u[:, ch]
    z = jax.nn.gelu(y_ref[...])
    gate = jax.nn.sigmoid(jnp.dot(z.astype(BF16), wglu_ref[...], preferred_element_type=F32)
                          + bglu_ref[...])
    o_ref[...] = _rms(z * gate, g_ref[...]).reshape(tl, BATCH, D_SSM)


def _ssm(u3, bblk, cblk, lam_re, lam_im, d_skip, w_glu_bf, b_glu, g_ssm_out, w_gate, w_up, w_down, tl):
    rows = tl * BATCH
    n_tiles = SEQ // tl
    assert n_tiles == N_CAST_STEPS
    c2 = lambda i: (0, 0)
    c3 = lambda i: (0, 0, 0)
    cur = lambda i: i
    slab = lambda i: (cur(i), 0)
    up_slab = (D_MODEL // N_CAST_STEPS, D_FF)
    down_slab = (D_FF // N_CAST_STEPS, D_MODEL)
    return pl.pallas_call(
        _ssm_kernel,
        grid=(n_tiles,),
        in_specs=[
            pl.BlockSpec((tl, BATCH, D_SSM), lambda i: (cur(i), 0, 0)),
            pl.BlockSpec((N_GROUP_TILES, TILE_CH, 2 * TILE_ST), c3),
            pl.BlockSpec((N_GROUP_TILES, 2 * TILE_ST, TILE_CH), c3),
            pl.BlockSpec((N_GROUP_TILES, BATCH, TILE_ST), c3),
            pl.BlockSpec((N_GROUP_TILES, BATCH, TILE_ST), c3),
            pl.BlockSpec((1, D_SSM), c2),
            pl.BlockSpec((D_SSM, D_SSM), c2),
            pl.BlockSpec((1, D_SSM), c2),
            pl.BlockSpec((1, D_SSM), c2),
            pl.BlockSpec(up_slab, slab),
            pl.BlockSpec(up_slab, slab),
            pl.BlockSpec(down_slab, slab),
        ],
        out_specs=[
            pl.BlockSpec((tl, BATCH, D_SSM), lambda i: (i, 0, 0)),
            pl.BlockSpec((up_slab[0], 2 * D_FF), slab),
            pl.BlockSpec(down_slab, slab),
        ],
        out_shape=[
            jax.ShapeDtypeStruct((SEQ, BATCH, D_SSM), F32),
            jax.ShapeDtypeStruct((D_MODEL, 2 * D_FF), BF16),
            jax.ShapeDtypeStruct((D_FF, D_MODEL), BF16),
        ],
        scratch_shapes=[
            pltpu.VMEM((2, rows, 2 * TILE_ST), F32),
            pltpu.VMEM((rows, D_SSM), F32),
            pltpu.VMEM((N_GROUP_TILES, BATCH, 2 * TILE_ST), F32),
        ],
        compiler_params=pltpu.CompilerParams(
            dimension_semantics=("arbitrary",), vmem_limit_bytes=52 * 2**20),
        name="ssm",
    )(u3, bblk, cblk, lam_re, lam_im, d_skip, w_glu_bf, b_glu, g_ssm_out, w_gate, w_up, w_down)


def _outproj_kernel(attn_lo_ref, attn_hi_ref, ssm_ref, x_ref, wo_ref, gpost_ref, gpre_ref, h_ref, hn_ref):
    tm = x_ref.shape[0]
    low_batches = pl.program_id(0) < BATCH // 2
    for r in range(tm // ROW_SUB):
        rs = slice(r * ROW_SUB, (r + 1) * ROW_SUB)
        attn = jnp.where(low_batches, attn_lo_ref[rs, :], attn_hi_ref[rs, :])
        mix = jnp.dot(attn, wo_ref[:D_ATTN, :], preferred_element_type=F32)
        mix = mix + jnp.dot(ssm_ref[rs, :].astype(BF16), wo_ref[D_ATTN:, :],
                            preferred_element_type=F32)
        h = x_ref[rs, :] + _rms(mix, gpost_ref[...])
        h_ref[rs, :] = h
        hn_ref[rs, :] = _rms(h, gpre_ref[...]).astype(BF16)


def _out_proj(attn_lo, attn_hi, ssm2, x2, w_o_bf, g_post_mix, g_pre_ffn, tm):
    T = x2.shape[0]
    nt = SEQ // tm
    half = BATCH // 2
    row = lambda b, i: (b * nt + i, 0)
    const = lambda b, i: (0, 0)
    lo_row = lambda b, i: (jnp.where(b < half, b * nt + i, half * nt - 1), 0)
    hi_row = lambda b, i: (jnp.where(b < half, 0, (b - half) * nt + i), 0)
    return pl.pallas_call(
        _outproj_kernel,
        grid=(BATCH, nt),
        in_specs=[
            pl.BlockSpec((tm, D_ATTN), lo_row),
            pl.BlockSpec((tm, D_ATTN), hi_row),
            pl.BlockSpec((tm, D_SSM), lambda b, i: (i, b)),
            pl.BlockSpec((tm, D_MODEL), row),
            pl.BlockSpec((D_MODEL, D_MODEL), const),
            pl.BlockSpec((1, D_MODEL), const),
            pl.BlockSpec((1, D_MODEL), const),
        ],
        out_specs=[pl.BlockSpec((tm, D_MODEL), row), pl.BlockSpec((tm, D_MODEL), row)],
        out_shape=[jax.ShapeDtypeStruct((T, D_MODEL), F32), jax.ShapeDtypeStruct((T, D_MODEL), BF16)],
        compiler_params=pltpu.CompilerParams(
            dimension_semantics=("parallel", "parallel"), vmem_limit_bytes=52 * 2**20),
        name="out_proj",
    )(attn_lo, attn_hi, ssm2, x2, w_o_bf, g_post_mix, g_pre_ffn)


def _ffn_kernel(hn_ref, h_ref, wgu_ref, wd_ref, g_ref, o_ref, acc_ref, *, nf):
    j = pl.program_id(1)
    te = o_ref.shape[0]

    def reduce_step(first):
        n_sub = hn_ref.shape[0] // ROW_SUB
        rows = lambda r: slice(r * ROW_SUB, (r + 1) * ROW_SUB)

        def gate_up(r):
            return jnp.dot(hn_ref[rows(r), :], wgu_ref[...], preferred_element_type=F32)

        ahead = gate_up(0)
        for r in range(n_sub):
            gu = ahead
            if r + 1 < n_sub:
                ahead = gate_up(r + 1)
            hid = jnp.concatenate(
                [jax.nn.silu(gu[:, 2 * c * MXU_COLS:(2 * c + 1) * MXU_COLS])
                 * gu[:, (2 * c + 1) * MXU_COLS:(2 * c + 2) * MXU_COLS]
                 for c in range(gu.shape[1] // (2 * MXU_COLS))], axis=1).astype(BF16)
            part = jnp.dot(hid, wd_ref[...], preferred_element_type=F32)
            if first:
                acc_ref[rows(r), :] = part
            else:
                acc_ref[rows(r), :] += part

    @pl.when(j == 0)
    def _():
        reduce_step(True)

    @pl.when(jnp.logical_and(j > 0, j < nf))
    def _():
        reduce_step(False)

    @pl.when(j >= nf)
    def _():
        r = pl.multiple_of((j - nf) * te, te)
        o_ref[...] = h_ref[...] + _rms(acc_ref[pl.ds(r, te), :], g_ref[...])


def _ffn(hn, h, wgu_bf, wd_bf, g_post_ffn, tm, tf, te):
    T = h.shape[0]
    nf = D_FF // tf
    sub = tm // te
    n_tiles = T // tm
    chunk = lambda j: jnp.where(j < nf, j, 0)
    hn_row = lambda i, j: (jnp.minimum(i + (j >= nf).astype(jnp.int32), n_tiles - 1), 0)
    out_row = lambda i, j: (i * sub + jnp.clip(j - nf, 0, sub - 1), 0)
    h_row = lambda i, j: (jnp.where(j == 0, jnp.maximum(i * sub - 1, 0),
                                    i * sub + jnp.clip(j - nf, 0, sub - 1)), 0)
    return pl.pallas_call(
        functools.partial(_ffn_kernel, nf=nf),
        grid=(n_tiles, nf + sub),
        in_specs=[
            pl.BlockSpec((tm, D_MODEL), hn_row),
            pl.BlockSpec((te, D_MODEL), h_row),
            pl.BlockSpec((D_MODEL, 2 * tf), lambda i, j: (0, chunk(j))),
            pl.BlockSpec((tf, D_MODEL), lambda i, j: (chunk(j), 0)),
            pl.BlockSpec((1, D_MODEL), lambda i, j: (0, 0)),
        ],
        out_specs=pl.BlockSpec((te, D_MODEL), out_row),
        out_shape=jax.ShapeDtypeStruct((T, D_MODEL), F32),
        scratch_shapes=[pltpu.VMEM((tm, D_MODEL), F32)],
        compiler_params=pltpu.CompilerParams(
            dimension_semantics=("parallel", "arbitrary"), vmem_limit_bytes=58 * 2**20),
        name="ffn",
    )(hn, h, wgu_bf, wd_bf, g_post_ffn)


def kernel(x, positions, g_pre_mix, w_in, sinks, a_re, a_im, log_dt, b_re, b_im, c_re, c_im, d_skip,
           w_glu, b_glu, g_attn_out, g_ssm_out, w_o, g_post_mix, g_pre_ffn, w_gate, w_up, w_down,
           g_post_ffn):
    depth = w_in.shape[0]
    T = BATCH * SEQ
    h = x.reshape(T, D_MODEL)
    pos2 = positions.reshape(T, 1)
    for i in range(depth):
        lam_re, lam_im, bblk, cblk = _ssm_prep(a_re[i], a_im[i], log_dt[i], b_re[i], b_im[i],
                                               c_re[i], c_im[i])
        w_in_bf = w_in[i].astype(BF16)
        wvt_bf = jnp.transpose(w_in[i, :, D_ATTN + D_KV:D_ATTN + 2 * D_KV]).astype(BF16)
        q, k, vt, u, w_o_bf, w_glu_bf = _in_proj(h, pos2, g_pre_mix[i][None, :], w_in_bf, wvt_bf,
                                                 w_o[i], w_glu[i], tm=512)
        half = BATCH // 2
        attn_lo = _attention(q, k, vt, sinks[i], g_attn_out[i][None, :], tq=512, batches=(0, half))
        ssm_n, wgu_bf, wd_bf = _ssm(u.reshape(SEQ, BATCH, D_SSM), bblk, cblk, lam_re, lam_im,
                                    d_skip[i].reshape(1, D_SSM), w_glu_bf, b_glu[i][None, :],
                                    g_ssm_out[i][None, :], w_gate[i], w_up[i], w_down[i], tl=64)
        attn_hi = _attention(q, k, vt, sinks[i], g_attn_out[i][None, :], tq=512, batches=(half, BATCH))
        h, hn = _out_proj(attn_lo, attn_hi, ssm_n.reshape(SEQ, BATCH * D_SSM), h, w_o_bf,
                          g_post_mix[i][None, :], g_pre_ffn[i][None, :], tm=512)
        h = _ffn(hn, h, wgu_bf, wd_bf, g_post_ffn[i][None, :], tm=1024, tf=512, te=512)
    return h.reshape(BATCH, SEQ, D_MODEL)
```

```python
import functools
import math

import jax
import jax.numpy as jnp
import numpy as np
from jax import lax
from jax.experimental import pallas as pl
from jax.experimental.pallas import tpu as pltpu

D_MODEL = 2048
BATCH = 8
SEQ = 2048
HEAD_DIM = 64
D_ATTN = 1024
N_Q_HEADS = 16
N_KV_HEADS = 4
Q_PER_KV = 4
D_KV = 256
WINDOW = 128
ROPE_THETA = 10000.0
D_SSM = 1024
SSM_GROUP = 16
N_SSM_GROUPS = 64
SSM_STATE = 64
D_IN = D_ATTN + 2 * D_KV + D_SSM
D_FF = 5632
RMS_EPS = 1e-6

LANES = 128
SUBLANES = 8
GROUPS_PER_TILE = 16
N_GROUP_TILES = N_SSM_GROUPS // GROUPS_PER_TILE
TILE_CH = GROUPS_PER_TILE * SSM_GROUP
TILE_ST = GROUPS_PER_TILE * SSM_STATE
N_CHUNKS = TILE_ST // LANES
D_KV_DUP = 2 * N_KV_HEADS * HEAD_DIM
ROW_SUB = 256
MXU_COLS = 256
N_CAST_STEPS = 32
SCORE_AHEAD = 8

F32 = jnp.float32
BF16 = jnp.bfloat16
NT_DIMS = (((1,), (1,)), ((), ()))


def _re_cols(c):
    return slice(2 * c * LANES, (2 * c + 1) * LANES)


def _im_cols(c):
    return slice((2 * c + 1) * LANES, (2 * c + 2) * LANES)


def _rms(x, g):
    ms = jnp.mean(x * x, axis=-1, keepdims=True)
    return x * lax.rsqrt(ms + RMS_EPS) * g


def _split3(x):
    x1 = x.astype(BF16)
    r1 = x - x1.astype(F32)
    x2 = r1.astype(BF16)
    x3 = (r1 - x2.astype(F32)).astype(BF16)
    return x1, x2, x3


def _ssm_prep_kernel(ar_ref, ai_ref, ldt_ref, br_ref, bi_ref, cr_ref, ci_ref, ep_ref, en_ref, wv_ref,
                     lre_ref, lim_ref, bblk_ref, cblk_ref, wvt_ref):
    wvt_ref[...] = wv_ref[...].T.astype(BF16)

    ar = ar_ref[0]
    ai = ai_ref[0]
    dt = jnp.exp(ldt_ref[0])
    mag = jnp.exp(ar * dt)
    lam_re = mag * jnp.cos(ai * dt)
    lam_im = mag * jnp.sin(ai * dt)
    den = ar * ar + ai * ai
    nr = lam_re - 1.0
    ni = lam_im
    f_re = (nr * ar + ni * ai) / den
    f_im = (ni * ar - nr * ai) / den
    lre_ref[0] = jnp.broadcast_to(lam_re, (BATCH, TILE_ST))
    lim_ref[0] = jnp.broadcast_to(lam_im, (BATCH, TILE_ST))

    ep = ep_ref[...]

    def spread_b(b):
        return sum(lax.dot_general(ep, piece, NT_DIMS, preferred_element_type=F32)
                   for piece in _split3(b))

    pb_re = spread_b(br_ref[0])
    pb_im = spread_b(bi_ref[0])
    row = lax.broadcasted_iota(jnp.int32, (TILE_CH, TILE_ST), 0)
    col = lax.broadcasted_iota(jnp.int32, (TILE_CH, TILE_ST), 1)
    diag = (row // SSM_GROUP) == (col // SSM_STATE)
    bb_re = jnp.where(diag, f_re * pb_re - f_im * pb_im, 0.0).astype(BF16)
    bb_im = jnp.where(diag, f_re * pb_im + f_im * pb_re, 0.0).astype(BF16)
    for c in range(N_CHUNKS):
        bblk_ref[0, :, _re_cols(c)] = bb_re[:, c * LANES:(c + 1) * LANES]
        bblk_ref[0, :, _im_cols(c)] = bb_im[:, c * LANES:(c + 1) * LANES]

    en = en_ref[...]
    row = lax.broadcasted_iota(jnp.int32, (TILE_ST, TILE_CH), 0)
    col = lax.broadcasted_iota(jnp.int32, (TILE_ST, TILE_CH), 1)
    diag = (row // SSM_STATE) == (col // SSM_GROUP)
    pc_re = lax.dot_general(en, cr_ref[0].astype(BF16), NT_DIMS, preferred_element_type=F32)
    pc_im = lax.dot_general(en, ci_ref[0].astype(BF16), NT_DIMS, preferred_element_type=F32)
    cc_re = jnp.where(diag, pc_re, 0.0).astype(BF16)
    cc_im = jnp.where(diag, -pc_im, 0.0).astype(BF16)
    for c in range(N_CHUNKS):
        cblk_ref[0, _re_cols(c), :] = cc_re[c * LANES:(c + 1) * LANES, :]
        cblk_ref[0, _im_cols(c), :] = cc_im[c * LANES:(c + 1) * LANES, :]


def _ssm_prep(a_re, a_im, log_dt, b_re, b_im, c_re, c_im, w_in):
    nt = N_GROUP_TILES
    wv_rows = D_MODEL // nt
    wv_col_block = (D_ATTN + D_KV) // D_KV
    lanes3 = lambda a: a.reshape(nt, 1, TILE_ST)
    ldt = jnp.broadcast_to(log_dt[:, None], (N_SSM_GROUPS, SSM_STATE))
    ep = jnp.asarray(np.tile(np.eye(SSM_GROUP, dtype=np.float32), (GROUPS_PER_TILE, 1)), BF16)
    en = jnp.asarray(np.tile(np.eye(SSM_STATE, dtype=np.float32), (GROUPS_PER_TILE, 1)), BF16)
    t3 = lambda t: (t, 0, 0)
    c2 = lambda t: (0, 0)
    return pl.pallas_call(
        _ssm_prep_kernel,
        grid=(nt,),
        in_specs=[
            pl.BlockSpec((1, 1, TILE_ST), t3),
            pl.BlockSpec((1, 1, TILE_ST), t3),
            pl.BlockSpec((1, 1, TILE_ST), t3),
            pl.BlockSpec((1, TILE_ST, SSM_GROUP), t3),
            pl.BlockSpec((1, TILE_ST, SSM_GROUP), t3),
            pl.BlockSpec((1, TILE_CH, SSM_STATE), t3),
            pl.BlockSpec((1, TILE_CH, SSM_STATE), t3),
            pl.BlockSpec((TILE_CH, SSM_GROUP), c2),
            pl.BlockSpec((TILE_ST, SSM_STATE), c2),
            pl.BlockSpec((wv_rows, D_KV), lambda t: (t, wv_col_block)),
        ],
        out_specs=[
            pl.BlockSpec((1, BATCH, TILE_ST), t3),
            pl.BlockSpec((1, BATCH, TILE_ST), t3),
            pl.BlockSpec((1, TILE_CH, 2 * TILE_ST), t3),
            pl.BlockSpec((1, 2 * TILE_ST, TILE_CH), t3),
            pl.BlockSpec((D_KV, wv_rows), lambda t: (0, t)),
        ],
        out_shape=[
            jax.ShapeDtypeStruct((nt, BATCH, TILE_ST), F32),
            jax.ShapeDtypeStruct((nt, BATCH, TILE_ST), F32),
            jax.ShapeDtypeStruct((nt, TILE_CH, 2 * TILE_ST), BF16),
            jax.ShapeDtypeStruct((nt, 2 * TILE_ST, TILE_CH), BF16),
            jax.ShapeDtypeStruct((D_KV, D_MODEL), BF16),
        ],
        compiler_params=pltpu.CompilerParams(dimension_semantics=("parallel",)),
        name="ssm_prep",
    )(lanes3(a_re), lanes3(a_im), lanes3(ldt),
      b_re.reshape(nt, TILE_ST, SSM_GROUP), b_im.reshape(nt, TILE_ST, SSM_GROUP),
      c_re.reshape(nt, TILE_CH, SSM_STATE), c_im.reshape(nt, TILE_CH, SSM_STATE), ep, en, w_in)


def _inproj_kernel(x_ref, pos_ref, g_ref, invf_ref, sgn_ref, w_ref, wvt_ref, wo_ref, wglu_ref,
                   q_ref, k_ref, vt_ref, u_ref, wo_bf_ref, wglu_bf_ref):
    tm = x_ref.shape[0]
    wo_bf_ref[...] = wo_ref[...].astype(BF16)
    wglu_bf_ref[...] = wglu_ref[...].astype(BF16)

    lane = lax.broadcasted_iota(jnp.int32, (ROW_SUB, LANES), 1)
    first_half = (lane & (HEAD_DIM // 2)) == 0
    first_head = lane < HEAD_DIM
    scale = 1.0 / math.sqrt(HEAD_DIM)
    for r in range(tm // ROW_SUB):
        rs = slice(r * ROW_SUB, (r + 1) * ROW_SUB)
        xn = _rms(x_ref[rs, :], g_ref[...]).astype(BF16)
        ang = pos_ref[rs, :].astype(F32) * invf_ref[...]
        cos = jnp.cos(ang)
        sin = jnp.sin(ang) * sgn_ref[...]

        def rotary(t):
            partner = jnp.where(first_half,
                                pltpu.roll(t, LANES - HEAD_DIM // 2, 1),
                                pltpu.roll(t, HEAD_DIM // 2, 1))
            return t * cos + partner * sin

        def store_dup(ref, c, t):
            swapped = pltpu.roll(t, HEAD_DIM, 1)
            ref[rs, (2 * c) * LANES:(2 * c + 1) * LANES] = jnp.where(first_head, t, swapped).astype(BF16)
            ref[rs, (2 * c + 1) * LANES:(2 * c + 2) * LANES] = jnp.where(first_head, swapped, t).astype(BF16)

        for j in range(D_ATTN // 256):
            p = jnp.dot(xn, w_ref[:, j * 256:(j + 1) * 256], preferred_element_type=F32)
            for c in range(2):
                col = j * 256 + c * LANES
                q_ref[rs, col:col + LANES] = (rotary(p[:, c * LANES:(c + 1) * LANES]) * scale).astype(BF16)
        p = jnp.dot(xn, w_ref[:, D_ATTN:D_ATTN + D_KV], preferred_element_type=F32)
        for c in range(2):
            store_dup(k_ref, c, rotary(p[:, c * LANES:(c + 1) * LANES]))
        vt_ref[:, rs] = lax.dot_general(wvt_ref[...], xn, NT_DIMS,
                                        preferred_element_type=F32).astype(BF16)
        for j in range(D_SSM // 256):
            col = D_ATTN + 2 * D_KV + j * 256
            u_ref[rs, j * 256:(j + 1) * 256] = jnp.dot(xn, w_ref[:, col:col + 256],
                                                        preferred_element_type=F32)


def _in_proj(x2, pos2, g_pre_mix, w_in_bf, wvt_bf, w_o, w_glu, tm):
    T = x2.shape[0]
    nt = SEQ // tm
    assert BATCH * nt == N_CAST_STEPS
    wo_slab = (D_MODEL // N_CAST_STEPS, D_MODEL)
    wglu_slab = (D_SSM // N_CAST_STEPS, D_SSM)
    half = HEAD_DIM // 2
    invf = ROPE_THETA ** (-np.arange(half, dtype=np.float32) / half)
    invf = np.tile(invf.astype(np.float32), LANES // half)[None, :]
    sgn = np.tile(np.concatenate([-np.ones(half, np.float32), np.ones(half, np.float32)]),
                  LANES // HEAD_DIM)[None, :]
    row = lambda b, i: (b * nt + i, 0)
    const = lambda b, i: (0, 0)
    return pl.pallas_call(
        _inproj_kernel,
        grid=(BATCH, nt),
        in_specs=[
            pl.BlockSpec((tm, D_MODEL), row),
            pl.BlockSpec((tm, 1), row),
            pl.BlockSpec((1, D_MODEL), const),
            pl.BlockSpec((1, LANES), const),
            pl.BlockSpec((1, LANES), const),
            pl.BlockSpec((D_MODEL, D_IN), const),
            pl.BlockSpec((D_KV, D_MODEL), const),
            pl.BlockSpec(wo_slab, row),
            pl.BlockSpec(wglu_slab, row),
        ],
        out_specs=[
            pl.BlockSpec((tm, D_ATTN), row),
            pl.BlockSpec((tm, D_KV_DUP), row),
            pl.BlockSpec((D_KV, tm), lambda b, i: (0, b * nt + i)),
            pl.BlockSpec((tm, D_SSM), lambda b, i: (i, b)),
            pl.BlockSpec(wo_slab, row),
            pl.BlockSpec(wglu_slab, row),
        ],
        out_shape=[
            jax.ShapeDtypeStruct((T, D_ATTN), BF16),
            jax.ShapeDtypeStruct((T, D_KV_DUP), BF16),
            jax.ShapeDtypeStruct((D_KV, T), BF16),
            jax.ShapeDtypeStruct((SEQ, BATCH * D_SSM), F32),
            jax.ShapeDtypeStruct((D_MODEL, D_MODEL), BF16),
            jax.ShapeDtypeStruct((D_SSM, D_SSM), BF16),
        ],
        compiler_params=pltpu.CompilerParams(
            dimension_semantics=("parallel", "parallel"), vmem_limit_bytes=52 * 2**20),
        name="in_proj",
    )(x2, pos2, g_pre_mix, jnp.asarray(invf), jnp.asarray(sgn), w_in_bf, wvt_bf, w_o, w_glu)


def _attn_kernel(sinks_ref, q_ref, k_ref, vt_ref, kh_ref, vth_ref, g_ref, o_ref, acc_ref):
    tq = q_ref.shape[0]
    first_tile = pl.program_id(1) == 0
    two = 2 * WINDOW
    key = lax.broadcasted_iota(jnp.int32, (WINDOW, two), 0)
    qry = lax.broadcasted_iota(jnp.int32, (WINDOW, two), 1) % WINDOW
    use_prev = key > qry
    pair0 = lax.broadcasted_iota(jnp.int32, (1, two), 1) < WINDOW
    low_lanes = lax.broadcasted_iota(jnp.int32, (two, LANES), 1) < HEAD_DIM
    zero = jnp.zeros((), BF16)
    no_values = jnp.zeros((HEAD_DIM, two), BF16)

    def keys_of(j):
        lo, hi = j * WINDOW, (j + 1) * WINDOW
        if j == 0:
            return (jnp.concatenate([kh_ref[...], k_ref[lo:hi, :]], axis=0),
                    jnp.concatenate([vth_ref[...], vt_ref[:, lo:hi]], axis=1))
        return k_ref[lo - WINDOW:hi, :], vt_ref[:, lo - WINDOW:hi]

    def scores(j, kv, side):
        lo, hi = j * WINDOW, (j + 1) * WINDOW
        c0 = kv * Q_PER_KV * HEAD_DIM
        kd = keys_of(j)[0][:, kv * LANES:(kv + 1) * LANES]
        ks = jnp.where(low_lanes if side == 0 else jnp.logical_not(low_lanes), kd, zero)
        q2 = jnp.concatenate([q_ref[lo:hi, c0:c0 + LANES], q_ref[lo:hi, c0 + LANES:c0 + 2 * LANES]],
                             axis=0)
        return lax.dot_general(ks, q2, NT_DIMS, preferred_element_type=F32)

    def probs(j, kv, side, st):
        s_prev = st[:WINDOW, :]
        if j == 0:
            s_prev = jnp.where(first_tile, -jnp.inf, s_prev)
        sf = jnp.where(use_prev, s_prev, st[WINDOW:, :])
        sink = jnp.where(pair0, sinks_ref[kv * Q_PER_KV + side], sinks_ref[kv * Q_PER_KV + 2 + side])
        m = jnp.maximum(jnp.max(sf, axis=0, keepdims=True), sink)
        p = jnp.exp(sf - m)
        den = jnp.sum(p, axis=0, keepdims=True) + jnp.exp(sink - m)
        pn = p * (1.0 / den)
        return jnp.concatenate([jnp.where(use_prev, pn, 0.0), jnp.where(use_prev, 0.0, pn)],
                               axis=0).astype(BF16)

    def weighted_values(j, kv, side, pcat):
        vdt = keys_of(j)[1][kv * HEAD_DIM:(kv + 1) * HEAD_DIM, :]
        vst = jnp.concatenate([vdt, no_values] if side == 0 else [no_values, vdt], axis=0)
        return jnp.dot(vst, pcat, preferred_element_type=F32)

    tasks = [(j, kv, side) for j in range(tq // WINDOW) for kv in range(N_KV_HEADS) for side in range(2)]
    pending = {}
    out = None
    for n in range(len(tasks) + SCORE_AHEAD):
        if n < len(tasks):
            pending[n] = scores(*tasks[n])
        if n < SCORE_AHEAD:
            continue
        j, kv, side = tasks[n - SCORE_AHEAD]
        o = weighted_values(j, kv, side, probs(j, kv, side, pending.pop(n - SCORE_AHEAD)))
        if side == 0:
            out = o
            continue
        out = out + o
        c0 = kv * Q_PER_KV * HEAD_DIM
        acc_ref[c0:c0 + LANES, :] = out[:, :WINDOW]
        acc_ref[c0 + LANES:c0 + 2 * LANES, :] = out[:, WINDOW:]
        if kv == N_KV_HEADS - 1:
            a = acc_ref[...]
            ms = jnp.mean(a * a, axis=0, keepdims=True)
            o_ref[j * WINDOW:(j + 1) * WINDOW, :] = (
                (a * lax.rsqrt(ms + RMS_EPS)).T * g_ref[...]).astype(BF16)


def _attention(q, k, vt, sinks, g_attn_out, tq, batches):
    nt = SEQ // tq
    per = tq // WINDOW
    b0, b1 = batches
    row = lambda b, i: ((b0 + b) * nt + i, 0)
    col = lambda b, i: (0, (b0 + b) * nt + i)
    halo = lambda b, i: jnp.maximum((b0 + b) * (SEQ // WINDOW) + i * per - 1, 0)
    return pl.pallas_call(
        _attn_kernel,
        grid=(b1 - b0, nt),
        in_specs=[
            pl.BlockSpec(memory_space=pltpu.SMEM),
            pl.BlockSpec((tq, D_ATTN), row),
            pl.BlockSpec((tq, D_KV_DUP), row),
            pl.BlockSpec((D_KV, tq), col),
            pl.BlockSpec((WINDOW, D_KV_DUP), lambda b, i: (halo(b, i), 0)),
            pl.BlockSpec((D_KV, WINDOW), lambda b, i: (0, halo(b, i))),
            pl.BlockSpec((1, D_ATTN), lambda b, i: (0, 0)),
        ],
        out_specs=pl.BlockSpec((tq, D_ATTN), lambda b, i: (b * nt + i, 0)),
        out_shape=jax.ShapeDtypeStruct(((b1 - b0) * SEQ, D_ATTN), BF16),
        scratch_shapes=[pltpu.VMEM((D_ATTN, WINDOW), F32)],
        compiler_params=pltpu.CompilerParams(dimension_semantics=("parallel", "parallel")),
        name="attention",
    )(sinks, q, k, vt, k, vt, g_attn_out)


def _ssm_kernel(u_ref, bblk_ref, cblk_ref, lre_ref, lim_ref, dskip_ref, wglu_ref, bglu_ref, g_ref,
                wg_ref, wu_ref, wd_ref, after_ref, o_ref, wgu_bf_ref, wd_bf_ref, s_ref, y_ref, carry_ref):
    del after_ref
    tl = u_ref.shape[0]
    rows = tl * BATCH
    for c in range(D_FF // MXU_COLS):
        src = slice(c * MXU_COLS, (c + 1) * MXU_COLS)
        wgu_bf_ref[:, 2 * c * MXU_COLS:(2 * c + 1) * MXU_COLS] = wg_ref[:, src].astype(BF16)
        wgu_bf_ref[:, (2 * c + 1) * MXU_COLS:(2 * c + 2) * MXU_COLS] = wu_ref[:, src].astype(BF16)
    wd_bf_ref[...] = wd_ref[...].astype(BF16)

    @pl.when(pl.program_id(0) == 0)
    def _():
        carry_ref[...] = jnp.zeros_like(carry_ref)

    u = u_ref[...].reshape(rows, D_SSM)
    ub = u.astype(BF16)

    def project_in(gt):
        ch = slice(gt * TILE_CH, (gt + 1) * TILE_CH)
        s_ref[gt % 2] = jnp.dot(ub[:, ch], bblk_ref[gt], preferred_element_type=F32)

    project_in(0)
    for gt in range(N_GROUP_TILES):
        buf = gt % 2
        ch = slice(gt * TILE_CH, (gt + 1) * TILE_CH)
        if gt + 1 < N_GROUP_TILES:
            project_in(gt + 1)
        for c in range(N_CHUNKS):
            lanes = slice(c * LANES, (c + 1) * LANES)
            re_cols, im_cols = _re_cols(c), _im_cols(c)
            lr = lre_ref[gt, :, lanes]
            li = lim_ref[gt, :, lanes]
            sr = carry_ref[gt, :, re_cols]
            si = carry_ref[gt, :, im_cols]
            for t in range(tl):
                r = slice(t * BATCH, (t + 1) * BATCH)
                nr = lr * sr - li * si + s_ref[buf, r, re_cols]
                ni = lr * si + li * sr + s_ref[buf, r, im_cols]
                s_ref[buf, r, re_cols] = nr
                s_ref[buf, r, im_cols] = ni
                sr, si = nr, ni
            carry_ref[gt, :, re_cols] = sr
            carry_ref[gt, :, im_cols] = si
        y = jnp.dot(s_ref[buf].astype(BF16), cblk_ref[gt], preferred_element_type=F32)
        y_ref[:, ch] = y + dskip_ref[:, ch] * u[:, ch]
    z = jax.nn.gelu(y_ref[...])
    gate = jax.nn.sigmoid(jnp.dot(z.astype(BF16), wglu_ref[...], preferred_element_type=F32)
                          + bglu_ref[...])
    o_ref[...] = _rms(z * gate, g_ref[...]).reshape(tl, BATCH, D_SSM)


def _ssm(u3, bblk, cblk, lam_re, lam_im, d_skip, w_glu_bf, b_glu, g_ssm_out, w_gate, w_up, w_down,
         run_after, tl):
    rows = tl * BATCH
    n_tiles = SEQ // tl
    assert n_tiles == N_CAST_STEPS
    c2 = lambda i: (0, 0)
    c3 = lambda i: (0, 0, 0)
    cur = lambda i: i
    slab = lambda i: (cur(i), 0)
    up_slab = (D_MODEL // N_CAST_STEPS, D_FF)
    down_slab = (D_FF // N_CAST_STEPS, D_MODEL)
    return pl.pallas_call(
        _ssm_kernel,
        grid=(n_tiles,),
        in_specs=[
            pl.BlockSpec((tl, BATCH, D_SSM), lambda i: (cur(i), 0, 0)),
            pl.BlockSpec((N_GROUP_TILES, TILE_CH, 2 * TILE_ST), c3),
            pl.BlockSpec((N_GROUP_TILES, 2 * TILE_ST, TILE_CH), c3),
            pl.BlockSpec((N_GROUP_TILES, BATCH, TILE_ST), c3),
            pl.BlockSpec((N_GROUP_TILES, BATCH, TILE_ST), c3),
            pl.BlockSpec((1, D_SSM), c2),
            pl.BlockSpec((D_SSM, D_SSM), c2),
            pl.BlockSpec((1, D_SSM), c2),
            pl.BlockSpec((1, D_SSM), c2),
            pl.BlockSpec(up_slab, slab),
            pl.BlockSpec(up_slab, slab),
            pl.BlockSpec(down_slab, slab),
            pl.BlockSpec(memory_space=pl.ANY),
        ],
        out_specs=[
            pl.BlockSpec((tl, BATCH, D_SSM), lambda i: (i, 0, 0)),
            pl.BlockSpec((up_slab[0], 2 * D_FF), slab),
            pl.BlockSpec(down_slab, slab),
        ],
        out_shape=[
            jax.ShapeDtypeStruct((SEQ, BATCH, D_SSM), F32),
            jax.ShapeDtypeStruct((D_MODEL, 2 * D_FF), BF16),
            jax.ShapeDtypeStruct((D_FF, D_MODEL), BF16),
        ],
        scratch_shapes=[
            pltpu.VMEM((2, rows, 2 * TILE_ST), F32),
            pltpu.VMEM((rows, D_SSM), F32),
            pltpu.VMEM((N_GROUP_TILES, BATCH, 2 * TILE_ST), F32),
        ],
        compiler_params=pltpu.CompilerParams(
            dimension_semantics=("arbitrary",), vmem_limit_bytes=52 * 2**20),
        name="ssm",
    )(u3, bblk, cblk, lam_re, lam_im, d_skip, w_glu_bf, b_glu, g_ssm_out, w_gate, w_up, w_down, run_after)


def _outproj_kernel(attn_lo_ref, attn_hi_ref, ssm_ref, x_ref, wo_ref, gpost_ref, gpre_ref, h_ref, hn_ref):
    tm = x_ref.shape[0]
    low_batches = pl.program_id(0) < BATCH // 2
    for r in range(tm // ROW_SUB):
        rs = slice(r * ROW_SUB, (r + 1) * ROW_SUB)
        attn = jnp.where(low_batches, attn_lo_ref[rs, :], attn_hi_ref[rs, :])
        mix = jnp.dot(attn, wo_ref[:D_ATTN, :], preferred_element_type=F32)
        mix = mix + jnp.dot(ssm_ref[rs, :].astype(BF16), wo_ref[D_ATTN:, :],
                            preferred_element_type=F32)
        h = x_ref[rs, :] + _rms(mix, gpost_ref[...])
        h_ref[rs, :] = h
        hn_ref[rs, :] = _rms(h, gpre_ref[...]).astype(BF16)


def _out_proj(attn_lo, attn_hi, ssm2, x2, w_o_bf, g_post_mix, g_pre_ffn, tm):
    T = x2.shape[0]
    nt = SEQ // tm
    half = BATCH // 2
    row = lambda b, i: (b * nt + i, 0)
    const = lambda b, i: (0, 0)
    lo_row = lambda b, i: (jnp.where(b < half, b * nt + i, half * nt - 1), 0)
    hi_row = lambda b, i: (jnp.where(b < half, 0, (b - half) * nt + i), 0)
    return pl.pallas_call(
        _outproj_kernel,
        grid=(BATCH, nt),
        in_specs=[
            pl.BlockSpec((tm, D_ATTN), lo_row),
            pl.BlockSpec((tm, D_ATTN), hi_row),
            pl.BlockSpec((tm, D_SSM), lambda b, i: (i, b)),
            pl.BlockSpec((tm, D_MODEL), row),
            pl.BlockSpec((D_MODEL, D_MODEL), const),
            pl.BlockSpec((1, D_MODEL), const),
            pl.BlockSpec((1, D_MODEL), const),
        ],
        out_specs=[pl.BlockSpec((tm, D_MODEL), row), pl.BlockSpec((tm, D_MODEL), row)],
        out_shape=[jax.ShapeDtypeStruct((T, D_MODEL), F32), jax.ShapeDtypeStruct((T, D_MODEL), BF16)],
        compiler_params=pltpu.CompilerParams(
            dimension_semantics=("parallel", "parallel"), vmem_limit_bytes=52 * 2**20),
        name="out_proj",
    )(attn_lo, attn_hi, ssm2, x2, w_o_bf, g_post_mix, g_pre_ffn)


def _ffn_kernel(hn_ref, h_ref, wgu_ref, wd_ref, g_ref, o_ref, acc_ref, *, nf):
    j = pl.program_id(1)
    te = o_ref.shape[0]

    def reduce_step(first):
        n_sub = hn_ref.shape[0] // ROW_SUB
        rows = lambda r: slice(r * ROW_SUB, (r + 1) * ROW_SUB)

        def gate_up(r):
            return jnp.dot(hn_ref[rows(r), :], wgu_ref[...], preferred_element_type=F32)

        ahead = gate_up(0)
        for r in range(n_sub):
            gu = ahead
            if r + 1 < n_sub:
                ahead = gate_up(r + 1)
            hid = jnp.concatenate(
                [jax.nn.silu(gu[:, 2 * c * MXU_COLS:(2 * c + 1) * MXU_COLS])
                 * gu[:, (2 * c + 1) * MXU_COLS:(2 * c + 2) * MXU_COLS]
                 for c in range(gu.shape[1] // (2 * MXU_COLS))], axis=1).astype(BF16)
            part = jnp.dot(hid, wd_ref[...], preferred_element_type=F32)
            if first:
                acc_ref[rows(r), :] = part
            else:
                acc_ref[rows(r), :] += part

    @pl.when(j == 0)
    def _():
        reduce_step(True)

    @pl.when(jnp.logical_and(j > 0, j < nf))
    def _():
        reduce_step(False)

    @pl.when(j >= nf)
    def _():
        r = pl.multiple_of((j - nf) * te, te)
        o_ref[...] = h_ref[...] + _rms(acc_ref[pl.ds(r, te), :], g_ref[...])


def _ffn(hn, h, wgu_bf, wd_bf, g_post_ffn, tm, tf, te):
    T = h.shape[0]
    nf = D_FF // tf
    sub = tm // te
    n_tiles = T // tm
    chunk = lambda j: jnp.where(j < nf, j, 0)
    hn_row = lambda i, j: (jnp.minimum(i + (j >= nf).astype(jnp.int32), n_tiles - 1), 0)
    out_row = lambda i, j: (i * sub + jnp.clip(j - nf, 0, sub - 1), 0)
    h_row = lambda i, j: (jnp.where(j == 0, jnp.maximum(i * sub - 1, 0),
                                    i * sub + jnp.clip(j - nf, 0, sub - 1)), 0)
    return pl.pallas_call(
        functools.partial(_ffn_kernel, nf=nf),
        grid=(n_tiles, nf + sub),
        in_specs=[
            pl.BlockSpec((tm, D_MODEL), hn_row),
            pl.BlockSpec((te, D_MODEL), h_row),
            pl.BlockSpec((D_MODEL, 2 * tf), lambda i, j: (0, chunk(j))),
            pl.BlockSpec((tf, D_MODEL), lambda i, j: (chunk(j), 0)),
            pl.BlockSpec((1, D_MODEL), lambda i, j: (0, 0)),
        ],
        out_specs=pl.BlockSpec((te, D_MODEL), out_row),
        out_shape=jax.ShapeDtypeStruct((T, D_MODEL), F32),
        scratch_shapes=[pltpu.VMEM((tm, D_MODEL), F32)],
        compiler_params=pltpu.CompilerParams(
            dimension_semantics=("parallel", "arbitrary"), vmem_limit_bytes=58 * 2**20),
        name="ffn",
    )(hn, h, wgu_bf, wd_bf, g_post_ffn)


def kernel(x, positions, g_pre_mix, w_in, sinks, a_re, a_im, log_dt, b_re, b_im, c_re, c_im, d_skip,
           w_glu, b_glu, g_attn_out, g_ssm_out, w_o, g_post_mix, g_pre_ffn, w_gate, w_up, w_down,
           g_post_ffn):
    depth = w_in.shape[0]
    T = BATCH * SEQ
    h = x.reshape(T, D_MODEL)
    pos2 = positions.reshape(T, 1)
    for i in range(depth):
        lam_re, lam_im, bblk, cblk, wvt_bf = _ssm_prep(a_re[i], a_im[i], log_dt[i], b_re[i], b_im[i],
                                                       c_re[i], c_im[i], w_in[i])
        w_in_bf = w_in[i].astype(BF16)
        q, k, vt, u, w_o_bf, w_glu_bf = _in_proj(h, pos2, g_pre_mix[i][None, :], w_in_bf, wvt_bf,
                                                 w_o[i], w_glu[i], tm=512)
        half = BATCH // 2
        attn_lo = _attention(q, k, vt, sinks[i], g_attn_out[i][None, :], tq=512, batches=(0, half))
        ssm_n, wgu_bf, wd_bf = _ssm(u.reshape(SEQ, BATCH, D_SSM), bblk, cblk, lam_re, lam_im,
                                    d_skip[i].reshape(1, D_SSM), w_glu_bf, b_glu[i][None, :],
                                    g_ssm_out[i][None, :], w_gate[i], w_up[i], w_down[i],
                                    run_after=attn_lo, tl=64)
        attn_hi = _attention(q, k, vt, sinks[i], g_attn_out[i][None, :], tq=512, batches=(half, BATCH))
        h, hn = _out_proj(attn_lo, attn_hi, ssm_n.reshape(SEQ, BATCH * D_SSM), h, w_o_bf,
                          g_post_mix[i][None, :], g_pre_ffn[i][None, :], tm=512)
        h = _ffn(hn, h, wgu_bf, wd_bf, g_post_ffn[i][None, :], tm=1024, tf=512, te=512)
    return h.reshape(BATCH, SEQ, D_MODEL)
```

```python
import functools
import math

import jax
import jax.numpy as jnp
import numpy as np
from jax import lax
from jax.experimental import pallas as pl
from jax.experimental.pallas import tpu as pltpu

D_MODEL = 2048
BATCH = 8
SEQ = 2048
HEAD_DIM = 64
D_ATTN = 1024
N_Q_HEADS = 16
N_KV_HEADS = 4
Q_PER_KV = 4
D_KV = 256
WINDOW = 128
ROPE_THETA = 10000.0
D_SSM = 1024
SSM_GROUP = 16
N_SSM_GROUPS = 64
SSM_STATE = 64
D_IN = D_ATTN + 2 * D_KV + D_SSM
D_FF = 5632
RMS_EPS = 1e-6

LANES = 128
SUBLANES = 8
GROUPS_PER_TILE = 16
N_GROUP_TILES = N_SSM_GROUPS // GROUPS_PER_TILE
TILE_CH = GROUPS_PER_TILE * SSM_GROUP
TILE_ST = GROUPS_PER_TILE * SSM_STATE
N_CHUNKS = TILE_ST // LANES
D_KV_DUP = 2 * N_KV_HEADS * HEAD_DIM
ROW_SUB = 256
NORM_ROWS = 16
MXU_COLS = 256
N_CAST_STEPS = 32
SCORE_AHEAD = 8

F32 = jnp.float32
BF16 = jnp.bfloat16
NT_DIMS = (((1,), (1,)), ((), ()))


def _re_cols(c):
    return slice(2 * c * LANES, (2 * c + 1) * LANES)


def _im_cols(c):
    return slice((2 * c + 1) * LANES, (2 * c + 2) * LANES)


def _rms(x, g):
    ms = jnp.mean(x * x, axis=-1, keepdims=True)
    return x * lax.rsqrt(ms + RMS_EPS) * g


def _split3(x):
    x1 = x.astype(BF16)
    r1 = x - x1.astype(F32)
    x2 = r1.astype(BF16)
    x3 = (r1 - x2.astype(F32)).astype(BF16)
    return x1, x2, x3


def _ssm_prep_kernel(ar_ref, ai_ref, ldt_ref, br_ref, bi_ref, cr_ref, ci_ref, ep_ref, en_ref, wv_ref,
                     lre_ref, lim_ref, bblk_ref, cblk_ref, wvt_ref):
    wvt_ref[...] = wv_ref[...].T.astype(BF16)

    ar = ar_ref[0]
    ai = ai_ref[0]
    dt = jnp.exp(ldt_ref[0])
    mag = jnp.exp(ar * dt)
    lam_re = mag * jnp.cos(ai * dt)
    lam_im = mag * jnp.sin(ai * dt)
    den = ar * ar + ai * ai
    nr = lam_re - 1.0
    ni = lam_im
    f_re = (nr * ar + ni * ai) / den
    f_im = (ni * ar - nr * ai) / den
    lre_ref[0] = jnp.broadcast_to(lam_re, (BATCH, TILE_ST))
    lim_ref[0] = jnp.broadcast_to(lam_im, (BATCH, TILE_ST))

    ep = ep_ref[...]

    def spread_b(b):
        return sum(lax.dot_general(ep, piece, NT_DIMS, preferred_element_type=F32)
                   for piece in _split3(b))

    pb_re = spread_b(br_ref[0])
    pb_im = spread_b(bi_ref[0])
    row = lax.broadcasted_iota(jnp.int32, (TILE_CH, TILE_ST), 0)
    col = lax.broadcasted_iota(jnp.int32, (TILE_CH, TILE_ST), 1)
    diag = (row // SSM_GROUP) == (col // SSM_STATE)
    bb_re = jnp.where(diag, f_re * pb_re - f_im * pb_im, 0.0).astype(BF16)
    bb_im = jnp.where(diag, f_re * pb_im + f_im * pb_re, 0.0).astype(BF16)
    for c in range(N_CHUNKS):
        bblk_ref[0, :, _re_cols(c)] = bb_re[:, c * LANES:(c + 1) * LANES]
        bblk_ref[0, :, _im_cols(c)] = bb_im[:, c * LANES:(c + 1) * LANES]

    en = en_ref[...]
    row = lax.broadcasted_iota(jnp.int32, (TILE_ST, TILE_CH), 0)
    col = lax.broadcasted_iota(jnp.int32, (TILE_ST, TILE_CH), 1)
    diag = (row // SSM_STATE) == (col // SSM_GROUP)
    pc_re = lax.dot_general(en, cr_ref[0].astype(BF16), NT_DIMS, preferred_element_type=F32)
    pc_im = lax.dot_general(en, ci_ref[0].astype(BF16), NT_DIMS, preferred_element_type=F32)
    cc_re = jnp.where(diag, pc_re, 0.0).astype(BF16)
    cc_im = jnp.where(diag, -pc_im, 0.0).astype(BF16)
    for c in range(N_CHUNKS):
        cblk_ref[0, _re_cols(c), :] = cc_re[c * LANES:(c + 1) * LANES, :]
        cblk_ref[0, _im_cols(c), :] = cc_im[c * LANES:(c + 1) * LANES, :]


def _ssm_prep(a_re, a_im, log_dt, b_re, b_im, c_re, c_im, w_in):
    nt = N_GROUP_TILES
    wv_rows = D_MODEL // nt
    wv_col_block = (D_ATTN + D_KV) // D_KV
    lanes3 = lambda a: a.reshape(nt, 1, TILE_ST)
    ldt = jnp.broadcast_to(log_dt[:, None], (N_SSM_GROUPS, SSM_STATE))
    ep = jnp.asarray(np.tile(np.eye(SSM_GROUP, dtype=np.float32), (GROUPS_PER_TILE, 1)), BF16)
    en = jnp.asarray(np.tile(np.eye(SSM_STATE, dtype=np.float32), (GROUPS_PER_TILE, 1)), BF16)
    t3 = lambda t: (t, 0, 0)
    c2 = lambda t: (0, 0)
    return pl.pallas_call(
        _ssm_prep_kernel,
        grid=(nt,),
        in_specs=[
            pl.BlockSpec((1, 1, TILE_ST), t3),
            pl.BlockSpec((1, 1, TILE_ST), t3),
            pl.BlockSpec((1, 1, TILE_ST), t3),
            pl.BlockSpec((1, TILE_ST, SSM_GROUP), t3),
            pl.BlockSpec((1, TILE_ST, SSM_GROUP), t3),
            pl.BlockSpec((1, TILE_CH, SSM_STATE), t3),
            pl.BlockSpec((1, TILE_CH, SSM_STATE), t3),
            pl.BlockSpec((TILE_CH, SSM_GROUP), c2),
            pl.BlockSpec((TILE_ST, SSM_STATE), c2),
            pl.BlockSpec((wv_rows, D_KV), lambda t: (t, wv_col_block)),
        ],
        out_specs=[
            pl.BlockSpec((1, BATCH, TILE_ST), t3),
            pl.BlockSpec((1, BATCH, TILE_ST), t3),
            pl.BlockSpec((1, TILE_CH, 2 * TILE_ST), t3),
            pl.BlockSpec((1, 2 * TILE_ST, TILE_CH), t3),
            pl.BlockSpec((D_KV, wv_rows), lambda t: (0, t)),
        ],
        out_shape=[
            jax.ShapeDtypeStruct((nt, BATCH, TILE_ST), F32),
            jax.ShapeDtypeStruct((nt, BATCH, TILE_ST), F32),
            jax.ShapeDtypeStruct((nt, TILE_CH, 2 * TILE_ST), BF16),
            jax.ShapeDtypeStruct((nt, 2 * TILE_ST, TILE_CH), BF16),
            jax.ShapeDtypeStruct((D_KV, D_MODEL), BF16),
        ],
        compiler_params=pltpu.CompilerParams(dimension_semantics=("parallel",)),
        name="ssm_prep",
    )(lanes3(a_re), lanes3(a_im), lanes3(ldt),
      b_re.reshape(nt, TILE_ST, SSM_GROUP), b_im.reshape(nt, TILE_ST, SSM_GROUP),
      c_re.reshape(nt, TILE_CH, SSM_STATE), c_im.reshape(nt, TILE_CH, SSM_STATE), ep, en, w_in)


def _inproj_kernel(x_ref, pos_ref, g_ref, invf_ref, sgn_ref, w_ref, wvt_ref, wo_ref, wglu_ref,
                   q_ref, k_ref, vt_ref, u_ref, wo_bf_ref, wglu_bf_ref):
    tm = x_ref.shape[0]
    wo_bf_ref[...] = wo_ref[...].astype(BF16)
    wglu_bf_ref[...] = wglu_ref[...].astype(BF16)

    lane = lax.broadcasted_iota(jnp.int32, (ROW_SUB, LANES), 1)
    first_half = (lane & (HEAD_DIM // 2)) == 0
    first_head = lane < HEAD_DIM
    scale = 1.0 / math.sqrt(HEAD_DIM)
    for r in range(tm // ROW_SUB):
        rs = slice(r * ROW_SUB, (r + 1) * ROW_SUB)
        xn = _rms(x_ref[rs, :], g_ref[...]).astype(BF16)
        ang = pos_ref[rs, :].astype(F32) * invf_ref[...]
        cos = jnp.cos(ang)
        sin = jnp.sin(ang) * sgn_ref[...]

        def rotary(t):
            partner = jnp.where(first_half,
                                pltpu.roll(t, LANES - HEAD_DIM // 2, 1),
                                pltpu.roll(t, HEAD_DIM // 2, 1))
            return t * cos + partner * sin

        def store_dup(ref, c, t):
            swapped = pltpu.roll(t, HEAD_DIM, 1)
            ref[rs, (2 * c) * LANES:(2 * c + 1) * LANES] = jnp.where(first_head, t, swapped).astype(BF16)
            ref[rs, (2 * c + 1) * LANES:(2 * c + 2) * LANES] = jnp.where(first_head, swapped, t).astype(BF16)

        for j in range(D_ATTN // 256):
            p = jnp.dot(xn, w_ref[:, j * 256:(j + 1) * 256], preferred_element_type=F32)
            for c in range(2):
                col = j * 256 + c * LANES
                q_ref[rs, col:col + LANES] = (rotary(p[:, c * LANES:(c + 1) * LANES]) * scale).astype(BF16)
        p = jnp.dot(xn, w_ref[:, D_ATTN:D_ATTN + D_KV], preferred_element_type=F32)
        for c in range(2):
            store_dup(k_ref, c, rotary(p[:, c * LANES:(c + 1) * LANES]))
        vt_ref[:, rs] = lax.dot_general(wvt_ref[...], xn, NT_DIMS,
                                        preferred_element_type=F32).astype(BF16)
        for j in range(D_SSM // 256):
            col = D_ATTN + 2 * D_KV + j * 256
            u_ref[rs, j * 256:(j + 1) * 256] = jnp.dot(xn, w_ref[:, col:col + 256],
                                                        preferred_element_type=F32)


def _in_proj(x2, pos2, g_pre_mix, w_in_bf, wvt_bf, w_o, w_glu, tm):
    T = x2.shape[0]
    nt = SEQ // tm
    assert BATCH * nt == N_CAST_STEPS
    wo_slab = (D_MODEL // N_CAST_STEPS, D_MODEL)
    wglu_slab = (D_SSM // N_CAST_STEPS, D_SSM)
    half = HEAD_DIM // 2
    invf = ROPE_THETA ** (-np.arange(half, dtype=np.float32) / half)
    invf = np.tile(invf.astype(np.float32), LANES // half)[None, :]
    sgn = np.tile(np.concatenate([-np.ones(half, np.float32), np.ones(half, np.float32)]),
                  LANES // HEAD_DIM)[None, :]
    row = lambda b, i: (b * nt + i, 0)
    const = lambda b, i: (0, 0)
    return pl.pallas_call(
        _inproj_kernel,
        grid=(BATCH, nt),
        in_specs=[
            pl.BlockSpec((tm, D_MODEL), row),
            pl.BlockSpec((tm, 1), row),
            pl.BlockSpec((1, D_MODEL), const),
            pl.BlockSpec((1, LANES), const),
            pl.BlockSpec((1, LANES), const),
            pl.BlockSpec((D_MODEL, D_IN), const),
            pl.BlockSpec((D_KV, D_MODEL), const),
            pl.BlockSpec(wo_slab, row),
            pl.BlockSpec(wglu_slab, row),
        ],
        out_specs=[
            pl.BlockSpec((tm, D_ATTN), row),
            pl.BlockSpec((tm, D_KV_DUP), row),
            pl.BlockSpec((D_KV, tm), lambda b, i: (0, b * nt + i)),
            pl.BlockSpec((tm, D_SSM), lambda b, i: (i, b)),
            pl.BlockSpec(wo_slab, row),
            pl.BlockSpec(wglu_slab, row),
        ],
        out_shape=[
            jax.ShapeDtypeStruct((T, D_ATTN), BF16),
            jax.ShapeDtypeStruct((T, D_KV_DUP), BF16),
            jax.ShapeDtypeStruct((D_KV, T), BF16),
            jax.ShapeDtypeStruct((SEQ, BATCH * D_SSM), F32),
            jax.ShapeDtypeStruct((D_MODEL, D_MODEL), BF16),
            jax.ShapeDtypeStruct((D_SSM, D_SSM), BF16),
        ],
        compiler_params=pltpu.CompilerParams(
            dimension_semantics=("parallel", "parallel"), vmem_limit_bytes=52 * 2**20),
        name="in_proj",
    )(x2, pos2, g_pre_mix, jnp.asarray(invf), jnp.asarray(sgn), w_in_bf, wvt_bf, w_o, w_glu)


def _attn_kernel(sinks_ref, q_ref, k_ref, vt_ref, kh_ref, vth_ref, g_ref, o_ref, acc_ref):
    tq = q_ref.shape[0]
    first_tile = pl.program_id(1) == 0
    two = 2 * WINDOW
    key = lax.broadcasted_iota(jnp.int32, (WINDOW, two), 0)
    qry = lax.broadcasted_iota(jnp.int32, (WINDOW, two), 1) % WINDOW
    use_prev = key > qry
    pair0 = lax.broadcasted_iota(jnp.int32, (1, two), 1) < WINDOW
    low_lanes = lax.broadcasted_iota(jnp.int32, (two, LANES), 1) < HEAD_DIM
    zero = jnp.zeros((), BF16)
    no_values = jnp.zeros((HEAD_DIM, two), BF16)

    def keys_of(j):
        lo, hi = j * WINDOW, (j + 1) * WINDOW
        if j == 0:
            return (jnp.concatenate([kh_ref[...], k_ref[lo:hi, :]], axis=0),
                    jnp.concatenate([vth_ref[...], vt_ref[:, lo:hi]], axis=1))
        return k_ref[lo - WINDOW:hi, :], vt_ref[:, lo - WINDOW:hi]

    def scores(j, kv, side):
        lo, hi = j * WINDOW, (j + 1) * WINDOW
        c0 = kv * Q_PER_KV * HEAD_DIM
        kd = keys_of(j)[0][:, kv * LANES:(kv + 1) * LANES]
        ks = jnp.where(low_lanes if side == 0 else jnp.logical_not(low_lanes), kd, zero)
        q2 = jnp.concatenate([q_ref[lo:hi, c0:c0 + LANES], q_ref[lo:hi, c0 + LANES:c0 + 2 * LANES]],
                             axis=0)
        return lax.dot_general(ks, q2, NT_DIMS, preferred_element_type=F32)

    def probs(j, kv, side, st):
        s_prev = st[:WINDOW, :]
        if j == 0:
            s_prev = jnp.where(first_tile, -jnp.inf, s_prev)
        sf = jnp.where(use_prev, s_prev, st[WINDOW:, :])
        sink = jnp.where(pair0, sinks_ref[kv * Q_PER_KV + side], sinks_ref[kv * Q_PER_KV + 2 + side])
        m = jnp.maximum(jnp.max(sf, axis=0, keepdims=True), sink)
        p = jnp.exp(sf - m)
        den = jnp.sum(p, axis=0, keepdims=True) + jnp.exp(sink - m)
        pn = p * (1.0 / den)
        return jnp.concatenate([jnp.where(use_prev, pn, 0.0), jnp.where(use_prev, 0.0, pn)],
                               axis=0).astype(BF16)

    def weighted_values(j, kv, side, pcat):
        vdt = keys_of(j)[1][kv * HEAD_DIM:(kv + 1) * HEAD_DIM, :]
        vst = jnp.concatenate([vdt, no_values] if side == 0 else [no_values, vdt], axis=0)
        return jnp.dot(vst, pcat, preferred_element_type=F32)

    tasks = [(j, kv, side) for j in range(tq // WINDOW) for kv in range(N_KV_HEADS) for side in range(2)]
    pending = {}
    out = None
    for n in range(len(tasks) + SCORE_AHEAD):
        if n < len(tasks):
            pending[n] = scores(*tasks[n])
        if n < SCORE_AHEAD:
            continue
        j, kv, side = tasks[n - SCORE_AHEAD]
        o = weighted_values(j, kv, side, probs(j, kv, side, pending.pop(n - SCORE_AHEAD)))
        if side == 0:
            out = o
            continue
        out = out + o
        c0 = kv * Q_PER_KV * HEAD_DIM
        acc_ref[c0:c0 + LANES, :] = out[:, :WINDOW]
        acc_ref[c0 + LANES:c0 + 2 * LANES, :] = out[:, WINDOW:]
        if kv == N_KV_HEADS - 1:
            a = acc_ref[...]
            ms = jnp.mean(a * a, axis=0, keepdims=True)
            o_ref[j * WINDOW:(j + 1) * WINDOW, :] = (
                (a * lax.rsqrt(ms + RMS_EPS)).T * g_ref[...]).astype(BF16)


def _attention(q, k, vt, sinks, g_attn_out, tq, batches):
    nt = SEQ // tq
    per = tq // WINDOW
    b0, b1 = batches
    row = lambda b, i: ((b0 + b) * nt + i, 0)
    col = lambda b, i: (0, (b0 + b) * nt + i)
    halo = lambda b, i: jnp.maximum((b0 + b) * (SEQ // WINDOW) + i * per - 1, 0)
    return pl.pallas_call(
        _attn_kernel,
        grid=(b1 - b0, nt),
        in_specs=[
            pl.BlockSpec(memory_space=pltpu.SMEM),
            pl.BlockSpec((tq, D_ATTN), row),
            pl.BlockSpec((tq, D_KV_DUP), row),
            pl.BlockSpec((D_KV, tq), col),
            pl.BlockSpec((WINDOW, D_KV_DUP), lambda b, i: (halo(b, i), 0)),
            pl.BlockSpec((D_KV, WINDOW), lambda b, i: (0, halo(b, i))),
            pl.BlockSpec((1, D_ATTN), lambda b, i: (0, 0)),
        ],
        out_specs=pl.BlockSpec((tq, D_ATTN), lambda b, i: (b * nt + i, 0)),
        out_shape=jax.ShapeDtypeStruct(((b1 - b0) * SEQ, D_ATTN), BF16),
        scratch_shapes=[pltpu.VMEM((D_ATTN, WINDOW), F32)],
        compiler_params=pltpu.CompilerParams(dimension_semantics=("parallel", "parallel")),
        name="attention",
    )(sinks, q, k, vt, k, vt, g_attn_out)


def _ssm_kernel(u_ref, bblk_ref, cblk_ref, lre_ref, lim_ref, dskip_ref, wglu_ref, bglu_ref, g_ref,
                wg_ref, wu_ref, wd_ref, after_ref, o_ref, wgu_bf_ref, wd_bf_ref, s_ref, y_ref, carry_ref):
    del after_ref
    tl = u_ref.shape[0]
    rows = tl * BATCH
    for c in range(D_FF // MXU_COLS):
        src = slice(c * MXU_COLS, (c + 1) * MXU_COLS)
        wgu_bf_ref[:, 2 * c * MXU_COLS:(2 * c + 1) * MXU_COLS] = wg_ref[:, src].astype(BF16)
        wgu_bf_ref[:, (2 * c + 1) * MXU_COLS:(2 * c + 2) * MXU_COLS] = wu_ref[:, src].astype(BF16)
    wd_bf_ref[...] = wd_ref[...].astype(BF16)

    @pl.when(pl.program_id(0) == 0)
    def _():
        carry_ref[...] = jnp.zeros_like(carry_ref)

    u = u_ref[...].reshape(rows, D_SSM)
    ub = u.astype(BF16)

    def project_in(gt):
        ch = slice(gt * TILE_CH, (gt + 1) * TILE_CH)
        s_ref[gt % 2] = jnp.dot(ub[:, ch], bblk_ref[gt], preferred_element_type=F32)

    project_in(0)
    for gt in range(N_GROUP_TILES):
        buf = gt % 2
        ch = slice(gt * TILE_CH, (gt + 1) * TILE_CH)
        if gt + 1 < N_GROUP_TILES:
            project_in(gt + 1)
        for c in range(N_CHUNKS):
            lanes = slice(c * LANES, (c + 1) * LANES)
            re_cols, im_cols = _re_cols(c), _im_cols(c)
            lr = lre_ref[gt, :, lanes]
            li = lim_ref[gt, :, lanes]
            sr = carry_ref[gt, :, re_cols]
            si = carry_ref[gt, :, im_cols]
            for t in range(tl):
                r = slice(t * BATCH, (t + 1) * BATCH)
                nr = lr * sr - li * si + s_ref[buf, r, re_cols]
                ni = lr * si + li * sr + s_ref[buf, r, im_cols]
                s_ref[buf, r, re_cols] = nr
                s_ref[buf, r, im_cols] = ni
                sr, si = nr, ni
            carry_ref[gt, :, re_cols] = sr
            carry_ref[gt, :, im_cols] = si
        y = jnp.dot(s_ref[buf].astype(BF16), cblk_ref[gt], preferred_element_type=F32)
        y_ref[:, ch] = y + dskip_ref[:, ch] * u[:, ch]
    z = jax.nn.gelu(y_ref[...])
    gate = jax.nn.sigmoid(jnp.dot(z.astype(BF16), wglu_ref[...], preferred_element_type=F32)
                          + bglu_ref[...])
    o_ref[...] = _rms(z * gate, g_ref[...]).reshape(tl, BATCH, D_SSM)


def _ssm(u3, bblk, cblk, lam_re, lam_im, d_skip, w_glu_bf, b_glu, g_ssm_out, w_gate, w_up, w_down,
         run_after, tl):
    rows = tl * BATCH
    n_tiles = SEQ // tl
    assert n_tiles == N_CAST_STEPS
    c2 = lambda i: (0, 0)
    c3 = lambda i: (0, 0, 0)
    cur = lambda i: i
    slab = lambda i: (cur(i), 0)
    up_slab = (D_MODEL // N_CAST_STEPS, D_FF)
    down_slab = (D_FF // N_CAST_STEPS, D_MODEL)
    return pl.pallas_call(
        _ssm_kernel,
        grid=(n_tiles,),
        in_specs=[
            pl.BlockSpec((tl, BATCH, D_SSM), lambda i: (cur(i), 0, 0)),
            pl.BlockSpec((N_GROUP_TILES, TILE_CH, 2 * TILE_ST), c3),
            pl.BlockSpec((N_GROUP_TILES, 2 * TILE_ST, TILE_CH), c3),
            pl.BlockSpec((N_GROUP_TILES, BATCH, TILE_ST), c3),
            pl.BlockSpec((N_GROUP_TILES, BATCH, TILE_ST), c3),
            pl.BlockSpec((1, D_SSM), c2),
            pl.BlockSpec((D_SSM, D_SSM), c2),
            pl.BlockSpec((1, D_SSM), c2),
            pl.BlockSpec((1, D_SSM), c2),
            pl.BlockSpec(up_slab, slab),
            pl.BlockSpec(up_slab, slab),
            pl.BlockSpec(down_slab, slab),
            pl.BlockSpec(memory_space=pl.ANY),
        ],
        out_specs=[
            pl.BlockSpec((tl, BATCH, D_SSM), lambda i: (i, 0, 0)),
            pl.BlockSpec((up_slab[0], 2 * D_FF), slab),
            pl.BlockSpec(down_slab, slab),
        ],
        out_shape=[
            jax.ShapeDtypeStruct((SEQ, BATCH, D_SSM), F32),
            jax.ShapeDtypeStruct((D_MODEL, 2 * D_FF), BF16),
            jax.ShapeDtypeStruct((D_FF, D_MODEL), BF16),
        ],
        scratch_shapes=[
            pltpu.VMEM((2, rows, 2 * TILE_ST), F32),
            pltpu.VMEM((rows, D_SSM), F32),
            pltpu.VMEM((N_GROUP_TILES, BATCH, 2 * TILE_ST), F32),
        ],
        compiler_params=pltpu.CompilerParams(
            dimension_semantics=("arbitrary",), vmem_limit_bytes=52 * 2**20),
        name="ssm",
    )(u3, bblk, cblk, lam_re, lam_im, d_skip, w_glu_bf, b_glu, g_ssm_out, w_gate, w_up, w_down, run_after)


def _outproj_kernel(attn_lo_ref, attn_hi_ref, ssm_ref, x_ref, wo_ref, gpost_ref, gpre_ref, h_ref, hn_ref):
    tm = x_ref.shape[0]
    low_batches = pl.program_id(0) < BATCH // 2
    for r in range(tm // ROW_SUB):
        rs = slice(r * ROW_SUB, (r + 1) * ROW_SUB)
        attn = jnp.where(low_batches, attn_lo_ref[rs, :], attn_hi_ref[rs, :])
        mix = jnp.dot(attn, wo_ref[:D_ATTN, :], preferred_element_type=F32)
        mix = mix + jnp.dot(ssm_ref[rs, :].astype(BF16), wo_ref[D_ATTN:, :],
                            preferred_element_type=F32)
        h = x_ref[rs, :] + _rms(mix, gpost_ref[...])
        h_ref[rs, :] = h
        hn_ref[rs, :] = _rms(h, gpre_ref[...]).astype(BF16)


def _out_proj(attn_lo, attn_hi, ssm2, x2, w_o_bf, g_post_mix, g_pre_ffn, tm):
    T = x2.shape[0]
    nt = SEQ // tm
    half = BATCH // 2
    row = lambda b, i: (b * nt + i, 0)
    const = lambda b, i: (0, 0)
    lo_row = lambda b, i: (jnp.where(b < half, b * nt + i, half * nt - 1), 0)
    hi_row = lambda b, i: (jnp.where(b < half, 0, (b - half) * nt + i), 0)
    return pl.pallas_call(
        _outproj_kernel,
        grid=(BATCH, nt),
        in_specs=[
            pl.BlockSpec((tm, D_ATTN), lo_row),
            pl.BlockSpec((tm, D_ATTN), hi_row),
            pl.BlockSpec((tm, D_SSM), lambda b, i: (i, b)),
            pl.BlockSpec((tm, D_MODEL), row),
            pl.BlockSpec((D_MODEL, D_MODEL), const),
            pl.BlockSpec((1, D_MODEL), const),
            pl.BlockSpec((1, D_MODEL), const),
        ],
        out_specs=[pl.BlockSpec((tm, D_MODEL), row), pl.BlockSpec((tm, D_MODEL), row)],
        out_shape=[jax.ShapeDtypeStruct((T, D_MODEL), F32), jax.ShapeDtypeStruct((T, D_MODEL), BF16)],
        compiler_params=pltpu.CompilerParams(
            dimension_semantics=("parallel", "parallel"), vmem_limit_bytes=52 * 2**20),
        name="out_proj",
    )(attn_lo, attn_hi, ssm2, x2, w_o_bf, g_post_mix, g_pre_ffn)


def _ffn_kernel(hn_ref, h_ref, wgu_ref, wd_ref, g_ref, o_ref, acc_ref, *, nf):
    j = pl.program_id(1)
    te = o_ref.shape[0]

    def reduce_step(first):
        n_sub = hn_ref.shape[0] // ROW_SUB
        rows = lambda r: slice(r * ROW_SUB, (r + 1) * ROW_SUB)

        def gate_up(r):
            return jnp.dot(hn_ref[rows(r), :], wgu_ref[...], preferred_element_type=F32)

        ahead = gate_up(0)
        for r in range(n_sub):
            gu = ahead
            if r + 1 < n_sub:
                ahead = gate_up(r + 1)
            hid = jnp.concatenate(
                [jax.nn.silu(gu[:, 2 * c * MXU_COLS:(2 * c + 1) * MXU_COLS])
                 * gu[:, (2 * c + 1) * MXU_COLS:(2 * c + 2) * MXU_COLS]
                 for c in range(gu.shape[1] // (2 * MXU_COLS))], axis=1).astype(BF16)
            part = jnp.dot(hid, wd_ref[...], preferred_element_type=F32)
            if first:
                acc_ref[rows(r), :] = part
            else:
                acc_ref[rows(r), :] += part

    @pl.when(j == 0)
    def _():
        reduce_step(True)

    @pl.when(jnp.logical_and(j > 0, j < nf))
    def _():
        reduce_step(False)

    @pl.when(j >= nf)
    def _():
        base = (j - nf) * te
        g = g_ref[...]

        def norm_rows(c, carry):
            r = pl.multiple_of(c * NORM_ROWS, NORM_ROWS)
            a = acc_ref[pl.ds(pl.multiple_of(base + r, NORM_ROWS), NORM_ROWS), :]
            o_ref[pl.ds(r, NORM_ROWS), :] = h_ref[pl.ds(r, NORM_ROWS), :] + _rms(a, g)
            return carry

        lax.fori_loop(0, te // NORM_ROWS, norm_rows, 0, unroll=8)


def _ffn(hn, h, wgu_bf, wd_bf, g_post_ffn, tm, tf, te):
    T = h.shape[0]
    nf = D_FF // tf
    sub = tm // te
    n_tiles = T // tm
    chunk = lambda j: jnp.where(j < nf, j, 0)
    hn_row = lambda i, j: (jnp.minimum(i + (j >= nf).astype(jnp.int32), n_tiles - 1), 0)
    out_row = lambda i, j: (i * sub + jnp.clip(j - nf, 0, sub - 1), 0)
    h_row = lambda i, j: (jnp.where(j == 0, jnp.maximum(i * sub - 1, 0),
                                    i * sub + jnp.clip(j - nf, 0, sub - 1)), 0)
    return pl.pallas_call(
        functools.partial(_ffn_kernel, nf=nf),
        grid=(n_tiles, nf + sub),
        in_specs=[
            pl.BlockSpec((tm, D_MODEL), hn_row),
            pl.BlockSpec((te, D_MODEL), h_row),
            pl.BlockSpec((D_MODEL, 2 * tf), lambda i, j: (0, chunk(j))),
            pl.BlockSpec((tf, D_MODEL), lambda i, j: (chunk(j), 0)),
            pl.BlockSpec((1, D_MODEL), lambda i, j: (0, 0)),
        ],
        out_specs=pl.BlockSpec((te, D_MODEL), out_row),
        out_shape=jax.ShapeDtypeStruct((T, D_MODEL), F32),
        scratch_shapes=[pltpu.VMEM((tm, D_MODEL), F32)],
        compiler_params=pltpu.CompilerParams(
            dimension_semantics=("parallel", "arbitrary"), vmem_limit_bytes=58 * 2**20),
        name="ffn",
    )(hn, h, wgu_bf, wd_bf, g_post_ffn)


def kernel(x, positions, g_pre_mix, w_in, sinks, a_re, a_im, log_dt, b_re, b_im, c_re, c_im, d_skip,
           w_glu, b_glu, g_attn_out, g_ssm_out, w_o, g_post_mix, g_pre_ffn, w_gate, w_up, w_down,
           g_post_ffn):
    depth = w_in.shape[0]
    T = BATCH * SEQ
    h = x.reshape(T, D_MODEL)
    pos2 = positions.reshape(T, 1)
    for i in range(depth):
        lam_re, lam_im, bblk, cblk, wvt_bf = _ssm_prep(a_re[i], a_im[i], log_dt[i], b_re[i], b_im[i],
                                                       c_re[i], c_im[i], w_in[i])
        w_in_bf = w_in[i].astype(BF16)
        q, k, vt, u, w_o_bf, w_glu_bf = _in_proj(h, pos2, g_pre_mix[i][None, :], w_in_bf, wvt_bf,
                                                 w_o[i], w_glu[i], tm=512)
        half = BATCH // 2
        attn_lo = _attention(q, k, vt, sinks[i], g_attn_out[i][None, :], tq=1024, batches=(0, half))
        ssm_n, wgu_bf, wd_bf = _ssm(u.reshape(SEQ, BATCH, D_SSM), bblk, cblk, lam_re, lam_im,
                                    d_skip[i].reshape(1, D_SSM), w_glu_bf, b_glu[i][None, :],
                                    g_ssm_out[i][None, :], w_gate[i], w_up[i], w_down[i],
                                    run_after=attn_lo, tl=64)
        attn_hi = _attention(q, k, vt, sinks[i], g_attn_out[i][None, :], tq=1024, batches=(half, BATCH))
        h, hn = _out_proj(attn_lo, attn_hi, ssm_n.reshape(SEQ, BATCH * D_SSM), h, w_o_bf,
                          g_post_mix[i][None, :], g_pre_ffn[i][None, :], tm=512)
        h = _ffn(hn, h, wgu_bf, wd_bf, g_post_ffn[i][None, :], tm=1024, tf=512, te=512)
    return h.reshape(BATCH, SEQ, D_MODEL)
```

```python
import functools
import math

import jax
import jax.numpy as jnp
import numpy as np
from jax import lax
from jax.experimental import pallas as pl
from jax.experimental.pallas import tpu as pltpu

D_MODEL = 2048
BATCH = 8
SEQ = 2048
HEAD_DIM = 64
D_ATTN = 1024
N_Q_HEADS = 16
N_KV_HEADS = 4
Q_PER_KV = 4
D_KV = 256
WINDOW = 128
ROPE_THETA = 10000.0
D_SSM = 1024
SSM_GROUP = 16
N_SSM_GROUPS = 64
SSM_STATE = 64
D_IN = D_ATTN + 2 * D_KV + D_SSM
D_FF = 5632
RMS_EPS = 1e-6

LANES = 128
SUBLANES = 8
GROUPS_PER_TILE = 16
N_GROUP_TILES = N_SSM_GROUPS // GROUPS_PER_TILE
TILE_CH = GROUPS_PER_TILE * SSM_GROUP
TILE_ST = GROUPS_PER_TILE * SSM_STATE
N_CHUNKS = TILE_ST // LANES
D_KV_DUP = 2 * N_KV_HEADS * HEAD_DIM
ROW_SUB = 256
NORM_ROWS = 16
FF_CHUNK = 512
MXU_COLS = 256
N_CAST_STEPS = 32
SCORE_AHEAD = 8

F32 = jnp.float32
BF16 = jnp.bfloat16
NT_DIMS = (((1,), (1,)), ((), ()))


def _re_cols(c):
    return slice(2 * c * LANES, (2 * c + 1) * LANES)


def _im_cols(c):
    return slice((2 * c + 1) * LANES, (2 * c + 2) * LANES)


def _rms(x, g):
    ms = jnp.mean(x * x, axis=-1, keepdims=True)
    return x * lax.rsqrt(ms + RMS_EPS) * g


def _split3(x):
    x1 = x.astype(BF16)
    r1 = x - x1.astype(F32)
    x2 = r1.astype(BF16)
    x3 = (r1 - x2.astype(F32)).astype(BF16)
    return x1, x2, x3


def _ssm_prep_kernel(ar_ref, ai_ref, ldt_ref, br_ref, bi_ref, cr_ref, ci_ref, ep_ref, en_ref, wv_ref,
                     lre_ref, lim_ref, bblk_ref, cblk_ref, wvt_ref):
    wvt_ref[...] = wv_ref[...].T.astype(BF16)

    ar = ar_ref[0]
    ai = ai_ref[0]
    dt = jnp.exp(ldt_ref[0])
    mag = jnp.exp(ar * dt)
    lam_re = mag * jnp.cos(ai * dt)
    lam_im = mag * jnp.sin(ai * dt)
    den = ar * ar + ai * ai
    nr = lam_re - 1.0
    ni = lam_im
    f_re = (nr * ar + ni * ai) / den
    f_im = (ni * ar - nr * ai) / den
    lre_ref[0] = jnp.broadcast_to(lam_re, (BATCH, TILE_ST))
    lim_ref[0] = jnp.broadcast_to(lam_im, (BATCH, TILE_ST))

    ep = ep_ref[...]

    def spread_b(b):
        return sum(lax.dot_general(ep, piece, NT_DIMS, preferred_element_type=F32)
                   for piece in _split3(b))

    pb_re = spread_b(br_ref[0])
    pb_im = spread_b(bi_ref[0])
    row = lax.broadcasted_iota(jnp.int32, (TILE_CH, TILE_ST), 0)
    col = lax.broadcasted_iota(jnp.int32, (TILE_CH, TILE_ST), 1)
    diag = (row // SSM_GROUP) == (col // SSM_STATE)
    bb_re = jnp.where(diag, f_re * pb_re - f_im * pb_im, 0.0).astype(BF16)
    bb_im = jnp.where(diag, f_re * pb_im + f_im * pb_re, 0.0).astype(BF16)
    for c in range(N_CHUNKS):
        bblk_ref[0, :, _re_cols(c)] = bb_re[:, c * LANES:(c + 1) * LANES]
        bblk_ref[0, :, _im_cols(c)] = bb_im[:, c * LANES:(c + 1) * LANES]

    en = en_ref[...]
    row = lax.broadcasted_iota(jnp.int32, (TILE_ST, TILE_CH), 0)
    col = lax.broadcasted_iota(jnp.int32, (TILE_ST, TILE_CH), 1)
    diag = (row // SSM_STATE) == (col // SSM_GROUP)
    pc_re = lax.dot_general(en, cr_ref[0].astype(BF16), NT_DIMS, preferred_element_type=F32)
    pc_im = lax.dot_general(en, ci_ref[0].astype(BF16), NT_DIMS, preferred_element_type=F32)
    cc_re = jnp.where(diag, pc_re, 0.0).astype(BF16)
    cc_im = jnp.where(diag, -pc_im, 0.0).astype(BF16)
    for c in range(N_CHUNKS):
        cblk_ref[0, _re_cols(c), :] = cc_re[c * LANES:(c + 1) * LANES, :]
        cblk_ref[0, _im_cols(c), :] = cc_im[c * LANES:(c + 1) * LANES, :]


def _ssm_prep(a_re, a_im, log_dt, b_re, b_im, c_re, c_im, w_in):
    nt = N_GROUP_TILES
    wv_rows = D_MODEL // nt
    wv_col_block = (D_ATTN + D_KV) // D_KV
    lanes3 = lambda a: a.reshape(nt, 1, TILE_ST)
    ldt = jnp.broadcast_to(log_dt[:, None], (N_SSM_GROUPS, SSM_STATE))
    ep = jnp.asarray(np.tile(np.eye(SSM_GROUP, dtype=np.float32), (GROUPS_PER_TILE, 1)), BF16)
    en = jnp.asarray(np.tile(np.eye(SSM_STATE, dtype=np.float32), (GROUPS_PER_TILE, 1)), BF16)
    t3 = lambda t: (t, 0, 0)
    c2 = lambda t: (0, 0)
    return pl.pallas_call(
        _ssm_prep_kernel,
        grid=(nt,),
        in_specs=[
            pl.BlockSpec((1, 1, TILE_ST), t3),
            pl.BlockSpec((1, 1, TILE_ST), t3),
            pl.BlockSpec((1, 1, TILE_ST), t3),
            pl.BlockSpec((1, TILE_ST, SSM_GROUP), t3),
            pl.BlockSpec((1, TILE_ST, SSM_GROUP), t3),
            pl.BlockSpec((1, TILE_CH, SSM_STATE), t3),
            pl.BlockSpec((1, TILE_CH, SSM_STATE), t3),
            pl.BlockSpec((TILE_CH, SSM_GROUP), c2),
            pl.BlockSpec((TILE_ST, SSM_STATE), c2),
            pl.BlockSpec((wv_rows, D_KV), lambda t: (t, wv_col_block)),
        ],
        out_specs=[
            pl.BlockSpec((1, BATCH, TILE_ST), t3),
            pl.BlockSpec((1, BATCH, TILE_ST), t3),
            pl.BlockSpec((1, TILE_CH, 2 * TILE_ST), t3),
            pl.BlockSpec((1, 2 * TILE_ST, TILE_CH), t3),
            pl.BlockSpec((D_KV, wv_rows), lambda t: (0, t)),
        ],
        out_shape=[
            jax.ShapeDtypeStruct((nt, BATCH, TILE_ST), F32),
            jax.ShapeDtypeStruct((nt, BATCH, TILE_ST), F32),
            jax.ShapeDtypeStruct((nt, TILE_CH, 2 * TILE_ST), BF16),
            jax.ShapeDtypeStruct((nt, 2 * TILE_ST, TILE_CH), BF16),
            jax.ShapeDtypeStruct((D_KV, D_MODEL), BF16),
        ],
        compiler_params=pltpu.CompilerParams(dimension_semantics=("parallel",)),
        name="ssm_prep",
    )(lanes3(a_re), lanes3(a_im), lanes3(ldt),
      b_re.reshape(nt, TILE_ST, SSM_GROUP), b_im.reshape(nt, TILE_ST, SSM_GROUP),
      c_re.reshape(nt, TILE_CH, SSM_STATE), c_im.reshape(nt, TILE_CH, SSM_STATE), ep, en, w_in)


def _inproj_kernel(x_ref, pos_ref, g_ref, invf_ref, sgn_ref, w_ref, wvt_ref, wo_ref, wglu_ref,
                   q_ref, k_ref, vt_ref, u_ref, wo_bf_ref, wglu_bf_ref):
    tm = x_ref.shape[0]
    wo_bf_ref[...] = wo_ref[...].astype(BF16)
    wglu_bf_ref[...] = wglu_ref[...].astype(BF16)

    lane = lax.broadcasted_iota(jnp.int32, (ROW_SUB, LANES), 1)
    first_half = (lane & (HEAD_DIM // 2)) == 0
    first_head = lane < HEAD_DIM
    scale = 1.0 / math.sqrt(HEAD_DIM)
    for r in range(tm // ROW_SUB):
        rs = slice(r * ROW_SUB, (r + 1) * ROW_SUB)
        xn = _rms(x_ref[rs, :], g_ref[...]).astype(BF16)
        ang = pos_ref[rs, :].astype(F32) * invf_ref[...]
        cos = jnp.cos(ang)
        sin = jnp.sin(ang) * sgn_ref[...]

        def rotary(t):
            partner = jnp.where(first_half,
                                pltpu.roll(t, LANES - HEAD_DIM // 2, 1),
                                pltpu.roll(t, HEAD_DIM // 2, 1))
            return t * cos + partner * sin

        def store_dup(ref, c, t):
            swapped = pltpu.roll(t, HEAD_DIM, 1)
            ref[rs, (2 * c) * LANES:(2 * c + 1) * LANES] = jnp.where(first_head, t, swapped).astype(BF16)
            ref[rs, (2 * c + 1) * LANES:(2 * c + 2) * LANES] = jnp.where(first_head, swapped, t).astype(BF16)

        for j in range(D_ATTN // 256):
            p = jnp.dot(xn, w_ref[:, j * 256:(j + 1) * 256], preferred_element_type=F32)
            for c in range(2):
                col = j * 256 + c * LANES
                q_ref[rs, col:col + LANES] = (rotary(p[:, c * LANES:(c + 1) * LANES]) * scale).astype(BF16)
        p = jnp.dot(xn, w_ref[:, D_ATTN:D_ATTN + D_KV], preferred_element_type=F32)
        for c in range(2):
            store_dup(k_ref, c, rotary(p[:, c * LANES:(c + 1) * LANES]))
        vt_ref[:, rs] = lax.dot_general(wvt_ref[...], xn, NT_DIMS,
                                        preferred_element_type=F32).astype(BF16)
        for j in range(D_SSM // 256):
            col = D_ATTN + 2 * D_KV + j * 256
            u_ref[rs, j * 256:(j + 1) * 256] = jnp.dot(xn, w_ref[:, col:col + 256],
                                                        preferred_element_type=F32)


def _in_proj(x2, pos2, g_pre_mix, w_in_bf, wvt_bf, w_o, w_glu, tm):
    T = x2.shape[0]
    nt = SEQ // tm
    assert BATCH * nt == N_CAST_STEPS
    wo_slab = (D_MODEL // N_CAST_STEPS, D_MODEL)
    wglu_slab = (D_SSM // N_CAST_STEPS, D_SSM)
    half = HEAD_DIM // 2
    invf = ROPE_THETA ** (-np.arange(half, dtype=np.float32) / half)
    invf = np.tile(invf.astype(np.float32), LANES // half)[None, :]
    sgn = np.tile(np.concatenate([-np.ones(half, np.float32), np.ones(half, np.float32)]),
                  LANES // HEAD_DIM)[None, :]
    row = lambda b, i: (b * nt + i, 0)
    const = lambda b, i: (0, 0)
    return pl.pallas_call(
        _inproj_kernel,
        grid=(BATCH, nt),
        in_specs=[
            pl.BlockSpec((tm, D_MODEL), row),
            pl.BlockSpec((tm, 1), row),
            pl.BlockSpec((1, D_MODEL), const),
            pl.BlockSpec((1, LANES), const),
            pl.BlockSpec((1, LANES), const),
            pl.BlockSpec((D_MODEL, D_IN), const),
            pl.BlockSpec((D_KV, D_MODEL), const),
            pl.BlockSpec(wo_slab, row),
            pl.BlockSpec(wglu_slab, row),
        ],
        out_specs=[
            pl.BlockSpec((tm, D_ATTN), row),
            pl.BlockSpec((tm, D_KV_DUP), row),
            pl.BlockSpec((D_KV, tm), lambda b, i: (0, b * nt + i)),
            pl.BlockSpec((tm, D_SSM), lambda b, i: (i, b)),
            pl.BlockSpec(wo_slab, row),
            pl.BlockSpec(wglu_slab, row),
        ],
        out_shape=[
            jax.ShapeDtypeStruct((T, D_ATTN), BF16),
            jax.ShapeDtypeStruct((T, D_KV_DUP), BF16),
            jax.ShapeDtypeStruct((D_KV, T), BF16),
            jax.ShapeDtypeStruct((SEQ, BATCH * D_SSM), F32),
            jax.ShapeDtypeStruct((D_MODEL, D_MODEL), BF16),
            jax.ShapeDtypeStruct((D_SSM, D_SSM), BF16),
        ],
        compiler_params=pltpu.CompilerParams(
            dimension_semantics=("parallel", "parallel"), vmem_limit_bytes=52 * 2**20),
        name="in_proj",
    )(x2, pos2, g_pre_mix, jnp.asarray(invf), jnp.asarray(sgn), w_in_bf, wvt_bf, w_o, w_glu)


def _attn_kernel(sinks_ref, q_ref, k_ref, vt_ref, kh_ref, vth_ref, g_ref, o_ref, acc_ref):
    tq = q_ref.shape[0]
    first_tile = pl.program_id(1) == 0
    two = 2 * WINDOW
    key = lax.broadcasted_iota(jnp.int32, (WINDOW, two), 0)
    qry = lax.broadcasted_iota(jnp.int32, (WINDOW, two), 1) % WINDOW
    use_prev = key > qry
    pair0 = lax.broadcasted_iota(jnp.int32, (1, two), 1) < WINDOW
    low_lanes = lax.broadcasted_iota(jnp.int32, (two, LANES), 1) < HEAD_DIM
    zero = jnp.zeros((), BF16)
    no_values = jnp.zeros((HEAD_DIM, two), BF16)

    def keys_of(j):
        lo, hi = j * WINDOW, (j + 1) * WINDOW
        if j == 0:
            return (jnp.concatenate([kh_ref[...], k_ref[lo:hi, :]], axis=0),
                    jnp.concatenate([vth_ref[...], vt_ref[:, lo:hi]], axis=1))
        return k_ref[lo - WINDOW:hi, :], vt_ref[:, lo - WINDOW:hi]

    def scores(j, kv, side):
        lo, hi = j * WINDOW, (j + 1) * WINDOW
        c0 = kv * Q_PER_KV * HEAD_DIM
        kd = keys_of(j)[0][:, kv * LANES:(kv + 1) * LANES]
        ks = jnp.where(low_lanes if side == 0 else jnp.logical_not(low_lanes), kd, zero)
        q2 = jnp.concatenate([q_ref[lo:hi, c0:c0 + LANES], q_ref[lo:hi, c0 + LANES:c0 + 2 * LANES]],
                             axis=0)
        return lax.dot_general(ks, q2, NT_DIMS, preferred_element_type=F32)

    def probs(j, kv, side, st):
        s_prev = st[:WINDOW, :]
        if j == 0:
            s_prev = jnp.where(first_tile, -jnp.inf, s_prev)
        sf = jnp.where(use_prev, s_prev, st[WINDOW:, :])
        sink = jnp.where(pair0, sinks_ref[kv * Q_PER_KV + side], sinks_ref[kv * Q_PER_KV + 2 + side])
        m = jnp.maximum(jnp.max(sf, axis=0, keepdims=True), sink)
        p = jnp.exp(sf - m)
        den = jnp.sum(p, axis=0, keepdims=True) + jnp.exp(sink - m)
        pn = p * (1.0 / den)
        return jnp.concatenate([jnp.where(use_prev, pn, 0.0), jnp.where(use_prev, 0.0, pn)],
                               axis=0).astype(BF16)

    def weighted_values(j, kv, side, pcat):
        vdt = keys_of(j)[1][kv * HEAD_DIM:(kv + 1) * HEAD_DIM, :]
        vst = jnp.concatenate([vdt, no_values] if side == 0 else [no_values, vdt], axis=0)
        return jnp.dot(vst, pcat, preferred_element_type=F32)

    tasks = [(j, kv, side) for j in range(tq // WINDOW) for kv in range(N_KV_HEADS) for side in range(2)]
    pending = {}
    out = None
    for n in range(len(tasks) + SCORE_AHEAD):
        if n < len(tasks):
            pending[n] = scores(*tasks[n])
        if n < SCORE_AHEAD:
            continue
        j, kv, side = tasks[n - SCORE_AHEAD]
        o = weighted_values(j, kv, side, probs(j, kv, side, pending.pop(n - SCORE_AHEAD)))
        if side == 0:
            out = o
            continue
        out = out + o
        c0 = kv * Q_PER_KV * HEAD_DIM
        acc_ref[c0:c0 + LANES, :] = out[:, :WINDOW]
        acc_ref[c0 + LANES:c0 + 2 * LANES, :] = out[:, WINDOW:]
        if kv == N_KV_HEADS - 1:
            a = acc_ref[...]
            ms = jnp.mean(a * a, axis=0, keepdims=True)
            o_ref[j * WINDOW:(j + 1) * WINDOW, :] = (
                (a * lax.rsqrt(ms + RMS_EPS)).T * g_ref[...]).astype(BF16)


def _attention(q, k, vt, sinks, g_attn_out, tq, batches):
    nt = SEQ // tq
    per = tq // WINDOW
    b0, b1 = batches
    row = lambda b, i: ((b0 + b) * nt + i, 0)
    col = lambda b, i: (0, (b0 + b) * nt + i)
    halo = lambda b, i: jnp.maximum((b0 + b) * (SEQ // WINDOW) + i * per - 1, 0)
    return pl.pallas_call(
        _attn_kernel,
        grid=(b1 - b0, nt),
        in_specs=[
            pl.BlockSpec(memory_space=pltpu.SMEM),
            pl.BlockSpec((tq, D_ATTN), row),
            pl.BlockSpec((tq, D_KV_DUP), row),
            pl.BlockSpec((D_KV, tq), col),
            pl.BlockSpec((WINDOW, D_KV_DUP), lambda b, i: (halo(b, i), 0)),
            pl.BlockSpec((D_KV, WINDOW), lambda b, i: (0, halo(b, i))),
            pl.BlockSpec((1, D_ATTN), lambda b, i: (0, 0)),
        ],
        out_specs=pl.BlockSpec((tq, D_ATTN), lambda b, i: (b * nt + i, 0)),
        out_shape=jax.ShapeDtypeStruct(((b1 - b0) * SEQ, D_ATTN), BF16),
        scratch_shapes=[pltpu.VMEM((D_ATTN, WINDOW), F32)],
        compiler_params=pltpu.CompilerParams(dimension_semantics=("parallel", "parallel")),
        name="attention",
    )(sinks, q, k, vt, k, vt, g_attn_out)


def _ssm_kernel(u_ref, bblk_ref, cblk_ref, lre_ref, lim_ref, dskip_ref, wglu_ref, bglu_ref, g_ref,
                wg_ref, wu_ref, wd_ref, after_ref, o_ref, wgu_bf_ref, wd_bf_ref, s_ref, y_ref, carry_ref):
    del after_ref
    tl = u_ref.shape[0]
    rows = tl * BATCH
    per_chunk = FF_CHUNK // MXU_COLS
    for c in range(D_FF // MXU_COLS):
        src = slice(c * MXU_COLS, (c + 1) * MXU_COLS)
        dst = 2 * (c % per_chunk) * MXU_COLS
        wgu_bf_ref[c // per_chunk, :, dst:dst + MXU_COLS] = wg_ref[:, src].astype(BF16)
        wgu_bf_ref[c // per_chunk, :, dst + MXU_COLS:dst + 2 * MXU_COLS] = wu_ref[:, src].astype(BF16)
    wd_bf_ref[...] = wd_ref[...].astype(BF16)

    @pl.when(pl.program_id(0) == 0)
    def _():
        carry_ref[...] = jnp.zeros_like(carry_ref)

    u = u_ref[...].reshape(rows, D_SSM)
    ub = u.astype(BF16)

    def project_in(gt):
        ch = slice(gt * TILE_CH, (gt + 1) * TILE_CH)
        s_ref[gt % 2] = jnp.dot(ub[:, ch], bblk_ref[gt], preferred_element_type=F32)

    project_in(0)
    for gt in range(N_GROUP_TILES):
        buf = gt % 2
        ch = slice(gt * TILE_CH, (gt + 1) * TILE_CH)
        if gt + 1 < N_GROUP_TILES:
            project_in(gt + 1)
        for c in range(N_CHUNKS):
            lanes = slice(c * LANES, (c + 1) * LANES)
            re_cols, im_cols = _re_cols(c), _im_cols(c)
            lr = lre_ref[gt, :, lanes]
            li = lim_ref[gt, :, lanes]
            sr = carry_ref[gt, :, re_cols]
            si = carry_ref[gt, :, im_cols]
            for t in range(tl):
                r = slice(t * BATCH, (t + 1) * BATCH)
                nr = lr * sr - li * si + s_ref[buf, r, re_cols]
                ni = lr * si + li * sr + s_ref[buf, r, im_cols]
                s_ref[buf, r, re_cols] = nr
                s_ref[buf, r, im_cols] = ni
                sr, si = nr, ni
            carry_ref[gt, :, re_cols] = sr
            carry_ref[gt, :, im_cols] = si
        y = jnp.dot(s_ref[buf].astype(BF16), cblk_ref[gt], preferred_element_type=F32)
        y_ref[:, ch] = y + dskip_ref[:, ch] * u[:, ch]
    z = jax.nn.gelu(y_ref[...])
    gate = jax.nn.sigmoid(jnp.dot(z.astype(BF16), wglu_ref[...], preferred_element_type=F32)
                          + bglu_ref[...])
    o_ref[...] = _rms(z * gate, g_ref[...]).reshape(tl, BATCH, D_SSM)


def _ssm(u3, bblk, cblk, lam_re, lam_im, d_skip, w_glu_bf, b_glu, g_ssm_out, w_gate, w_up, w_down,
         run_after, tl):
    rows = tl * BATCH
    n_tiles = SEQ // tl
    assert n_tiles == N_CAST_STEPS
    c2 = lambda i: (0, 0)
    c3 = lambda i: (0, 0, 0)
    cur = lambda i: i
    slab = lambda i: (cur(i), 0)
    up_slab = (D_MODEL // N_CAST_STEPS, D_FF)
    down_slab = (D_FF // N_CAST_STEPS, D_MODEL)
    return pl.pallas_call(
        _ssm_kernel,
        grid=(n_tiles,),
        in_specs=[
            pl.BlockSpec((tl, BATCH, D_SSM), lambda i: (cur(i), 0, 0)),
            pl.BlockSpec((N_GROUP_TILES, TILE_CH, 2 * TILE_ST), c3),
            pl.BlockSpec((N_GROUP_TILES, 2 * TILE_ST, TILE_CH), c3),
            pl.BlockSpec((N_GROUP_TILES, BATCH, TILE_ST), c3),
            pl.BlockSpec((N_GROUP_TILES, BATCH, TILE_ST), c3),
            pl.BlockSpec((1, D_SSM), c2),
            pl.BlockSpec((D_SSM, D_SSM), c2),
            pl.BlockSpec((1, D_SSM), c2),
            pl.BlockSpec((1, D_SSM), c2),
            pl.BlockSpec(up_slab, slab),
            pl.BlockSpec(up_slab, slab),
            pl.BlockSpec(down_slab, slab),
            pl.BlockSpec(memory_space=pl.ANY),
        ],
        out_specs=[
            pl.BlockSpec((tl, BATCH, D_SSM), lambda i: (i, 0, 0)),
            pl.BlockSpec((D_FF // FF_CHUNK, up_slab[0], 2 * FF_CHUNK), lambda i: (0, cur(i), 0)),
            pl.BlockSpec(down_slab, slab),
        ],
        out_shape=[
            jax.ShapeDtypeStruct((SEQ, BATCH, D_SSM), F32),
            jax.ShapeDtypeStruct((D_FF // FF_CHUNK, D_MODEL, 2 * FF_CHUNK), BF16),
            jax.ShapeDtypeStruct((D_FF, D_MODEL), BF16),
        ],
        scratch_shapes=[
            pltpu.VMEM((2, rows, 2 * TILE_ST), F32),
            pltpu.VMEM((rows, D_SSM), F32),
            pltpu.VMEM((N_GROUP_TILES, BATCH, 2 * TILE_ST), F32),
        ],
        compiler_params=pltpu.CompilerParams(
            dimension_semantics=("arbitrary",), vmem_limit_bytes=52 * 2**20),
        name="ssm",
    )(u3, bblk, cblk, lam_re, lam_im, d_skip, w_glu_bf, b_glu, g_ssm_out, w_gate, w_up, w_down, run_after)


def _outproj_kernel(attn_lo_ref, attn_hi_ref, ssm_ref, x_ref, wo_ref, gpost_ref, gpre_ref, h_ref, hn_ref):
    tm = x_ref.shape[0]
    low_batches = pl.program_id(0) < BATCH // 2
    for r in range(tm // ROW_SUB):
        rs = slice(r * ROW_SUB, (r + 1) * ROW_SUB)
        attn = jnp.where(low_batches, attn_lo_ref[rs, :], attn_hi_ref[rs, :])
        mix = jnp.dot(attn, wo_ref[:D_ATTN, :], preferred_element_type=F32)
        mix = mix + jnp.dot(ssm_ref[rs, :].astype(BF16), wo_ref[D_ATTN:, :],
                            preferred_element_type=F32)
        h = x_ref[rs, :] + _rms(mix, gpost_ref[...])
        h_ref[rs, :] = h
        hn_ref[rs, :] = _rms(h, gpre_ref[...]).astype(BF16)


def _out_proj(attn_lo, attn_hi, ssm2, x2, w_o_bf, g_post_mix, g_pre_ffn, tm):
    T = x2.shape[0]
    nt = SEQ // tm
    half = BATCH // 2
    row = lambda b, i: (b * nt + i, 0)
    const = lambda b, i: (0, 0)
    lo_row = lambda b, i: (jnp.where(b < half, b * nt + i, half * nt - 1), 0)
    hi_row = lambda b, i: (jnp.where(b < half, 0, (b - half) * nt + i), 0)
    return pl.pallas_call(
        _outproj_kernel,
        grid=(BATCH, nt),
        in_specs=[
            pl.BlockSpec((tm, D_ATTN), lo_row),
            pl.BlockSpec((tm, D_ATTN), hi_row),
            pl.BlockSpec((tm, D_SSM), lambda b, i: (i, b)),
            pl.BlockSpec((tm, D_MODEL), row),
            pl.BlockSpec((D_MODEL, D_MODEL), const),
            pl.BlockSpec((1, D_MODEL), const),
            pl.BlockSpec((1, D_MODEL), const),
        ],
        out_specs=[pl.BlockSpec((tm, D_MODEL), row), pl.BlockSpec((tm, D_MODEL), row)],
        out_shape=[jax.ShapeDtypeStruct((T, D_MODEL), F32), jax.ShapeDtypeStruct((T, D_MODEL), BF16)],
        compiler_params=pltpu.CompilerParams(
            dimension_semantics=("parallel", "parallel"), vmem_limit_bytes=52 * 2**20),
        name="out_proj",
    )(attn_lo, attn_hi, ssm2, x2, w_o_bf, g_post_mix, g_pre_ffn)


def _ffn_kernel(hn_ref, h_ref, wgu_ref, wd_ref, g_ref, o_ref, acc_ref, *, nf):
    j = pl.program_id(1)
    te = o_ref.shape[0]

    def reduce_step(first):
        n_sub = hn_ref.shape[0] // ROW_SUB
        rows = lambda r: slice(r * ROW_SUB, (r + 1) * ROW_SUB)

        def gate_up(r):
            return jnp.dot(hn_ref[rows(r), :], wgu_ref[0], preferred_element_type=F32)

        ahead = gate_up(0)
        for r in range(n_sub):
            gu = ahead
            if r + 1 < n_sub:
                ahead = gate_up(r + 1)
            hid = jnp.concatenate(
                [jax.nn.silu(gu[:, 2 * c * MXU_COLS:(2 * c + 1) * MXU_COLS])
                 * gu[:, (2 * c + 1) * MXU_COLS:(2 * c + 2) * MXU_COLS]
                 for c in range(gu.shape[1] // (2 * MXU_COLS))], axis=1).astype(BF16)
            part = jnp.dot(hid, wd_ref[...], preferred_element_type=F32)
            if first:
                acc_ref[rows(r), :] = part
            else:
                acc_ref[rows(r), :] += part

    @pl.when(j == 0)
    def _():
        reduce_step(True)

    @pl.when(jnp.logical_and(j > 0, j < nf))
    def _():
        reduce_step(False)

    @pl.when(j >= nf)
    def _():
        base = (j - nf) * te
        g = g_ref[...]

        def norm_rows(c, carry):
            r = pl.multiple_of(c * NORM_ROWS, NORM_ROWS)
            a = acc_ref[pl.ds(pl.multiple_of(base + r, NORM_ROWS), NORM_ROWS), :]
            o_ref[pl.ds(r, NORM_ROWS), :] = h_ref[pl.ds(r, NORM_ROWS), :] + _rms(a, g)
            return carry

        lax.fori_loop(0, te // NORM_ROWS, norm_rows, 0, unroll=8)


def _ffn(hn, h, wgu_bf, wd_bf, g_post_ffn, tm, te):
    tf = FF_CHUNK
    T = h.shape[0]
    nf = D_FF // tf
    sub = tm // te
    n_tiles = T // tm
    chunk = lambda j: jnp.where(j < nf, j, 0)
    hn_row = lambda i, j: (jnp.minimum(i + (j >= nf).astype(jnp.int32), n_tiles - 1), 0)
    out_row = lambda i, j: (i * sub + jnp.clip(j - nf, 0, sub - 1), 0)
    h_row = lambda i, j: (jnp.where(j == 0, jnp.maximum(i * sub - 1, 0),
                                    i * sub + jnp.clip(j - nf, 0, sub - 1)), 0)
    return pl.pallas_call(
        functools.partial(_ffn_kernel, nf=nf),
        grid=(n_tiles, nf + sub),
        in_specs=[
            pl.BlockSpec((tm, D_MODEL), hn_row),
            pl.BlockSpec((te, D_MODEL), h_row),
            pl.BlockSpec((1, D_MODEL, 2 * tf), lambda i, j: (chunk(j), 0, 0)),
            pl.BlockSpec((tf, D_MODEL), lambda i, j: (chunk(j), 0)),
            pl.BlockSpec((1, D_MODEL), lambda i, j: (0, 0)),
        ],
        out_specs=pl.BlockSpec((te, D_MODEL), out_row),
        out_shape=jax.ShapeDtypeStruct((T, D_MODEL), F32),
        scratch_shapes=[pltpu.VMEM((tm, D_MODEL), F32)],
        compiler_params=pltpu.CompilerParams(
            dimension_semantics=("parallel", "arbitrary"), vmem_limit_bytes=58 * 2**20),
        name="ffn",
    )(hn, h, wgu_bf, wd_bf, g_post_ffn)


def kernel(x, positions, g_pre_mix, w_in, sinks, a_re, a_im, log_dt, b_re, b_im, c_re, c_im, d_skip,
           w_glu, b_glu, g_attn_out, g_ssm_out, w_o, g_post_mix, g_pre_ffn, w_gate, w_up, w_down,
           g_post_ffn):
    depth = w_in.shape[0]
    T = BATCH * SEQ
    h = x.reshape(T, D_MODEL)
    pos2 = positions.reshape(T, 1)
    for i in range(depth):
        lam_re, lam_im, bblk, cblk, wvt_bf = _ssm_prep(a_re[i], a_im[i], log_dt[i], b_re[i], b_im[i],
                                                       c_re[i], c_im[i], w_in[i])
        w_in_bf = w_in[i].astype(BF16)
        q, k, vt, u, w_o_bf, w_glu_bf = _in_proj(h, pos2, g_pre_mix[i][None, :], w_in_bf, wvt_bf,
                                                 w_o[i], w_glu[i], tm=512)
        half = BATCH // 2
        attn_lo = _attention(q, k, vt, sinks[i], g_attn_out[i][None, :], tq=1024, batches=(0, half))
        ssm_n, wgu_bf, wd_bf = _ssm(u.reshape(SEQ, BATCH, D_SSM), bblk, cblk, lam_re, lam_im,
                                    d_skip[i].reshape(1, D_SSM), w_glu_bf, b_glu[i][None, :],
                                    g_ssm_out[i][None, :], w_gate[i], w_up[i], w_down[i],
                                    run_after=attn_lo, tl=64)
        attn_hi = _attention(q, k, vt, sinks[i], g_attn_out[i][None, :], tq=1024, batches=(half, BATCH))
        h, hn = _out_proj(attn_lo, attn_hi, ssm_n.reshape(SEQ, BATCH * D_SSM), h, w_o_bf,
                          g_post_mix[i][None, :], g_pre_ffn[i][None, :], tm=512)
        h = _ffn(hn, h, wgu_bf, wd_bf, g_post_ffn[i][None, :], tm=1024, te=512)
    return h.reshape(BATCH, SEQ, D_MODEL)
```

```python
import functools
import math

import jax
import jax.numpy as jnp
import numpy as np
from jax import lax
from jax.experimental import pallas as pl
from jax.experimental.pallas import tpu as pltpu

D_MODEL = 2048
BATCH = 8
SEQ = 2048
HEAD_DIM = 64
D_ATTN = 1024
N_Q_HEADS = 16
N_KV_HEADS = 4
Q_PER_KV = 4
D_KV = 256
WINDOW = 128
ROPE_THETA = 10000.0
D_SSM = 1024
SSM_GROUP = 16
N_SSM_GROUPS = 64
SSM_STATE = 64
D_IN = D_ATTN + 2 * D_KV + D_SSM
D_FF = 5632
RMS_EPS = 1e-6

LANES = 128
SUBLANES = 8
GROUPS_PER_TILE = 16
N_GROUP_TILES = N_SSM_GROUPS // GROUPS_PER_TILE
TILE_CH = GROUPS_PER_TILE * SSM_GROUP
TILE_ST = GROUPS_PER_TILE * SSM_STATE
N_CHUNKS = TILE_ST // LANES
D_KV_DUP = 2 * N_KV_HEADS * HEAD_DIM
ROW_SUB = 256
NORM_ROWS = 16
MXU_COLS = 256
N_CAST_STEPS = 32
SCORE_AHEAD = 8

F32 = jnp.float32
BF16 = jnp.bfloat16
NT_DIMS = (((1,), (1,)), ((), ()))


def _re_cols(c):
    return slice(2 * c * LANES, (2 * c + 1) * LANES)


def _im_cols(c):
    return slice((2 * c + 1) * LANES, (2 * c + 2) * LANES)


def _rms(x, g):
    ms = jnp.mean(x * x, axis=-1, keepdims=True)
    return x * lax.rsqrt(ms + RMS_EPS) * g


def _split3(x):
    x1 = x.astype(BF16)
    r1 = x - x1.astype(F32)
    x2 = r1.astype(BF16)
    x3 = (r1 - x2.astype(F32)).astype(BF16)
    return x1, x2, x3


def _ssm_prep_kernel(ar_ref, ai_ref, ldt_ref, br_ref, bi_ref, cr_ref, ci_ref, ep_ref, en_ref, wv_ref,
                     lre_ref, lim_ref, bblk_ref, cblk_ref, wvt_ref):
    wvt_ref[...] = wv_ref[...].T.astype(BF16)

    ar = ar_ref[0]
    ai = ai_ref[0]
    dt = jnp.exp(ldt_ref[0])
    mag = jnp.exp(ar * dt)
    lam_re = mag * jnp.cos(ai * dt)
    lam_im = mag * jnp.sin(ai * dt)
    den = ar * ar + ai * ai
    nr = lam_re - 1.0
    ni = lam_im
    f_re = (nr * ar + ni * ai) / den
    f_im = (ni * ar - nr * ai) / den
    lre_ref[0] = jnp.broadcast_to(lam_re, (BATCH, TILE_ST))
    lim_ref[0] = jnp.broadcast_to(lam_im, (BATCH, TILE_ST))

    ep = ep_ref[...]

    def spread_b(b):
        return sum(lax.dot_general(ep, piece, NT_DIMS, preferred_element_type=F32)
                   for piece in _split3(b))

    pb_re = spread_b(br_ref[0])
    pb_im = spread_b(bi_ref[0])
    row = lax.broadcasted_iota(jnp.int32, (TILE_CH, TILE_ST), 0)
    col = lax.broadcasted_iota(jnp.int32, (TILE_CH, TILE_ST), 1)
    diag = (row // SSM_GROUP) == (col // SSM_STATE)
    bb_re = jnp.where(diag, f_re * pb_re - f_im * pb_im, 0.0).astype(BF16)
    bb_im = jnp.where(diag, f_re * pb_im + f_im * pb_re, 0.0).astype(BF16)
    for c in range(N_CHUNKS):
        bblk_ref[0, :, _re_cols(c)] = bb_re[:, c * LANES:(c + 1) * LANES]
        bblk_ref[0, :, _im_cols(c)] = bb_im[:, c * LANES:(c + 1) * LANES]

    en = en_ref[...]
    row = lax.broadcasted_iota(jnp.int32, (TILE_ST, TILE_CH), 0)
    col = lax.broadcasted_iota(jnp.int32, (TILE_ST, TILE_CH), 1)
    diag = (row // SSM_STATE) == (col // SSM_GROUP)
    pc_re = lax.dot_general(en, cr_ref[0].astype(BF16), NT_DIMS, preferred_element_type=F32)
    pc_im = lax.dot_general(en, ci_ref[0].astype(BF16), NT_DIMS, preferred_element_type=F32)
    cc_re = jnp.where(diag, pc_re, 0.0).astype(BF16)
    cc_im = jnp.where(diag, -pc_im, 0.0).astype(BF16)
    for c in range(N_CHUNKS):
        cblk_ref[0, _re_cols(c), :] = cc_re[c * LANES:(c + 1) * LANES, :]
        cblk_ref[0, _im_cols(c), :] = cc_im[c * LANES:(c + 1) * LANES, :]


def _ssm_prep(a_re, a_im, log_dt, b_re, b_im, c_re, c_im, w_in):
    nt = N_GROUP_TILES
    wv_rows = D_MODEL // nt
    wv_col_block = (D_ATTN + D_KV) // D_KV
    lanes3 = lambda a: a.reshape(nt, 1, TILE_ST)
    ldt = jnp.broadcast_to(log_dt[:, None], (N_SSM_GROUPS, SSM_STATE))
    ep = jnp.asarray(np.tile(np.eye(SSM_GROUP, dtype=np.float32), (GROUPS_PER_TILE, 1)), BF16)
    en = jnp.asarray(np.tile(np.eye(SSM_STATE, dtype=np.float32), (GROUPS_PER_TILE, 1)), BF16)
    t3 = lambda t: (t, 0, 0)
    c2 = lambda t: (0, 0)
    return pl.pallas_call(
        _ssm_prep_kernel,
        grid=(nt,),
        in_specs=[
            pl.BlockSpec((1, 1, TILE_ST), t3),
            pl.BlockSpec((1, 1, TILE_ST), t3),
            pl.BlockSpec((1, 1, TILE_ST), t3),
            pl.BlockSpec((1, TILE_ST, SSM_GROUP), t3),
            pl.BlockSpec((1, TILE_ST, SSM_GROUP), t3),
            pl.BlockSpec((1, TILE_CH, SSM_STATE), t3),
            pl.BlockSpec((1, TILE_CH, SSM_STATE), t3),
            pl.BlockSpec((TILE_CH, SSM_GROUP), c2),
            pl.BlockSpec((TILE_ST, SSM_STATE), c2),
            pl.BlockSpec((wv_rows, D_KV), lambda t: (t, wv_col_block)),
        ],
        out_specs=[
            pl.BlockSpec((1, BATCH, TILE_ST), t3),
            pl.BlockSpec((1, BATCH, TILE_ST), t3),
            pl.BlockSpec((1, TILE_CH, 2 * TILE_ST), t3),
            pl.BlockSpec((1, 2 * TILE_ST, TILE_CH), t3),
            pl.BlockSpec((D_KV, wv_rows), lambda t: (0, t)),
        ],
        out_shape=[
            jax.ShapeDtypeStruct((nt, BATCH, TILE_ST), F32),
            jax.ShapeDtypeStruct((nt, BATCH, TILE_ST), F32),
            jax.ShapeDtypeStruct((nt, TILE_CH, 2 * TILE_ST), BF16),
            jax.ShapeDtypeStruct((nt, 2 * TILE_ST, TILE_CH), BF16),
            jax.ShapeDtypeStruct((D_KV, D_MODEL), BF16),
        ],
        compiler_params=pltpu.CompilerParams(dimension_semantics=("parallel",)),
        name="ssm_prep",
    )(lanes3(a_re), lanes3(a_im), lanes3(ldt),
      b_re.reshape(nt, TILE_ST, SSM_GROUP), b_im.reshape(nt, TILE_ST, SSM_GROUP),
      c_re.reshape(nt, TILE_CH, SSM_STATE), c_im.reshape(nt, TILE_CH, SSM_STATE), ep, en, w_in)


def _inproj_kernel(x_ref, pos_ref, g_ref, invf_ref, sgn_ref, w_ref, wvt_ref, wo_ref, wglu_ref,
                   q_ref, k_ref, vt_ref, u_ref, wo_bf_ref, wglu_bf_ref):
    tm = x_ref.shape[0]
    wo_bf_ref[...] = wo_ref[...].astype(BF16)
    wglu_bf_ref[...] = wglu_ref[...].astype(BF16)

    lane = lax.broadcasted_iota(jnp.int32, (ROW_SUB, LANES), 1)
    first_half = (lane & (HEAD_DIM // 2)) == 0
    first_head = lane < HEAD_DIM
    scale = 1.0 / math.sqrt(HEAD_DIM)
    for r in range(tm // ROW_SUB):
        rs = slice(r * ROW_SUB, (r + 1) * ROW_SUB)
        xn = _rms(x_ref[rs, :], g_ref[...]).astype(BF16)
        ang = pos_ref[rs, :].astype(F32) * invf_ref[...]
        cos = jnp.cos(ang)
        sin = jnp.sin(ang) * sgn_ref[...]

        def rotary(t):
            partner = jnp.where(first_half,
                                pltpu.roll(t, LANES - HEAD_DIM // 2, 1),
                                pltpu.roll(t, HEAD_DIM // 2, 1))
            return t * cos + partner * sin

        def store_dup(ref, c, t):
            swapped = pltpu.roll(t, HEAD_DIM, 1)
            ref[rs, (2 * c) * LANES:(2 * c + 1) * LANES] = jnp.where(first_head, t, swapped).astype(BF16)
            ref[rs, (2 * c + 1) * LANES:(2 * c + 2) * LANES] = jnp.where(first_head, swapped, t).astype(BF16)

        for j in range(D_ATTN // 256):
            p = jnp.dot(xn, w_ref[:, j * 256:(j + 1) * 256], preferred_element_type=F32)
            for c in range(2):
                col = j * 256 + c * LANES
                q_ref[rs, col:col + LANES] = (rotary(p[:, c * LANES:(c + 1) * LANES]) * scale).astype(BF16)
        p = jnp.dot(xn, w_ref[:, D_ATTN:D_ATTN + D_KV], preferred_element_type=F32)
        for c in range(2):
            store_dup(k_ref, c, rotary(p[:, c * LANES:(c + 1) * LANES]))
        vt_ref[:, rs] = lax.dot_general(wvt_ref[...], xn, NT_DIMS,
                                        preferred_element_type=F32).astype(BF16)
        for j in range(D_SSM // 256):
            col = D_ATTN + 2 * D_KV + j * 256
            u_ref[rs, j * 256:(j + 1) * 256] = jnp.dot(xn, w_ref[:, col:col + 256],
                                                        preferred_element_type=F32)


def _in_proj(x2, pos2, g_pre_mix, w_in_bf, wvt_bf, w_o, w_glu, tm):
    T = x2.shape[0]
    nt = SEQ // tm
    n_steps = BATCH * nt
    wo_slab = (D_MODEL // n_steps, D_MODEL)
    wglu_slab = (D_SSM // n_steps, D_SSM)
    half = HEAD_DIM // 2
    invf = ROPE_THETA ** (-np.arange(half, dtype=np.float32) / half)
    invf = np.tile(invf.astype(np.float32), LANES // half)[None, :]
    sgn = np.tile(np.concatenate([-np.ones(half, np.float32), np.ones(half, np.float32)]),
                  LANES // HEAD_DIM)[None, :]
    row = lambda b, i: (b * nt + i, 0)
    const = lambda b, i: (0, 0)
    return pl.pallas_call(
        _inproj_kernel,
        grid=(BATCH, nt),
        in_specs=[
            pl.BlockSpec((tm, D_MODEL), row),
            pl.BlockSpec((tm, 1), row),
            pl.BlockSpec((1, D_MODEL), const),
            pl.BlockSpec((1, LANES), const),
            pl.BlockSpec((1, LANES), const),
            pl.BlockSpec((D_MODEL, D_IN), const, pipeline_mode=pl.Buffered(1)),
            pl.BlockSpec((D_KV, D_MODEL), const, pipeline_mode=pl.Buffered(1)),
            pl.BlockSpec(wo_slab, row),
            pl.BlockSpec(wglu_slab, row),
        ],
        out_specs=[
            pl.BlockSpec((tm, D_ATTN), row),
            pl.BlockSpec((tm, D_KV_DUP), row),
            pl.BlockSpec((D_KV, tm), lambda b, i: (0, b * nt + i)),
            pl.BlockSpec((tm, D_SSM), lambda b, i: (i, b)),
            pl.BlockSpec(wo_slab, row),
            pl.BlockSpec(wglu_slab, row),
        ],
        out_shape=[
            jax.ShapeDtypeStruct((T, D_ATTN), BF16),
            jax.ShapeDtypeStruct((T, D_KV_DUP), BF16),
            jax.ShapeDtypeStruct((D_KV, T), BF16),
            jax.ShapeDtypeStruct((SEQ, BATCH * D_SSM), F32),
            jax.ShapeDtypeStruct((D_MODEL, D_MODEL), BF16),
            jax.ShapeDtypeStruct((D_SSM, D_SSM), BF16),
        ],
        compiler_params=pltpu.CompilerParams(
            dimension_semantics=("parallel", "parallel"), vmem_limit_bytes=52 * 2**20),
        name="in_proj",
    )(x2, pos2, g_pre_mix, jnp.asarray(invf), jnp.asarray(sgn), w_in_bf, wvt_bf, w_o, w_glu)


def _attn_kernel(sinks_ref, q_ref, k_ref, vt_ref, kh_ref, vth_ref, g_ref, o_ref, acc_ref):
    tq = q_ref.shape[0]
    first_tile = pl.program_id(1) == 0
    two = 2 * WINDOW
    key = lax.broadcasted_iota(jnp.int32, (WINDOW, two), 0)
    qry = lax.broadcasted_iota(jnp.int32, (WINDOW, two), 1) % WINDOW
    use_prev = key > qry
    pair0 = lax.broadcasted_iota(jnp.int32, (1, two), 1) < WINDOW
    low_lanes = lax.broadcasted_iota(jnp.int32, (two, LANES), 1) < HEAD_DIM
    zero = jnp.zeros((), BF16)
    no_values = jnp.zeros((HEAD_DIM, two), BF16)

    def keys_of(j):
        lo, hi = j * WINDOW, (j + 1) * WINDOW
        if j == 0:
            return (jnp.concatenate([kh_ref[...], k_ref[lo:hi, :]], axis=0),
                    jnp.concatenate([vth_ref[...], vt_ref[:, lo:hi]], axis=1))
        return k_ref[lo - WINDOW:hi, :], vt_ref[:, lo - WINDOW:hi]

    def scores(j, kv, side):
        lo, hi = j * WINDOW, (j + 1) * WINDOW
        c0 = kv * Q_PER_KV * HEAD_DIM
        kd = keys_of(j)[0][:, kv * LANES:(kv + 1) * LANES]
        ks = jnp.where(low_lanes if side == 0 else jnp.logical_not(low_lanes), kd, zero)
        q2 = jnp.concatenate([q_ref[lo:hi, c0:c0 + LANES], q_ref[lo:hi, c0 + LANES:c0 + 2 * LANES]],
                             axis=0)
        return lax.dot_general(ks, q2, NT_DIMS, preferred_element_type=F32)

    def probs(j, kv, side, st):
        s_prev = st[:WINDOW, :]
        if j == 0:
            s_prev = jnp.where(first_tile, -jnp.inf, s_prev)
        sf = jnp.where(use_prev, s_prev, st[WINDOW:, :])
        sink = jnp.where(pair0, sinks_ref[kv * Q_PER_KV + side], sinks_ref[kv * Q_PER_KV + 2 + side])
        m = jnp.maximum(jnp.max(sf, axis=0, keepdims=True), sink)
        p = jnp.exp(sf - m)
        den = jnp.sum(p, axis=0, keepdims=True) + jnp.exp(sink - m)
        pn = p * (1.0 / den)
        return jnp.concatenate([jnp.where(use_prev, pn, 0.0), jnp.where(use_prev, 0.0, pn)],
                               axis=0).astype(BF16)

    def weighted_values(j, kv, side, pcat):
        vdt = keys_of(j)[1][kv * HEAD_DIM:(kv + 1) * HEAD_DIM, :]
        vst = jnp.concatenate([vdt, no_values] if side == 0 else [no_values, vdt], axis=0)
        return jnp.dot(vst, pcat, preferred_element_type=F32)

    tasks = [(j, kv, side) for j in range(tq // WINDOW) for kv in range(N_KV_HEADS) for side in range(2)]
    pending = {}
    out = None
    for n in range(len(tasks) + SCORE_AHEAD):
        if n < len(tasks):
            pending[n] = scores(*tasks[n])
        if n < SCORE_AHEAD:
            continue
        j, kv, side = tasks[n - SCORE_AHEAD]
        o = weighted_values(j, kv, side, probs(j, kv, side, pending.pop(n - SCORE_AHEAD)))
        if side == 0:
            out = o
            continue
        out = out + o
        c0 = kv * Q_PER_KV * HEAD_DIM
        acc_ref[c0:c0 + LANES, :] = out[:, :WINDOW]
        acc_ref[c0 + LANES:c0 + 2 * LANES, :] = out[:, WINDOW:]
        if kv == N_KV_HEADS - 1:
            a = acc_ref[...]
            ms = jnp.mean(a * a, axis=0, keepdims=True)
            o_ref[j * WINDOW:(j + 1) * WINDOW, :] = (
                (a * lax.rsqrt(ms + RMS_EPS)).T * g_ref[...]).astype(BF16)


def _attention(q, k, vt, sinks, g_attn_out, tq, batches):
    nt = SEQ // tq
    per = tq // WINDOW
    b0, b1 = batches
    row = lambda b, i: ((b0 + b) * nt + i, 0)
    col = lambda b, i: (0, (b0 + b) * nt + i)
    halo = lambda b, i: jnp.maximum((b0 + b) * (SEQ // WINDOW) + i * per - 1, 0)
    return pl.pallas_call(
        _attn_kernel,
        grid=(b1 - b0, nt),
        in_specs=[
            pl.BlockSpec(memory_space=pltpu.SMEM),
            pl.BlockSpec((tq, D_ATTN), row),
            pl.BlockSpec((tq, D_KV_DUP), row),
            pl.BlockSpec((D_KV, tq), col),
            pl.BlockSpec((WINDOW, D_KV_DUP), lambda b, i: (halo(b, i), 0)),
            pl.BlockSpec((D_KV, WINDOW), lambda b, i: (0, halo(b, i))),
            pl.BlockSpec((1, D_ATTN), lambda b, i: (0, 0)),
        ],
        out_specs=pl.BlockSpec((tq, D_ATTN), lambda b, i: (b * nt + i, 0)),
        out_shape=jax.ShapeDtypeStruct(((b1 - b0) * SEQ, D_ATTN), BF16),
        scratch_shapes=[pltpu.VMEM((D_ATTN, WINDOW), F32)],
        compiler_params=pltpu.CompilerParams(dimension_semantics=("parallel", "parallel")),
        name="attention",
    )(sinks, q, k, vt, k, vt, g_attn_out)


def _ssm_kernel(u_ref, bblk_ref, cblk_ref, lre_ref, lim_ref, dskip_ref, wglu_ref, bglu_ref, g_ref,
                wg_ref, wu_ref, wd_ref, after_ref, o_ref, wgu_bf_ref, wd_bf_ref, s_ref, y_ref, carry_ref):
    del after_ref
    tl = u_ref.shape[0]
    rows = tl * BATCH
    for c in range(D_FF // MXU_COLS):
        src = slice(c * MXU_COLS, (c + 1) * MXU_COLS)
        wgu_bf_ref[:, 2 * c * MXU_COLS:(2 * c + 1) * MXU_COLS] = wg_ref[:, src].astype(BF16)
        wgu_bf_ref[:, (2 * c + 1) * MXU_COLS:(2 * c + 2) * MXU_COLS] = wu_ref[:, src].astype(BF16)
    wd_bf_ref[...] = wd_ref[...].astype(BF16)

    @pl.when(pl.program_id(0) == 0)
    def _():
        carry_ref[...] = jnp.zeros_like(carry_ref)

    u = u_ref[...].reshape(rows, D_SSM)
    ub = u.astype(BF16)

    def project_in(gt):
        ch = slice(gt * TILE_CH, (gt + 1) * TILE_CH)
        s_ref[gt % 2] = jnp.dot(ub[:, ch], bblk_ref[gt], preferred_element_type=F32)

    project_in(0)
    for gt in range(N_GROUP_TILES):
        buf = gt % 2
        ch = slice(gt * TILE_CH, (gt + 1) * TILE_CH)
        if gt + 1 < N_GROUP_TILES:
            project_in(gt + 1)
        for c in range(N_CHUNKS):
            lanes = slice(c * LANES, (c + 1) * LANES)
            re_cols, im_cols = _re_cols(c), _im_cols(c)
            lr = lre_ref[gt, :, lanes]
            li = lim_ref[gt, :, lanes]
            sr = carry_ref[gt, :, re_cols]
            si = carry_ref[gt, :, im_cols]
            for t in range(tl):
                r = slice(t * BATCH, (t + 1) * BATCH)
                nr = lr * sr - li * si + s_ref[buf, r, re_cols]
                ni = lr * si + li * sr + s_ref[buf, r, im_cols]
                s_ref[buf, r, re_cols] = nr
                s_ref[buf, r, im_cols] = ni
                sr, si = nr, ni
            carry_ref[gt, :, re_cols] = sr
            carry_ref[gt, :, im_cols] = si
        y = jnp.dot(s_ref[buf].astype(BF16), cblk_ref[gt], preferred_element_type=F32)
        y_ref[:, ch] = y + dskip_ref[:, ch] * u[:, ch]
    z = jax.nn.gelu(y_ref[...])
    gate = jax.nn.sigmoid(jnp.dot(z.astype(BF16), wglu_ref[...], preferred_element_type=F32)
                          + bglu_ref[...])
    o_ref[...] = _rms(z * gate, g_ref[...]).reshape(tl, BATCH, D_SSM)


def _ssm(u3, bblk, cblk, lam_re, lam_im, d_skip, w_glu_bf, b_glu, g_ssm_out, w_gate, w_up, w_down,
         run_after, tl):
    rows = tl * BATCH
    n_tiles = SEQ // tl
    assert n_tiles == N_CAST_STEPS
    c2 = lambda i: (0, 0)
    c3 = lambda i: (0, 0, 0)
    cur = lambda i: i
    slab = lambda i: (cur(i), 0)
    up_slab = (D_MODEL // N_CAST_STEPS, D_FF)
    down_slab = (D_FF // N_CAST_STEPS, D_MODEL)
    return pl.pallas_call(
        _ssm_kernel,
        grid=(n_tiles,),
        in_specs=[
            pl.BlockSpec((tl, BATCH, D_SSM), lambda i: (cur(i), 0, 0)),
            pl.BlockSpec((N_GROUP_TILES, TILE_CH, 2 * TILE_ST), c3),
            pl.BlockSpec((N_GROUP_TILES, 2 * TILE_ST, TILE_CH), c3),
            pl.BlockSpec((N_GROUP_TILES, BATCH, TILE_ST), c3),
            pl.BlockSpec((N_GROUP_TILES, BATCH, TILE_ST), c3),
            pl.BlockSpec((1, D_SSM), c2),
            pl.BlockSpec((D_SSM, D_SSM), c2),
            pl.BlockSpec((1, D_SSM), c2),
            pl.BlockSpec((1, D_SSM), c2),
            pl.BlockSpec(up_slab, slab),
            pl.BlockSpec(up_slab, slab),
            pl.BlockSpec(down_slab, slab),
            pl.BlockSpec(memory_space=pl.ANY),
        ],
        out_specs=[
            pl.BlockSpec((tl, BATCH, D_SSM), lambda i: (i, 0, 0)),
            pl.BlockSpec((up_slab[0], 2 * D_FF), slab),
            pl.BlockSpec(down_slab, slab),
        ],
        out_shape=[
            jax.ShapeDtypeStruct((SEQ, BATCH, D_SSM), F32),
            jax.ShapeDtypeStruct((D_MODEL, 2 * D_FF), BF16),
            jax.ShapeDtypeStruct((D_FF, D_MODEL), BF16),
        ],
        scratch_shapes=[
            pltpu.VMEM((2, rows, 2 * TILE_ST), F32),
            pltpu.VMEM((rows, D_SSM), F32),
            pltpu.VMEM((N_GROUP_TILES, BATCH, 2 * TILE_ST), F32),
        ],
        compiler_params=pltpu.CompilerParams(
            dimension_semantics=("arbitrary",), vmem_limit_bytes=52 * 2**20),
        name="ssm",
    )(u3, bblk, cblk, lam_re, lam_im, d_skip, w_glu_bf, b_glu, g_ssm_out, w_gate, w_up, w_down, run_after)


def _outproj_kernel(attn_lo_ref, attn_hi_ref, ssm_ref, x_ref, wo_ref, gpost_ref, gpre_ref, h_ref, hn_ref):
    tm = x_ref.shape[0]
    low_batches = pl.program_id(0) < BATCH // 2
    for r in range(tm // ROW_SUB):
        rs = slice(r * ROW_SUB, (r + 1) * ROW_SUB)
        attn = jnp.where(low_batches, attn_lo_ref[rs, :], attn_hi_ref[rs, :])
        mix = jnp.dot(attn, wo_ref[:D_ATTN, :], preferred_element_type=F32)
        mix = mix + jnp.dot(ssm_ref[rs, :].astype(BF16), wo_ref[D_ATTN:, :],
                            preferred_element_type=F32)
        h = x_ref[rs, :] + _rms(mix, gpost_ref[...])
        h_ref[rs, :] = h
        hn_ref[rs, :] = _rms(h, gpre_ref[...]).astype(BF16)


def _out_proj(attn_lo, attn_hi, ssm2, x2, w_o_bf, g_post_mix, g_pre_ffn, tm):
    T = x2.shape[0]
    nt = SEQ // tm
    half = BATCH // 2
    row = lambda b, i: (b * nt + i, 0)
    const = lambda b, i: (0, 0)
    lo_row = lambda b, i: (jnp.where(b < half, b * nt + i, half * nt - 1), 0)
    hi_row = lambda b, i: (jnp.where(b < half, 0, (b - half) * nt + i), 0)
    return pl.pallas_call(
        _outproj_kernel,
        grid=(BATCH, nt),
        in_specs=[
            pl.BlockSpec((tm, D_ATTN), lo_row),
            pl.BlockSpec((tm, D_ATTN), hi_row),
            pl.BlockSpec((tm, D_SSM), lambda b, i: (i, b)),
            pl.BlockSpec((tm, D_MODEL), row),
            pl.BlockSpec((D_MODEL, D_MODEL), const),
            pl.BlockSpec((1, D_MODEL), const),
            pl.BlockSpec((1, D_MODEL), const),
        ],
        out_specs=[pl.BlockSpec((tm, D_MODEL), row), pl.BlockSpec((tm, D_MODEL), row)],
        out_shape=[jax.ShapeDtypeStruct((T, D_MODEL), F32), jax.ShapeDtypeStruct((T, D_MODEL), BF16)],
        compiler_params=pltpu.CompilerParams(
            dimension_semantics=("parallel", "parallel"), vmem_limit_bytes=52 * 2**20),
        name="out_proj",
    )(attn_lo, attn_hi, ssm2, x2, w_o_bf, g_post_mix, g_pre_ffn)


def _ffn_kernel(hn_ref, h_ref, wgu_ref, wd_ref, g_ref, o_ref, acc_ref, *, nf):
    j = pl.program_id(1)
    te = o_ref.shape[0]

    def reduce_step(first):
        n_sub = hn_ref.shape[0] // ROW_SUB
        rows = lambda r: slice(r * ROW_SUB, (r + 1) * ROW_SUB)

        def gate_up(r):
            return jnp.dot(hn_ref[rows(r), :], wgu_ref[...], preferred_element_type=F32)

        ahead = gate_up(0)
        for r in range(n_sub):
            gu = ahead
            if r + 1 < n_sub:
                ahead = gate_up(r + 1)
            hid = jnp.concatenate(
                [jax.nn.silu(gu[:, 2 * c * MXU_COLS:(2 * c + 1) * MXU_COLS])
                 * gu[:, (2 * c + 1) * MXU_COLS:(2 * c + 2) * MXU_COLS]
                 for c in range(gu.shape[1] // (2 * MXU_COLS))], axis=1).astype(BF16)
            part = jnp.dot(hid, wd_ref[...], preferred_element_type=F32)
            if first:
                acc_ref[rows(r), :] = part
            else:
                acc_ref[rows(r), :] += part

    @pl.when(j == 0)
    def _():
        reduce_step(True)

    @pl.when(jnp.logical_and(j > 0, j < nf))
    def _():
        reduce_step(False)

    @pl.when(j >= nf)
    def _():
        base = (j - nf) * te
        g = g_ref[...]

        def norm_rows(c, carry):
            r = pl.multiple_of(c * NORM_ROWS, NORM_ROWS)
            a = acc_ref[pl.ds(pl.multiple_of(base + r, NORM_ROWS), NORM_ROWS), :]
            o_ref[pl.ds(r, NORM_ROWS), :] = h_ref[pl.ds(r, NORM_ROWS), :] + _rms(a, g)
            return carry

        lax.fori_loop(0, te // NORM_ROWS, norm_rows, 0, unroll=8)


def _ffn(hn, h, wgu_bf, wd_bf, g_post_ffn, tm, tf, te):
    T = h.shape[0]
    nf = D_FF // tf
    sub = tm // te
    n_tiles = T // tm
    chunk = lambda j: jnp.where(j < nf, j, 0)
    hn_row = lambda i, j: (jnp.minimum(i + (j >= nf).astype(jnp.int32), n_tiles - 1), 0)
    out_row = lambda i, j: (i * sub + jnp.clip(j - nf, 0, sub - 1), 0)
    h_row = lambda i, j: (jnp.where(j == 0, jnp.maximum(i * sub - 1, 0),
                                    i * sub + jnp.clip(j - nf, 0, sub - 1)), 0)
    return pl.pallas_call(
        functools.partial(_ffn_kernel, nf=nf),
        grid=(n_tiles, nf + sub),
        in_specs=[
            pl.BlockSpec((tm, D_MODEL), hn_row),
            pl.BlockSpec((te, D_MODEL), h_row),
            pl.BlockSpec((D_MODEL, 2 * tf), lambda i, j: (0, chunk(j))),
            pl.BlockSpec((tf, D_MODEL), lambda i, j: (chunk(j), 0)),
            pl.BlockSpec((1, D_MODEL), lambda i, j: (0, 0)),
        ],
        out_specs=pl.BlockSpec((te, D_MODEL), out_row),
        out_shape=jax.ShapeDtypeStruct((T, D_MODEL), F32),
        scratch_shapes=[pltpu.VMEM((tm, D_MODEL), F32)],
        compiler_params=pltpu.CompilerParams(
            dimension_semantics=("parallel", "arbitrary"), vmem_limit_bytes=58 * 2**20),
        name="ffn",
    )(hn, h, wgu_bf, wd_bf, g_post_ffn)


def kernel(x, positions, g_pre_mix, w_in, sinks, a_re, a_im, log_dt, b_re, b_im, c_re, c_im, d_skip,
           w_glu, b_glu, g_attn_out, g_ssm_out, w_o, g_post_mix, g_pre_ffn, w_gate, w_up, w_down,
           g_post_ffn):
    depth = w_in.shape[0]
    T = BATCH * SEQ
    h = x.reshape(T, D_MODEL)
    pos2 = positions.reshape(T, 1)
    for i in range(depth):
        lam_re, lam_im, bblk, cblk, wvt_bf = _ssm_prep(a_re[i], a_im[i], log_dt[i], b_re[i], b_im[i],
                                                       c_re[i], c_im[i], w_in[i])
        w_in_bf = w_in[i].astype(BF16)
        q, k, vt, u, w_o_bf, w_glu_bf = _in_proj(h, pos2, g_pre_mix[i][None, :], w_in_bf, wvt_bf,
                                                 w_o[i], w_glu[i], tm=1024)
        half = BATCH // 2
        attn_lo = _attention(q, k, vt, sinks[i], g_attn_out[i][None, :], tq=1024, batches=(0, half))
        ssm_n, wgu_bf, wd_bf = _ssm(u.reshape(SEQ, BATCH, D_SSM), bblk, cblk, lam_re, lam_im,
                                    d_skip[i].reshape(1, D_SSM), w_glu_bf, b_glu[i][None, :],
                                    g_ssm_out[i][None, :], w_gate[i], w_up[i], w_down[i],
                                    run_after=attn_lo, tl=64)
        attn_hi = _attention(q, k, vt, sinks[i], g_attn_out[i][None, :], tq=1024, batches=(half, BATCH))
        h, hn = _out_proj(attn_lo, attn_hi, ssm_n.reshape(SEQ, BATCH * D_SSM), h, w_o_bf,
                          g_post_mix[i][None, :], g_pre_ffn[i][None, :], tm=512)
        h = _ffn(hn, h, wgu_bf, wd_bf, g_post_ffn[i][None, :], tm=1024, tf=512, te=512)
    return h.reshape(BATCH, SEQ, D_MODEL)
```

```python
import math
from typing import NamedTuple

import jax
import jax.numpy as jnp
import numpy as np
from jax import lax
from jax.experimental import pallas as pl
from jax.experimental.pallas import tpu as pltpu

D_MODEL = 2048
BATCH = 8
SEQ = 2048
HEAD_DIM = 64
D_ATTN = 1024
N_KV_HEADS = 4
Q_PER_KV = 4
D_KV = 256
WINDOW = 128
ROPE_THETA = 10000.0
D_SSM = 1024
SSM_GROUP = 16
N_SSM_GROUPS = 64
SSM_STATE = 64
D_IN = D_ATTN + 2 * D_KV + D_SSM
D_FF = 5632
RMS_EPS = 1e-6

LANES = 128
MXU_COLS = 256
VMEM_BYTES = 64 * 2**20
GROUPS_PER_TILE = 16
N_GROUP_TILES = N_SSM_GROUPS // GROUPS_PER_TILE
TILE_CH = GROUPS_PER_TILE * SSM_GROUP
TILE_ST = GROUPS_PER_TILE * SSM_STATE
N_CHUNKS = TILE_ST // LANES
D_KV_DUP = 2 * N_KV_HEADS * HEAD_DIM
ROW_SUB = 256
SCORE_AHEAD = 8


class _Tiles(NamedTuple):
    in_proj_rows: int = 1024
    attention_rows: int = 1024
    ssm_steps: int = 64
    out_proj_rows: int = 512
    ffn_rows: int = 1024
    ffn_cols: int = 512
    ffn_down_rows: int = 512
    vmem_limit: int = VMEM_BYTES * 13 // 16
    ffn_vmem_limit: int = VMEM_BYTES * 29 // 32


TILES = _Tiles()
N_CAST_STEPS = SEQ // TILES.ssm_steps

F32 = jnp.float32
BF16 = jnp.bfloat16
NT_DIMS = (((1,), (1,)), ((), ()))


def _re_cols(c):
    return slice(2 * c * LANES, (2 * c + 1) * LANES)


def _im_cols(c):
    return slice((2 * c + 1) * LANES, (2 * c + 2) * LANES)


def _rms(x, g):
    ms = jnp.mean(x * x, axis=-1, keepdims=True)
    return x * lax.rsqrt(ms + RMS_EPS) * g


def _split3(x):
    x1 = x.astype(BF16)
    r1 = x - x1.astype(F32)
    x2 = r1.astype(BF16)
    x3 = (r1 - x2.astype(F32)).astype(BF16)
    return x1, x2, x3


def _ssm_prep_kernel(ar_ref, ai_ref, ldt_ref, br_ref, bi_ref, cr_ref, ci_ref, ep_ref, en_ref, wv_ref,
                     lre_ref, lim_ref, bblk_ref, cblk_ref, wvt_ref):
    wvt_ref[...] = wv_ref[...].T.astype(BF16)

    ar = ar_ref[0]
    ai = ai_ref[0]
    dt = jnp.exp(ldt_ref[0])
    mag = jnp.exp(ar * dt)
    lam_re = mag * jnp.cos(ai * dt)
    lam_im = mag * jnp.sin(ai * dt)
    den = ar * ar + ai * ai
    nr = lam_re - 1.0
    ni = lam_im
    f_re = (nr * ar + ni * ai) / den
    f_im = (ni * ar - nr * ai) / den
    lre_ref[0] = jnp.broadcast_to(lam_re, (BATCH, TILE_ST))
    lim_ref[0] = jnp.broadcast_to(lam_im, (BATCH, TILE_ST))

    ep = ep_ref[...]

    def spread_b(b):
        return sum(lax.dot_general(ep, piece, NT_DIMS, preferred_element_type=F32)
                   for piece in _split3(b))

    pb_re = spread_b(br_ref[0])
    pb_im = spread_b(bi_ref[0])
    row = lax.broadcasted_iota(jnp.int32, (TILE_CH, TILE_ST), 0)
    col = lax.broadcasted_iota(jnp.int32, (TILE_CH, TILE_ST), 1)
    diag = (row // SSM_GROUP) == (col // SSM_STATE)
    bb_re = jnp.where(diag, f_re * pb_re - f_im * pb_im, 0.0).astype(BF16)
    bb_im = jnp.where(diag, f_re * pb_im + f_im * pb_re, 0.0).astype(BF16)
    for c in range(N_CHUNKS):
        bblk_ref[0, :, _re_cols(c)] = bb_re[:, c * LANES:(c + 1) * LANES]
        bblk_ref[0, :, _im_cols(c)] = bb_im[:, c * LANES:(c + 1) * LANES]

    en = en_ref[...]
    row = lax.broadcasted_iota(jnp.int32, (TILE_ST, TILE_CH), 0)
    col = lax.broadcasted_iota(jnp.int32, (TILE_ST, TILE_CH), 1)
    diag = (row // SSM_STATE) == (col // SSM_GROUP)
    pc_re = lax.dot_general(en, cr_ref[0].astype(BF16), NT_DIMS, preferred_element_type=F32)
    pc_im = lax.dot_general(en, ci_ref[0].astype(BF16), NT_DIMS, preferred_element_type=F32)
    cc_re = jnp.where(diag, pc_re, 0.0).astype(BF16)
    cc_im = jnp.where(diag, -pc_im, 0.0).astype(BF16)
    for c in range(N_CHUNKS):
        cblk_ref[0, _re_cols(c), :] = cc_re[c * LANES:(c + 1) * LANES, :]
        cblk_ref[0, _im_cols(c), :] = cc_im[c * LANES:(c + 1) * LANES, :]


def _ssm_prep(a_re, a_im, log_dt, b_re, b_im, c_re, c_im, w_in):
    nt = N_GROUP_TILES
    wv_rows = D_MODEL // nt
    wv_col_block = (D_ATTN + D_KV) // D_KV
    lanes3 = lambda a: a.reshape(nt, 1, TILE_ST)
    ldt = jnp.broadcast_to(log_dt[:, None], (N_SSM_GROUPS, SSM_STATE))
    ep = jnp.asarray(np.tile(np.eye(SSM_GROUP, dtype=np.float32), (GROUPS_PER_TILE, 1)), BF16)
    en = jnp.asarray(np.tile(np.eye(SSM_STATE, dtype=np.float32), (GROUPS_PER_TILE, 1)), BF16)
    t3 = lambda t: (t, 0, 0)
    c2 = lambda t: (0, 0)
    return pl.pallas_call(
        _ssm_prep_kernel,
        grid=(nt,),
        in_specs=[
            pl.BlockSpec((1, 1, TILE_ST), t3),
            pl.BlockSpec((1, 1, TILE_ST), t3),
            pl.BlockSpec((1, 1, TILE_ST), t3),
            pl.BlockSpec((1, TILE_ST, SSM_GROUP), t3),
            pl.BlockSpec((1, TILE_ST, SSM_GROUP), t3),
            pl.BlockSpec((1, TILE_CH, SSM_STATE), t3),
            pl.BlockSpec((1, TILE_CH, SSM_STATE), t3),
            pl.BlockSpec((TILE_CH, SSM_GROUP), c2),
            pl.BlockSpec((TILE_ST, SSM_STATE), c2),
            pl.BlockSpec((wv_rows, D_KV), lambda t: (t, wv_col_block)),
        ],
        out_specs=[
            pl.BlockSpec((1, BATCH, TILE_ST), t3),
            pl.BlockSpec((1, BATCH, TILE_ST), t3),
            pl.BlockSpec((1, TILE_CH, 2 * TILE_ST), t3),
            pl.BlockSpec((1, 2 * TILE_ST, TILE_CH), t3),
            pl.BlockSpec((D_KV, wv_rows), lambda t: (0, t)),
        ],
        out_shape=[
            jax.ShapeDtypeStruct((nt, BATCH, TILE_ST), F32),
            jax.ShapeDtypeStruct((nt, BATCH, TILE_ST), F32),
            jax.ShapeDtypeStruct((nt, TILE_CH, 2 * TILE_ST), BF16),
            jax.ShapeDtypeStruct((nt, 2 * TILE_ST, TILE_CH), BF16),
            jax.ShapeDtypeStruct((D_KV, D_MODEL), BF16),
        ],
        compiler_params=pltpu.CompilerParams(dimension_semantics=("parallel",)),
        name="ssm_prep",
    )(lanes3(a_re), lanes3(a_im), lanes3(ldt),
      b_re.reshape(nt, TILE_ST, SSM_GROUP), b_im.reshape(nt, TILE_ST, SSM_GROUP),
      c_re.reshape(nt, TILE_CH, SSM_STATE), c_im.reshape(nt, TILE_CH, SSM_STATE), ep, en, w_in)


def _inproj_kernel(x_ref, pos_ref, g_ref, invf_ref, sgn_ref, w_ref, wvt_ref, wo_ref, wglu_ref,
                   q_ref, k_ref, vt_ref, u_ref, wo_bf_ref, wglu_bf_ref):
    tm = x_ref.shape[0]
    wo_bf_ref[...] = wo_ref[...].astype(BF16)
    wglu_bf_ref[...] = wglu_ref[...].astype(BF16)

    lane = lax.broadcasted_iota(jnp.int32, (ROW_SUB, LANES), 1)
    first_half = (lane & (HEAD_DIM // 2)) == 0
    first_head = lane < HEAD_DIM
    scale = 1.0 / math.sqrt(HEAD_DIM)
    for r in range(tm // ROW_SUB):
        rs = slice(r * ROW_SUB, (r + 1) * ROW_SUB)
        xn = _rms(x_ref[rs, :], g_ref[...]).astype(BF16)
        ang = pos_ref[rs, :].astype(F32) * invf_ref[...]
        cos = jnp.cos(ang)
        sin = jnp.sin(ang) * sgn_ref[...]

        def rotary(t):
            partner = jnp.where(first_half,
                                pltpu.roll(t, LANES - HEAD_DIM // 2, 1),
                                pltpu.roll(t, HEAD_DIM // 2, 1))
            return t * cos + partner * sin

        def store_dup(ref, c, t):
            swapped = pltpu.roll(t, HEAD_DIM, 1)
            ref[rs, (2 * c) * LANES:(2 * c + 1) * LANES] = jnp.where(first_head, t, swapped).astype(BF16)
            ref[rs, (2 * c + 1) * LANES:(2 * c + 2) * LANES] = jnp.where(first_head, swapped, t).astype(BF16)

        for j in range(D_ATTN // MXU_COLS):
            p = jnp.dot(xn, w_ref[:, j * MXU_COLS:(j + 1) * MXU_COLS], preferred_element_type=F32)
            for c in range(MXU_COLS // LANES):
                col = j * MXU_COLS + c * LANES
                q_ref[rs, col:col + LANES] = (rotary(p[:, c * LANES:(c + 1) * LANES]) * scale).astype(BF16)
        p = jnp.dot(xn, w_ref[:, D_ATTN:D_ATTN + D_KV], preferred_element_type=F32)
        for c in range(D_KV // LANES):
            store_dup(k_ref, c, rotary(p[:, c * LANES:(c + 1) * LANES]))
        vt_ref[:, rs] = lax.dot_general(wvt_ref[...], xn, NT_DIMS,
                                        preferred_element_type=F32).astype(BF16)
        for j in range(D_SSM // MXU_COLS):
            col = D_ATTN + 2 * D_KV + j * MXU_COLS
            u_ref[rs, j * MXU_COLS:(j + 1) * MXU_COLS] = jnp.dot(xn, w_ref[:, col:col + MXU_COLS],
                                                                 preferred_element_type=F32)


def _in_proj(x2, pos2, g_pre_mix, w_in_bf, wvt_bf, w_o, w_glu):
    T = x2.shape[0]
    tm = TILES.in_proj_rows
    nt = SEQ // tm
    n_steps = BATCH * nt
    wo_slab = (D_MODEL // n_steps, D_MODEL)
    wglu_slab = (D_SSM // n_steps, D_SSM)
    half = HEAD_DIM // 2
    invf = ROPE_THETA ** (-np.arange(half, dtype=np.float32) / half)
    invf = np.tile(invf.astype(np.float32), LANES // half)[None, :]
    sgn = np.tile(np.concatenate([-np.ones(half, np.float32), np.ones(half, np.float32)]),
                  LANES // HEAD_DIM)[None, :]
    row = lambda b, i: (b * nt + i, 0)
    const = lambda b, i: (0, 0)
    return pl.pallas_call(
        _inproj_kernel,
        grid=(BATCH, nt),
        in_specs=[
            pl.BlockSpec((tm, D_MODEL), row),
            pl.BlockSpec((tm, 1), row),
            pl.BlockSpec((1, D_MODEL), const),
            pl.BlockSpec((1, LANES), const),
            pl.BlockSpec((1, LANES), const),
            pl.BlockSpec((D_MODEL, D_IN), const, pipeline_mode=pl.Buffered(1)),
            pl.BlockSpec((D_KV, D_MODEL), const, pipeline_mode=pl.Buffered(1)),
            pl.BlockSpec(wo_slab, row),
            pl.BlockSpec(wglu_slab, row),
        ],
        out_specs=[
            pl.BlockSpec((tm, D_ATTN), row),
            pl.BlockSpec((tm, D_KV_DUP), row),
            pl.BlockSpec((D_KV, tm), lambda b, i: (0, b * nt + i)),
            pl.BlockSpec((tm, D_SSM), lambda b, i: (i, b)),
            pl.BlockSpec(wo_slab, row),
            pl.BlockSpec(wglu_slab, row),
        ],
        out_shape=[
            jax.ShapeDtypeStruct((T, D_ATTN), BF16),
            jax.ShapeDtypeStruct((T, D_KV_DUP), BF16),
            jax.ShapeDtypeStruct((D_KV, T), BF16),
            jax.ShapeDtypeStruct((SEQ, BATCH * D_SSM), F32),
            jax.ShapeDtypeStruct((D_MODEL, D_MODEL), BF16),
            jax.ShapeDtypeStruct((D_SSM, D_SSM), BF16),
        ],
        compiler_params=pltpu.CompilerParams(
            dimension_semantics=("parallel", "parallel"), vmem_limit_bytes=TILES.vmem_limit),
        name="in_proj",
    )(x2, pos2, g_pre_mix, jnp.asarray(invf), jnp.asarray(sgn), w_in_bf, wvt_bf, w_o, w_glu)


def _attn_kernel(sinks_ref, q_ref, k_ref, vt_ref, kh_ref, vth_ref, g_ref, o_ref, acc_ref):
    tq = q_ref.shape[0]
    first_tile = pl.program_id(1) == 0
    two = 2 * WINDOW
    key = lax.broadcasted_iota(jnp.int32, (WINDOW, two), 0)
    qry = lax.broadcasted_iota(jnp.int32, (WINDOW, two), 1) % WINDOW
    use_prev = key > qry
    pair0 = lax.broadcasted_iota(jnp.int32, (1, two), 1) < WINDOW
    low_lanes = lax.broadcasted_iota(jnp.int32, (two, LANES), 1) < HEAD_DIM
    zero = jnp.zeros((), BF16)
    no_values = jnp.zeros((HEAD_DIM, two), BF16)

    def keys_of(j):
        lo, hi = j * WINDOW, (j + 1) * WINDOW
        if j == 0:
            return (jnp.concatenate([kh_ref[...], k_ref[lo:hi, :]], axis=0),
                    jnp.concatenate([vth_ref[...], vt_ref[:, lo:hi]], axis=1))
        return k_ref[lo - WINDOW:hi, :], vt_ref[:, lo - WINDOW:hi]

    def scores(j, kv, side):
        lo, hi = j * WINDOW, (j + 1) * WINDOW
        c0 = kv * Q_PER_KV * HEAD_DIM
        kd = keys_of(j)[0][:, kv * LANES:(kv + 1) * LANES]
        ks = jnp.where(low_lanes if side == 0 else jnp.logical_not(low_lanes), kd, zero)
        q2 = jnp.concatenate([q_ref[lo:hi, c0:c0 + LANES], q_ref[lo:hi, c0 + LANES:c0 + 2 * LANES]],
                             axis=0)
        return lax.dot_general(ks, q2, NT_DIMS, preferred_element_type=F32)

    def probs(j, kv, side, st):
        s_prev = st[:WINDOW, :]
        if j == 0:
            s_prev = jnp.where(first_tile, -jnp.inf, s_prev)
        sf = jnp.where(use_prev, s_prev, st[WINDOW:, :])
        sink = jnp.where(pair0, sinks_ref[kv * Q_PER_KV + side], sinks_ref[kv * Q_PER_KV + 2 + side])
        m = jnp.maximum(jnp.max(sf, axis=0, keepdims=True), sink)
        p = jnp.exp(sf - m)
        den = jnp.sum(p, axis=0, keepdims=True) + jnp.exp(sink - m)
        pn = p * (1.0 / den)
        return jnp.concatenate([jnp.where(use_prev, pn, 0.0), jnp.where(use_prev, 0.0, pn)],
                               axis=0).astype(BF16)

    def weighted_values(j, kv, side, pcat):
        vdt = keys_of(j)[1][kv * HEAD_DIM:(kv + 1) * HEAD_DIM, :]
        vst = jnp.concatenate([vdt, no_values] if side == 0 else [no_values, vdt], axis=0)
        return jnp.dot(vst, pcat, preferred_element_type=F32)

    tasks = [(j, kv, side) for j in range(tq // WINDOW) for kv in range(N_KV_HEADS) for side in range(2)]
    pending = {}
    out = None
    for n in range(len(tasks) + SCORE_AHEAD):
        if n < len(tasks):
            pending[n] = scores(*tasks[n])
        if n < SCORE_AHEAD:
            continue
        j, kv, side = tasks[n - SCORE_AHEAD]
        o = weighted_values(j, kv, side, probs(j, kv, side, pending.pop(n - SCORE_AHEAD)))
        if side == 0:
            out = o
            continue
        out = out + o
        c0 = kv * Q_PER_KV * HEAD_DIM
        acc_ref[c0:c0 + LANES, :] = out[:, :WINDOW]
        acc_ref[c0 + LANES:c0 + 2 * LANES, :] = out[:, WINDOW:]
        if kv == N_KV_HEADS - 1:
            a = acc_ref[...]
            ms = jnp.mean(a * a, axis=0, keepdims=True)
            o_ref[j * WINDOW:(j + 1) * WINDOW, :] = (
                (a * lax.rsqrt(ms + RMS_EPS)).T * g_ref[...]).astype(BF16)


def _attention(q, k, vt, sinks, g_attn_out, batches):
    tq = TILES.attention_rows
    nt = SEQ // tq
    per = tq // WINDOW
    b0, b1 = batches
    row = lambda b, i: ((b0 + b) * nt + i, 0)
    col = lambda b, i: (0, (b0 + b) * nt + i)
    halo = lambda b, i: jnp.maximum((b0 + b) * (SEQ // WINDOW) + i * per - 1, 0)
    return pl.pallas_call(
        _attn_kernel,
        grid=(b1 - b0, nt),
        in_specs=[
            pl.BlockSpec(memory_space=pltpu.SMEM),
            pl.BlockSpec((tq, D_ATTN), row),
            pl.BlockSpec((tq, D_KV_DUP), row),
            pl.BlockSpec((D_KV, tq), col),
            pl.BlockSpec((WINDOW, D_KV_DUP), lambda b, i: (halo(b, i), 0)),
            pl.BlockSpec((D_KV, WINDOW), lambda b, i: (0, halo(b, i))),
            pl.BlockSpec((1, D_ATTN), lambda b, i: (0, 0)),
        ],
        out_specs=pl.BlockSpec((tq, D_ATTN), lambda b, i: (b * nt + i, 0)),
        out_shape=jax.ShapeDtypeStruct(((b1 - b0) * SEQ, D_ATTN), BF16),
        scratch_shapes=[pltpu.VMEM((D_ATTN, WINDOW), F32)],
        compiler_params=pltpu.CompilerParams(dimension_semantics=("parallel", "parallel")),
        name="attention",
    )(sinks, q, k, vt, k, vt, g_attn_out)


def _ssm_kernel(u_ref, bblk_ref, cblk_ref, lre_ref, lim_ref, dskip_ref, wglu_ref, bglu_ref, g_ref,
                wg_ref, wu_ref, wd_ref, after_ref, o_ref, wgu_bf_ref, wd_bf_ref, s_ref, y_ref, carry_ref):
    del after_ref
    tl = u_ref.shape[0]
    rows = tl * BATCH
    for c in range(D_FF // MXU_COLS):
        src = slice(c * MXU_COLS, (c + 1) * MXU_COLS)
        wgu_bf_ref[:, 2 * c * MXU_COLS:(2 * c + 1) * MXU_COLS] = wg_ref[:, src].astype(BF16)
        wgu_bf_ref[:, (2 * c + 1) * MXU_COLS:(2 * c + 2) * MXU_COLS] = wu_ref[:, src].astype(BF16)
    wd_bf_ref[...] = wd_ref[...].astype(BF16)

    @pl.when(pl.program_id(0) == 0)
    def _():
        carry_ref[...] = jnp.zeros_like(carry_ref)

    u = u_ref[...].reshape(rows, D_SSM)
    ub = u.astype(BF16)

    def project_in(gt):
        ch = slice(gt * TILE_CH, (gt + 1) * TILE_CH)
        s_ref[gt % 2] = jnp.dot(ub[:, ch], bblk_ref[gt], preferred_element_type=F32)

    project_in(0)
    for gt in range(N_GROUP_TILES):
        buf = gt % 2
        ch = slice(gt * TILE_CH, (gt + 1) * TILE_CH)
        if gt + 1 < N_GROUP_TILES:
            project_in(gt + 1)
        for c in range(N_CHUNKS):
            lanes = slice(c * LANES, (c + 1) * LANES)
            re_cols, im_cols = _re_cols(c), _im_cols(c)
            lr = lre_ref[gt, :, lanes]
            li = lim_ref[gt, :, lanes]
            sr = carry_ref[gt, :, re_cols]
            si = carry_ref[gt, :, im_cols]
            for t in range(tl):
                r = slice(t * BATCH, (t + 1) * BATCH)
                nr = lr * sr - li * si + s_ref[buf, r, re_cols]
                ni = lr * si + li * sr + s_ref[buf, r, im_cols]
                s_ref[buf, r, re_cols] = nr
                s_ref[buf, r, im_cols] = ni
                sr, si = nr, ni
            carry_ref[gt, :, re_cols] = sr
            carry_ref[gt, :, im_cols] = si
        y = jnp.dot(s_ref[buf].astype(BF16), cblk_ref[gt], preferred_element_type=F32)
        y_ref[:, ch] = y + dskip_ref[:, ch] * u[:, ch]
    z = jax.nn.gelu(y_ref[...])
    gate = jax.nn.sigmoid(jnp.dot(z.astype(BF16), wglu_ref[...], preferred_element_type=F32)
                          + bglu_ref[...])
    o_ref[...] = _rms(z * gate, g_ref[...]).reshape(tl, BATCH, D_SSM)


def _ssm(u3, bblk, cblk, lam_re, lam_im, d_skip, w_glu_bf, b_glu, g_ssm_out, w_gate, w_up, w_down,
         run_after):
    tl = TILES.ssm_steps
    rows = tl * BATCH
    c2 = lambda i: (0, 0)
    c3 = lambda i: (0, 0, 0)
    slab = lambda i: (i, 0)
    up_slab = (D_MODEL // N_CAST_STEPS, D_FF)
    down_slab = (D_FF // N_CAST_STEPS, D_MODEL)
    return pl.pallas_call(
        _ssm_kernel,
        grid=(N_CAST_STEPS,),
        in_specs=[
            pl.BlockSpec((tl, BATCH, D_SSM), lambda i: (i, 0, 0)),
            pl.BlockSpec((N_GROUP_TILES, TILE_CH, 2 * TILE_ST), c3),
            pl.BlockSpec((N_GROUP_TILES, 2 * TILE_ST, TILE_CH), c3),
            pl.BlockSpec((N_GROUP_TILES, BATCH, TILE_ST), c3),
            pl.BlockSpec((N_GROUP_TILES, BATCH, TILE_ST), c3),
            pl.BlockSpec((1, D_SSM), c2),
            pl.BlockSpec((D_SSM, D_SSM), c2),
            pl.BlockSpec((1, D_SSM), c2),
            pl.BlockSpec((1, D_SSM), c2),
            pl.BlockSpec(up_slab, slab),
            pl.BlockSpec(up_slab, slab),
            pl.BlockSpec(down_slab, slab),
            pl.BlockSpec(memory_space=pl.ANY),
        ],
        out_specs=[
            pl.BlockSpec((tl, BATCH, D_SSM), lambda i: (i, 0, 0)),
            pl.BlockSpec((up_slab[0], 2 * D_FF), slab),
            pl.BlockSpec(down_slab, slab),
        ],
        out_shape=[
            jax.ShapeDtypeStruct((SEQ, BATCH, D_SSM), F32),
            jax.ShapeDtypeStruct((D_MODEL, 2 * D_FF), BF16),
            jax.ShapeDtypeStruct((D_FF, D_MODEL), BF16),
        ],
        scratch_shapes=[
            pltpu.VMEM((2, rows, 2 * TILE_ST), F32),
            pltpu.VMEM((rows, D_SSM), F32),
            pltpu.VMEM((N_GROUP_TILES, BATCH, 2 * TILE_ST), F32),
        ],
        compiler_params=pltpu.CompilerParams(
            dimension_semantics=("arbitrary",), vmem_limit_bytes=TILES.vmem_limit),
        name="ssm",
    )(u3, bblk, cblk, lam_re, lam_im, d_skip, w_glu_bf, b_glu, g_ssm_out, w_gate, w_up, w_down, run_after)


def _outproj_kernel(attn_lo_ref, attn_hi_ref, ssm_ref, x_ref, wo_ref, gpost_ref, gpre_ref, h_ref, hn_ref):
    tm = x_ref.shape[0]
    low_batches = pl.program_id(0) < BATCH // 2
    for r in range(tm // ROW_SUB):
        rs = slice(r * ROW_SUB, (r + 1) * ROW_SUB)
        attn = jnp.where(low_batches, attn_lo_ref[rs, :], attn_hi_ref[rs, :])
        mix = jnp.dot(attn, wo_ref[:D_ATTN, :], preferred_element_type=F32)
        mix = mix + jnp.dot(ssm_ref[rs, :].astype(BF16), wo_ref[D_ATTN:, :],
                            preferred_element_type=F32)
        h = x_ref[rs, :] + _rms(mix, gpost_ref[...])
        h_ref[rs, :] = h
        hn_ref[rs, :] = _rms(h, gpre_ref[...]).astype(BF16)


def _out_proj(attn_lo, attn_hi, ssm2, x2, w_o_bf, g_post_mix, g_pre_ffn):
    T = x2.shape[0]
    tm = TILES.out_proj_rows
    nt = SEQ // tm
    half = BATCH // 2
    row = lambda b, i: (b * nt + i, 0)
    const = lambda b, i: (0, 0)
    lo_row = lambda b, i: (jnp.where(b < half, b * nt + i, half * nt - 1), 0)
    hi_row = lambda b, i: (jnp.where(b < half, 0, (b - half) * nt + i), 0)
    return pl.pallas_call(
        _outproj_kernel,
        grid=(BATCH, nt),
        in_specs=[
            pl.BlockSpec((tm, D_ATTN), lo_row),
            pl.BlockSpec((tm, D_ATTN), hi_row),
            pl.BlockSpec((tm, D_SSM), lambda b, i: (i, b)),
            pl.BlockSpec((tm, D_MODEL), row),
            pl.BlockSpec((D_MODEL, D_MODEL), const),
            pl.BlockSpec((1, D_MODEL), const),
            pl.BlockSpec((1, D_MODEL), const),
        ],
        out_specs=[pl.BlockSpec((tm, D_MODEL), row), pl.BlockSpec((tm, D_MODEL), row)],
        out_shape=[jax.ShapeDtypeStruct((T, D_MODEL), F32), jax.ShapeDtypeStruct((T, D_MODEL), BF16)],
        compiler_params=pltpu.CompilerParams(
            dimension_semantics=("parallel", "parallel"), vmem_limit_bytes=TILES.vmem_limit),
        name="out_proj",
    )(attn_lo, attn_hi, ssm2, x2, w_o_bf, g_post_mix, g_pre_ffn)


def _ffn_up_kernel(hn_ref, wgu_ref, hid_ref):
    for r in range(hn_ref.shape[0] // ROW_SUB):
        rs = slice(r * ROW_SUB, (r + 1) * ROW_SUB)
        gu = jnp.dot(hn_ref[rs, :], wgu_ref[...], preferred_element_type=F32)
        hid_ref[rs, :] = jnp.concatenate(
            [jax.nn.silu(gu[:, 2 * c * MXU_COLS:(2 * c + 1) * MXU_COLS])
             * gu[:, (2 * c + 1) * MXU_COLS:(2 * c + 2) * MXU_COLS]
             for c in range(gu.shape[1] // (2 * MXU_COLS))], axis=1).astype(BF16)


def _ffn_down_kernel(hid_ref, wd_ref, h_ref, g_ref, o_ref):
    for r in range(hid_ref.shape[0] // ROW_SUB):
        rs = slice(r * ROW_SUB, (r + 1) * ROW_SUB)
        ff = jnp.dot(hid_ref[rs, :], wd_ref[...], preferred_element_type=F32)
        o_ref[rs, :] = h_ref[rs, :] + _rms(ff, g_ref[...])


def _ffn(hn, h, wgu_bf, wd_bf, g_post_ffn):
    T = h.shape[0]
    tm, tf, td = TILES.ffn_rows, TILES.ffn_cols, TILES.ffn_down_rows
    hid = pl.pallas_call(
        _ffn_up_kernel,
        grid=(D_FF // tf, T // tm),
        in_specs=[
            pl.BlockSpec((tm, D_MODEL), lambda f, i: (i, 0)),
            pl.BlockSpec((D_MODEL, 2 * tf), lambda f, i: (0, f)),
        ],
        out_specs=pl.BlockSpec((tm, tf), lambda f, i: (i, f)),
        out_shape=jax.ShapeDtypeStruct((T, D_FF), BF16),
        compiler_params=pltpu.CompilerParams(
            dimension_semantics=("parallel", "parallel"), vmem_limit_bytes=TILES.vmem_limit),
        name="ffn_up",
    )(hn, wgu_bf)
    return pl.pallas_call(
        _ffn_down_kernel,
        grid=(T // td,),
        in_specs=[
            pl.BlockSpec((td, D_FF), lambda i: (i, 0)),
            pl.BlockSpec((D_FF, D_MODEL), lambda i: (0, 0), pipeline_mode=pl.Buffered(1)),
            pl.BlockSpec((td, D_MODEL), lambda i: (i, 0)),
            pl.BlockSpec((1, D_MODEL), lambda i: (0, 0)),
        ],
        out_specs=pl.BlockSpec((td, D_MODEL), lambda i: (i, 0)),
        out_shape=jax.ShapeDtypeStruct((T, D_MODEL), F32),
        compiler_params=pltpu.CompilerParams(
            dimension_semantics=("parallel",), vmem_limit_bytes=TILES.ffn_vmem_limit),
        name="ffn_down",
    )(hid, wd_bf, h, g_post_ffn)


def kernel(x, positions, g_pre_mix, w_in, sinks, a_re, a_im, log_dt, b_re, b_im, c_re, c_im, d_skip,
           w_glu, b_glu, g_attn_out, g_ssm_out, w_o, g_post_mix, g_pre_ffn, w_gate, w_up, w_down,
           g_post_ffn):
    depth = w_in.shape[0]
    T = BATCH * SEQ
    h = x.reshape(T, D_MODEL)
    pos2 = positions.reshape(T, 1)
    for i in range(depth):
        lam_re, lam_im, bblk, cblk, wvt_bf = _ssm_prep(a_re[i], a_im[i], log_dt[i], b_re[i], b_im[i],
                                                       c_re[i], c_im[i], w_in[i])
        w_in_bf = w_in[i].astype(BF16)
        q, k, vt, u, w_o_bf, w_glu_bf = _in_proj(h, pos2, g_pre_mix[i][None, :], w_in_bf, wvt_bf,
                                                 w_o[i], w_glu[i])
        half = BATCH // 2
        attn_lo = _attention(q, k, vt, sinks[i], g_attn_out[i][None, :], batches=(0, half))
        ssm_n, wgu_bf, wd_bf = _ssm(u.reshape(SEQ, BATCH, D_SSM), bblk, cblk, lam_re, lam_im,
                                    d_skip[i].reshape(1, D_SSM), w_glu_bf, b_glu[i][None, :],
                                    g_ssm_out[i][None, :], w_gate[i], w_up[i], w_down[i],
                                    run_after=attn_lo)
        attn_hi = _attention(q, k, vt, sinks[i], g_attn_out[i][None, :], batches=(half, BATCH))
        h, hn = _out_proj(attn_lo, attn_hi, ssm_n.reshape(SEQ, BATCH * D_SSM), h, w_o_bf,
                          g_post_mix[i][None, :], g_pre_ffn[i][None, :])
        h = _ffn(hn, h, wgu_bf, wd_bf, g_post_ffn[i][None, :])
    return h.reshape(BATCH, SEQ, D_MODEL)
```

```python
import math
from typing import NamedTuple

import jax
import jax.numpy as jnp
import numpy as np
from jax import lax
from jax.experimental import pallas as pl
from jax.experimental.pallas import tpu as pltpu

D_MODEL = 2048
BATCH = 8
SEQ = 2048
HEAD_DIM = 64
D_ATTN = 1024
N_KV_HEADS = 4
Q_PER_KV = 4
D_KV = 256
WINDOW = 128
ROPE_THETA = 10000.0
D_SSM = 1024
SSM_GROUP = 16
N_SSM_GROUPS = 64
SSM_STATE = 64
D_IN = D_ATTN + 2 * D_KV + D_SSM
D_FF = 5632
RMS_EPS = 1e-6

LANES = 128
MXU_COLS = 256
VMEM_BYTES = 64 * 2**20
GROUPS_PER_TILE = 16
N_GROUP_TILES = N_SSM_GROUPS // GROUPS_PER_TILE
TILE_CH = GROUPS_PER_TILE * SSM_GROUP
TILE_ST = GROUPS_PER_TILE * SSM_STATE
N_CHUNKS = TILE_ST // LANES
D_KV_DUP = 2 * N_KV_HEADS * HEAD_DIM
ROW_SUB = 256
SCORE_AHEAD = 8


class _Tiles(NamedTuple):
    in_proj_rows: int = 1024
    attention_rows: int = 1024
    ssm_steps: int = 64
    out_proj_rows: int = 512
    ffn_rows: int = 2048
    ffn_cols: int = 512
    ffn_down_rows: int = 512
    vmem_limit: int = VMEM_BYTES * 13 // 16
    ffn_vmem_limit: int = VMEM_BYTES * 29 // 32


TILES = _Tiles()
N_CAST_STEPS = SEQ // TILES.ssm_steps

F32 = jnp.float32
BF16 = jnp.bfloat16
NT_DIMS = (((1,), (1,)), ((), ()))


def _re_cols(c):
    return slice(2 * c * LANES, (2 * c + 1) * LANES)


def _im_cols(c):
    return slice((2 * c + 1) * LANES, (2 * c + 2) * LANES)


def _rms(x, g):
    ms = jnp.mean(x * x, axis=-1, keepdims=True)
    return x * lax.rsqrt(ms + RMS_EPS) * g


def _split3(x):
    x1 = x.astype(BF16)
    r1 = x - x1.astype(F32)
    x2 = r1.astype(BF16)
    x3 = (r1 - x2.astype(F32)).astype(BF16)
    return x1, x2, x3


def _ssm_prep_kernel(ar_ref, ai_ref, ldt_ref, br_ref, bi_ref, cr_ref, ci_ref, ep_ref, en_ref, wv_ref,
                     lre_ref, lim_ref, bblk_ref, cblk_ref, wvt_ref):
    wvt_ref[...] = wv_ref[...].T.astype(BF16)

    ar = ar_ref[0]
    ai = ai_ref[0]
    dt = jnp.exp(ldt_ref[0])
    mag = jnp.exp(ar * dt)
    lam_re = mag * jnp.cos(ai * dt)
    lam_im = mag * jnp.sin(ai * dt)
    den = ar * ar + ai * ai
    nr = lam_re - 1.0
    ni = lam_im
    f_re = (nr * ar + ni * ai) / den
    f_im = (ni * ar - nr * ai) / den
    lre_ref[0] = jnp.broadcast_to(lam_re, (BATCH, TILE_ST))
    lim_ref[0] = jnp.broadcast_to(lam_im, (BATCH, TILE_ST))

    ep = ep_ref[...]

    def spread_b(b):
        return sum(lax.dot_general(ep, piece, NT_DIMS, preferred_element_type=F32)
                   for piece in _split3(b))

    pb_re = spread_b(br_ref[0])
    pb_im = spread_b(bi_ref[0])
    row = lax.broadcasted_iota(jnp.int32, (TILE_CH, TILE_ST), 0)
    col = lax.broadcasted_iota(jnp.int32, (TILE_CH, TILE_ST), 1)
    diag = (row // SSM_GROUP) == (col // SSM_STATE)
    bb_re = jnp.where(diag, f_re * pb_re - f_im * pb_im, 0.0).astype(BF16)
    bb_im = jnp.where(diag, f_re * pb_im + f_im * pb_re, 0.0).astype(BF16)
    for c in range(N_CHUNKS):
        bblk_ref[0, :, _re_cols(c)] = bb_re[:, c * LANES:(c + 1) * LANES]
        bblk_ref[0, :, _im_cols(c)] = bb_im[:, c * LANES:(c + 1) * LANES]

    en = en_ref[...]
    row = lax.broadcasted_iota(jnp.int32, (TILE_ST, TILE_CH), 0)
    col = lax.broadcasted_iota(jnp.int32, (TILE_ST, TILE_CH), 1)
    diag = (row // SSM_STATE) == (col // SSM_GROUP)
    pc_re = lax.dot_general(en, cr_ref[0].astype(BF16), NT_DIMS, preferred_element_type=F32)
    pc_im = lax.dot_general(en, ci_ref[0].astype(BF16), NT_DIMS, preferred_element_type=F32)
    cc_re = jnp.where(diag, pc_re, 0.0).astype(BF16)
    cc_im = jnp.where(diag, -pc_im, 0.0).astype(BF16)
    for c in range(N_CHUNKS):
        cblk_ref[0, _re_cols(c), :] = cc_re[c * LANES:(c + 1) * LANES, :]
        cblk_ref[0, _im_cols(c), :] = cc_im[c * LANES:(c + 1) * LANES, :]


def _ssm_prep(a_re, a_im, log_dt, b_re, b_im, c_re, c_im, w_in):
    nt = N_GROUP_TILES
    wv_rows = D_MODEL // nt
    wv_col_block = (D_ATTN + D_KV) // D_KV
    lanes3 = lambda a: a.reshape(nt, 1, TILE_ST)
    ldt = jnp.broadcast_to(log_dt[:, None], (N_SSM_GROUPS, SSM_STATE))
    ep = jnp.asarray(np.tile(np.eye(SSM_GROUP, dtype=np.float32), (GROUPS_PER_TILE, 1)), BF16)
    en = jnp.asarray(np.tile(np.eye(SSM_STATE, dtype=np.float32), (GROUPS_PER_TILE, 1)), BF16)
    t3 = lambda t: (t, 0, 0)
    c2 = lambda t: (0, 0)
    return pl.pallas_call(
        _ssm_prep_kernel,
        grid=(nt,),
        in_specs=[
            pl.BlockSpec((1, 1, TILE_ST), t3),
            pl.BlockSpec((1, 1, TILE_ST), t3),
            pl.BlockSpec((1, 1, TILE_ST), t3),
            pl.BlockSpec((1, TILE_ST, SSM_GROUP), t3),
            pl.BlockSpec((1, TILE_ST, SSM_GROUP), t3),
            pl.BlockSpec((1, TILE_CH, SSM_STATE), t3),
            pl.BlockSpec((1, TILE_CH, SSM_STATE), t3),
            pl.BlockSpec((TILE_CH, SSM_GROUP), c2),
            pl.BlockSpec((TILE_ST, SSM_STATE), c2),
            pl.BlockSpec((wv_rows, D_KV), lambda t: (t, wv_col_block)),
        ],
        out_specs=[
            pl.BlockSpec((1, BATCH, TILE_ST), t3),
            pl.BlockSpec((1, BATCH, TILE_ST), t3),
            pl.BlockSpec((1, TILE_CH, 2 * TILE_ST), t3),
            pl.BlockSpec((1, 2 * TILE_ST, TILE_CH), t3),
            pl.BlockSpec((D_KV, wv_rows), lambda t: (0, t)),
        ],
        out_shape=[
            jax.ShapeDtypeStruct((nt, BATCH, TILE_ST), F32),
            jax.ShapeDtypeStruct((nt, BATCH, TILE_ST), F32),
            jax.ShapeDtypeStruct((nt, TILE_CH, 2 * TILE_ST), BF16),
            jax.ShapeDtypeStruct((nt, 2 * TILE_ST, TILE_CH), BF16),
            jax.ShapeDtypeStruct((D_KV, D_MODEL), BF16),
        ],
        compiler_params=pltpu.CompilerParams(dimension_semantics=("parallel",)),
        name="ssm_prep",
    )(lanes3(a_re), lanes3(a_im), lanes3(ldt),
      b_re.reshape(nt, TILE_ST, SSM_GROUP), b_im.reshape(nt, TILE_ST, SSM_GROUP),
      c_re.reshape(nt, TILE_CH, SSM_STATE), c_im.reshape(nt, TILE_CH, SSM_STATE), ep, en, w_in)


def _inproj_kernel(x_ref, pos_ref, g_ref, invf_ref, sgn_ref, w_ref, wvt_ref, wo_ref, wglu_ref,
                   q_ref, k_ref, vt_ref, u_ref, wo_bf_ref, wglu_bf_ref):
    tm = x_ref.shape[0]
    wo_bf_ref[...] = wo_ref[...].astype(BF16)
    wglu_bf_ref[...] = wglu_ref[...].astype(BF16)

    lane = lax.broadcasted_iota(jnp.int32, (ROW_SUB, LANES), 1)
    first_half = (lane & (HEAD_DIM // 2)) == 0
    first_head = lane < HEAD_DIM
    scale = 1.0 / math.sqrt(HEAD_DIM)
    for r in range(tm // ROW_SUB):
        rs = slice(r * ROW_SUB, (r + 1) * ROW_SUB)
        xn = _rms(x_ref[rs, :], g_ref[...]).astype(BF16)
        ang = pos_ref[rs, :].astype(F32) * invf_ref[...]
        cos = jnp.cos(ang)
        sin = jnp.sin(ang) * sgn_ref[...]

        def rotary(t):
            partner = jnp.where(first_half,
                                pltpu.roll(t, LANES - HEAD_DIM // 2, 1),
                                pltpu.roll(t, HEAD_DIM // 2, 1))
            return t * cos + partner * sin

        def store_dup(ref, c, t):
            swapped = pltpu.roll(t, HEAD_DIM, 1)
            ref[rs, (2 * c) * LANES:(2 * c + 1) * LANES] = jnp.where(first_head, t, swapped).astype(BF16)
            ref[rs, (2 * c + 1) * LANES:(2 * c + 2) * LANES] = jnp.where(first_head, swapped, t).astype(BF16)

        for j in range(D_ATTN // MXU_COLS):
            p = jnp.dot(xn, w_ref[:, j * MXU_COLS:(j + 1) * MXU_COLS], preferred_element_type=F32)
            for c in range(MXU_COLS // LANES):
                col = j * MXU_COLS + c * LANES
                q_ref[rs, col:col + LANES] = (rotary(p[:, c * LANES:(c + 1) * LANES]) * scale).astype(BF16)
        p = jnp.dot(xn, w_ref[:, D_ATTN:D_ATTN + D_KV], preferred_element_type=F32)
        for c in range(D_KV // LANES):
            store_dup(k_ref, c, rotary(p[:, c * LANES:(c + 1) * LANES]))
        vt_ref[:, rs] = lax.dot_general(wvt_ref[...], xn, NT_DIMS,
                                        preferred_element_type=F32).astype(BF16)
        for j in range(D_SSM // MXU_COLS):
            col = D_ATTN + 2 * D_KV + j * MXU_COLS
            u_ref[rs, j * MXU_COLS:(j + 1) * MXU_COLS] = jnp.dot(xn, w_ref[:, col:col + MXU_COLS],
                                                                 preferred_element_type=F32)


def _in_proj(x2, pos2, g_pre_mix, w_in_bf, wvt_bf, w_o, w_glu):
    T = x2.shape[0]
    tm = TILES.in_proj_rows
    nt = SEQ // tm
    n_steps = BATCH * nt
    wo_slab = (D_MODEL // n_steps, D_MODEL)
    wglu_slab = (D_SSM // n_steps, D_SSM)
    half = HEAD_DIM // 2
    invf = ROPE_THETA ** (-np.arange(half, dtype=np.float32) / half)
    invf = np.tile(invf.astype(np.float32), LANES // half)[None, :]
    sgn = np.tile(np.concatenate([-np.ones(half, np.float32), np.ones(half, np.float32)]),
                  LANES // HEAD_DIM)[None, :]
    row = lambda b, i: (b * nt + i, 0)
    const = lambda b, i: (0, 0)
    return pl.pallas_call(
        _inproj_kernel,
        grid=(BATCH, nt),
        in_specs=[
            pl.BlockSpec((tm, D_MODEL), row),
            pl.BlockSpec((tm, 1), row),
            pl.BlockSpec((1, D_MODEL), const),
            pl.BlockSpec((1, LANES), const),
            pl.BlockSpec((1, LANES), const),
            pl.BlockSpec((D_MODEL, D_IN), const, pipeline_mode=pl.Buffered(1)),
            pl.BlockSpec((D_KV, D_MODEL), const, pipeline_mode=pl.Buffered(1)),
            pl.BlockSpec(wo_slab, row),
            pl.BlockSpec(wglu_slab, row),
        ],
        out_specs=[
            pl.BlockSpec((tm, D_ATTN), row),
            pl.BlockSpec((tm, D_KV_DUP), row),
            pl.BlockSpec((D_KV, tm), lambda b, i: (0, b * nt + i)),
            pl.BlockSpec((tm, D_SSM), lambda b, i: (i, b)),
            pl.BlockSpec(wo_slab, row),
            pl.BlockSpec(wglu_slab, row),
        ],
        out_shape=[
            jax.ShapeDtypeStruct((T, D_ATTN), BF16),
            jax.ShapeDtypeStruct((T, D_KV_DUP), BF16),
            jax.ShapeDtypeStruct((D_KV, T), BF16),
            jax.ShapeDtypeStruct((SEQ, BATCH * D_SSM), F32),
            jax.ShapeDtypeStruct((D_MODEL, D_MODEL), BF16),
            jax.ShapeDtypeStruct((D_SSM, D_SSM), BF16),
        ],
        compiler_params=pltpu.CompilerParams(
            dimension_semantics=("parallel", "parallel"), vmem_limit_bytes=TILES.vmem_limit),
        name="in_proj",
    )(x2, pos2, g_pre_mix, jnp.asarray(invf), jnp.asarray(sgn), w_in_bf, wvt_bf, w_o, w_glu)


def _attn_kernel(sinks_ref, q_ref, k_ref, vt_ref, kh_ref, vth_ref, g_ref, o_ref, acc_ref):
    tq = q_ref.shape[0]
    first_tile = pl.program_id(1) == 0
    two = 2 * WINDOW
    key = lax.broadcasted_iota(jnp.int32, (WINDOW, two), 0)
    qry = lax.broadcasted_iota(jnp.int32, (WINDOW, two), 1) % WINDOW
    use_prev = key > qry
    pair0 = lax.broadcasted_iota(jnp.int32, (1, two), 1) < WINDOW
    low_lanes = lax.broadcasted_iota(jnp.int32, (two, LANES), 1) < HEAD_DIM
    zero = jnp.zeros((), BF16)
    no_values = jnp.zeros((HEAD_DIM, two), BF16)

    def keys_of(j):
        lo, hi = j * WINDOW, (j + 1) * WINDOW
        if j == 0:
            return (jnp.concatenate([kh_ref[...], k_ref[lo:hi, :]], axis=0),
                    jnp.concatenate([vth_ref[...], vt_ref[:, lo:hi]], axis=1))
        return k_ref[lo - WINDOW:hi, :], vt_ref[:, lo - WINDOW:hi]

    def scores(j, kv, side):
        lo, hi = j * WINDOW, (j + 1) * WINDOW
        c0 = kv * Q_PER_KV * HEAD_DIM
        kd = keys_of(j)[0][:, kv * LANES:(kv + 1) * LANES]
        ks = jnp.where(low_lanes if side == 0 else jnp.logical_not(low_lanes), kd, zero)
        q2 = jnp.concatenate([q_ref[lo:hi, c0:c0 + LANES], q_ref[lo:hi, c0 + LANES:c0 + 2 * LANES]],
                             axis=0)
        return lax.dot_general(ks, q2, NT_DIMS, preferred_element_type=F32)

    def probs(j, kv, side, st):
        s_prev = st[:WINDOW, :]
        if j == 0:
            s_prev = jnp.where(first_tile, -jnp.inf, s_prev)
        sf = jnp.where(use_prev, s_prev, st[WINDOW:, :])
        sink = jnp.where(pair0, sinks_ref[kv * Q_PER_KV + side], sinks_ref[kv * Q_PER_KV + 2 + side])
        m = jnp.maximum(jnp.max(sf, axis=0, keepdims=True), sink)
        p = jnp.exp(sf - m)
        den = jnp.sum(p, axis=0, keepdims=True) + jnp.exp(sink - m)
        pn = p * (1.0 / den)
        return jnp.concatenate([jnp.where(use_prev, pn, 0.0), jnp.where(use_prev, 0.0, pn)],
                               axis=0).astype(BF16)

    def weighted_values(j, kv, side, pcat):
        vdt = keys_of(j)[1][kv * HEAD_DIM:(kv + 1) * HEAD_DIM, :]
        vst = jnp.concatenate([vdt, no_values] if side == 0 else [no_values, vdt], axis=0)
        return jnp.dot(vst, pcat, preferred_element_type=F32)

    tasks = [(j, kv, side) for j in range(tq // WINDOW) for kv in range(N_KV_HEADS) for side in range(2)]
    pending = {}
    out = None
    for n in range(len(tasks) + SCORE_AHEAD):
        if n < len(tasks):
            pending[n] = scores(*tasks[n])
        if n < SCORE_AHEAD:
            continue
        j, kv, side = tasks[n - SCORE_AHEAD]
        o = weighted_values(j, kv, side, probs(j, kv, side, pending.pop(n - SCORE_AHEAD)))
        if side == 0:
            out = o
            continue
        out = out + o
        c0 = kv * Q_PER_KV * HEAD_DIM
        acc_ref[c0:c0 + LANES, :] = out[:, :WINDOW]
        acc_ref[c0 + LANES:c0 + 2 * LANES, :] = out[:, WINDOW:]
        if kv == N_KV_HEADS - 1:
            a = acc_ref[...]
            ms = jnp.mean(a * a, axis=0, keepdims=True)
            o_ref[j * WINDOW:(j + 1) * WINDOW, :] = (
                (a * lax.rsqrt(ms + RMS_EPS)).T * g_ref[...]).astype(BF16)


def _attention(q, k, vt, sinks, g_attn_out, batches):
    tq = TILES.attention_rows
    nt = SEQ // tq
    per = tq // WINDOW
    b0, b1 = batches
    row = lambda b, i: ((b0 + b) * nt + i, 0)
    col = lambda b, i: (0, (b0 + b) * nt + i)
    halo = lambda b, i: jnp.maximum((b0 + b) * (SEQ // WINDOW) + i * per - 1, 0)
    return pl.pallas_call(
        _attn_kernel,
        grid=(b1 - b0, nt),
        in_specs=[
            pl.BlockSpec(memory_space=pltpu.SMEM),
            pl.BlockSpec((tq, D_ATTN), row),
            pl.BlockSpec((tq, D_KV_DUP), row),
            pl.BlockSpec((D_KV, tq), col),
            pl.BlockSpec((WINDOW, D_KV_DUP), lambda b, i: (halo(b, i), 0)),
            pl.BlockSpec((D_KV, WINDOW), lambda b, i: (0, halo(b, i))),
            pl.BlockSpec((1, D_ATTN), lambda b, i: (0, 0)),
        ],
        out_specs=pl.BlockSpec((tq, D_ATTN), lambda b, i: (b * nt + i, 0)),
        out_shape=jax.ShapeDtypeStruct(((b1 - b0) * SEQ, D_ATTN), BF16),
        scratch_shapes=[pltpu.VMEM((D_ATTN, WINDOW), F32)],
        compiler_params=pltpu.CompilerParams(dimension_semantics=("parallel", "parallel")),
        name="attention",
    )(sinks, q, k, vt, k, vt, g_attn_out)


def _ssm_kernel(u_ref, bblk_ref, cblk_ref, lre_ref, lim_ref, dskip_ref, wglu_ref, bglu_ref, g_ref,
                wg_ref, wu_ref, wd_ref, after_ref, o_ref, wgu_bf_ref, wd_bf_ref, s_ref, y_ref, carry_ref):
    del after_ref
    tl = u_ref.shape[0]
    rows = tl * BATCH
    for c in range(D_FF // MXU_COLS):
        src = slice(c * MXU_COLS, (c + 1) * MXU_COLS)
        wgu_bf_ref[:, 2 * c * MXU_COLS:(2 * c + 1) * MXU_COLS] = wg_ref[:, src].astype(BF16)
        wgu_bf_ref[:, (2 * c + 1) * MXU_COLS:(2 * c + 2) * MXU_COLS] = wu_ref[:, src].astype(BF16)
    wd_bf_ref[...] = wd_ref[...].astype(BF16)

    @pl.when(pl.program_id(0) == 0)
    def _():
        carry_ref[...] = jnp.zeros_like(carry_ref)

    u = u_ref[...].reshape(rows, D_SSM)
    ub = u.astype(BF16)

    def project_in(gt):
        ch = slice(gt * TILE_CH, (gt + 1) * TILE_CH)
        s_ref[gt % 2] = jnp.dot(ub[:, ch], bblk_ref[gt], preferred_element_type=F32)

    project_in(0)
    for gt in range(N_GROUP_TILES):
        buf = gt % 2
        ch = slice(gt * TILE_CH, (gt + 1) * TILE_CH)
        if gt + 1 < N_GROUP_TILES:
            project_in(gt + 1)
        for c in range(N_CHUNKS):
            lanes = slice(c * LANES, (c + 1) * LANES)
            re_cols, im_cols = _re_cols(c), _im_cols(c)
            lr = lre_ref[gt, :, lanes]
            li = lim_ref[gt, :, lanes]
            sr = carry_ref[gt, :, re_cols]
            si = carry_ref[gt, :, im_cols]
            for t in range(tl):
                r = slice(t * BATCH, (t + 1) * BATCH)
                nr = lr * sr - li * si + s_ref[buf, r, re_cols]
                ni = lr * si + li * sr + s_ref[buf, r, im_cols]
                s_ref[buf, r, re_cols] = nr
                s_ref[buf, r, im_cols] = ni
                sr, si = nr, ni
            carry_ref[gt, :, re_cols] = sr
            carry_ref[gt, :, im_cols] = si
        y = jnp.dot(s_ref[buf].astype(BF16), cblk_ref[gt], preferred_element_type=F32)
        y_ref[:, ch] = y + dskip_ref[:, ch] * u[:, ch]
    z = jax.nn.gelu(y_ref[...])
    gate = jax.nn.sigmoid(jnp.dot(z.astype(BF16), wglu_ref[...], preferred_element_type=F32)
                          + bglu_ref[...])
    o_ref[...] = _rms(z * gate, g_ref[...]).reshape(tl, BATCH, D_SSM)


def _ssm(u3, bblk, cblk, lam_re, lam_im, d_skip, w_glu_bf, b_glu, g_ssm_out, w_gate, w_up, w_down,
         run_after):
    tl = TILES.ssm_steps
    rows = tl * BATCH
    c2 = lambda i: (0, 0)
    c3 = lambda i: (0, 0, 0)
    slab = lambda i: (i, 0)
    up_slab = (D_MODEL // N_CAST_STEPS, D_FF)
    down_slab = (D_FF // N_CAST_STEPS, D_MODEL)
    return pl.pallas_call(
        _ssm_kernel,
        grid=(N_CAST_STEPS,),
        in_specs=[
            pl.BlockSpec((tl, BATCH, D_SSM), lambda i: (i, 0, 0)),
            pl.BlockSpec((N_GROUP_TILES, TILE_CH, 2 * TILE_ST), c3),
            pl.BlockSpec((N_GROUP_TILES, 2 * TILE_ST, TILE_CH), c3),
            pl.BlockSpec((N_GROUP_TILES, BATCH, TILE_ST), c3),
            pl.BlockSpec((N_GROUP_TILES, BATCH, TILE_ST), c3),
            pl.BlockSpec((1, D_SSM), c2),
            pl.BlockSpec((D_SSM, D_SSM), c2),
            pl.BlockSpec((1, D_SSM), c2),
            pl.BlockSpec((1, D_SSM), c2),
            pl.BlockSpec(up_slab, slab),
            pl.BlockSpec(up_slab, slab),
            pl.BlockSpec(down_slab, slab),
            pl.BlockSpec(memory_space=pl.ANY),
        ],
        out_specs=[
            pl.BlockSpec((tl, BATCH, D_SSM), lambda i: (i, 0, 0)),
            pl.BlockSpec((up_slab[0], 2 * D_FF), slab),
            pl.BlockSpec(down_slab, slab),
        ],
        out_shape=[
            jax.ShapeDtypeStruct((SEQ, BATCH, D_SSM), F32),
            jax.ShapeDtypeStruct((D_MODEL, 2 * D_FF), BF16),
            jax.ShapeDtypeStruct((D_FF, D_MODEL), BF16),
        ],
        scratch_shapes=[
            pltpu.VMEM((2, rows, 2 * TILE_ST), F32),
            pltpu.VMEM((rows, D_SSM), F32),
            pltpu.VMEM((N_GROUP_TILES, BATCH, 2 * TILE_ST), F32),
        ],
        compiler_params=pltpu.CompilerParams(
            dimension_semantics=("arbitrary",), vmem_limit_bytes=TILES.vmem_limit),
        name="ssm",
    )(u3, bblk, cblk, lam_re, lam_im, d_skip, w_glu_bf, b_glu, g_ssm_out, w_gate, w_up, w_down, run_after)


def _outproj_kernel(attn_lo_ref, attn_hi_ref, ssm_ref, x_ref, wo_ref, gpost_ref, gpre_ref, h_ref, hn_ref):
    tm = x_ref.shape[0]
    low_batches = pl.program_id(0) < BATCH // 2
    for r in range(tm // ROW_SUB):
        rs = slice(r * ROW_SUB, (r + 1) * ROW_SUB)
        attn = jnp.where(low_batches, attn_lo_ref[rs, :], attn_hi_ref[rs, :])
        mix = jnp.dot(attn, wo_ref[:D_ATTN, :], preferred_element_type=F32)
        mix = mix + jnp.dot(ssm_ref[rs, :].astype(BF16), wo_ref[D_ATTN:, :],
                            preferred_element_type=F32)
        h = x_ref[rs, :] + _rms(mix, gpost_ref[...])
        h_ref[rs, :] = h
        hn_ref[rs, :] = _rms(h, gpre_ref[...]).astype(BF16)


def _out_proj(attn_lo, attn_hi, ssm2, x2, w_o_bf, g_post_mix, g_pre_ffn):
    T = x2.shape[0]
    tm = TILES.out_proj_rows
    nt = SEQ // tm
    half = BATCH // 2
    row = lambda b, i: (b * nt + i, 0)
    const = lambda b, i: (0, 0)
    lo_row = lambda b, i: (jnp.where(b < half, b * nt + i, half * nt - 1), 0)
    hi_row = lambda b, i: (jnp.where(b < half, 0, (b - half) * nt + i), 0)
    return pl.pallas_call(
        _outproj_kernel,
        grid=(BATCH, nt),
        in_specs=[
            pl.BlockSpec((tm, D_ATTN), lo_row),
            pl.BlockSpec((tm, D_ATTN), hi_row),
            pl.BlockSpec((tm, D_SSM), lambda b, i: (i, b)),
            pl.BlockSpec((tm, D_MODEL), row),
            pl.BlockSpec((D_MODEL, D_MODEL), const),
            pl.BlockSpec((1, D_MODEL), const),
            pl.BlockSpec((1, D_MODEL), const),
        ],
        out_specs=[pl.BlockSpec((tm, D_MODEL), row), pl.BlockSpec((tm, D_MODEL), row)],
        out_shape=[jax.ShapeDtypeStruct((T, D_MODEL), F32), jax.ShapeDtypeStruct((T, D_MODEL), BF16)],
        compiler_params=pltpu.CompilerParams(
            dimension_semantics=("parallel", "parallel"), vmem_limit_bytes=TILES.vmem_limit),
        name="out_proj",
    )(attn_lo, attn_hi, ssm2, x2, w_o_bf, g_post_mix, g_pre_ffn)


def _ffn_up_kernel(hn_ref, wgu_ref, hid_ref):
    for r in range(hn_ref.shape[0] // ROW_SUB):
        rs = slice(r * ROW_SUB, (r + 1) * ROW_SUB)
        gu = jnp.dot(hn_ref[rs, :], wgu_ref[...], preferred_element_type=F32)
        hid_ref[rs, :] = jnp.concatenate(
            [jax.nn.silu(gu[:, 2 * c * MXU_COLS:(2 * c + 1) * MXU_COLS])
             * gu[:, (2 * c + 1) * MXU_COLS:(2 * c + 2) * MXU_COLS]
             for c in range(gu.shape[1] // (2 * MXU_COLS))], axis=1).astype(BF16)


def _ffn_down_kernel(hid_ref, wd_ref, h_ref, g_ref, o_ref):
    for r in range(hid_ref.shape[0] // ROW_SUB):
        rs = slice(r * ROW_SUB, (r + 1) * ROW_SUB)
        ff = jnp.dot(hid_ref[rs, :], wd_ref[...], preferred_element_type=F32)
        o_ref[rs, :] = h_ref[rs, :] + _rms(ff, g_ref[...])


def _ffn(hn, h, wgu_bf, wd_bf, g_post_ffn):
    T = h.shape[0]
    tm, tf, td = TILES.ffn_rows, TILES.ffn_cols, TILES.ffn_down_rows
    hid = pl.pallas_call(
        _ffn_up_kernel,
        grid=(D_FF // tf, T // tm),
        in_specs=[
            pl.BlockSpec((tm, D_MODEL), lambda f, i: (i, 0)),
            pl.BlockSpec((D_MODEL, 2 * tf), lambda f, i: (0, f)),
        ],
        out_specs=pl.BlockSpec((tm, tf), lambda f, i: (i, f)),
        out_shape=jax.ShapeDtypeStruct((T, D_FF), BF16),
        compiler_params=pltpu.CompilerParams(
            dimension_semantics=("parallel", "parallel"), vmem_limit_bytes=TILES.vmem_limit),
        name="ffn_up",
    )(hn, wgu_bf)
    return pl.pallas_call(
        _ffn_down_kernel,
        grid=(T // td,),
        in_specs=[
            pl.BlockSpec((td, D_FF), lambda i: (i, 0)),
            pl.BlockSpec((D_FF, D_MODEL), lambda i: (0, 0), pipeline_mode=pl.Buffered(1)),
            pl.BlockSpec((td, D_MODEL), lambda i: (i, 0)),
            pl.BlockSpec((1, D_MODEL), lambda i: (0, 0)),
        ],
        out_specs=pl.BlockSpec((td, D_MODEL), lambda i: (i, 0)),
        out_shape=jax.ShapeDtypeStruct((T, D_MODEL), F32),
        compiler_params=pltpu.CompilerParams(
            dimension_semantics=("parallel",), vmem_limit_bytes=TILES.ffn_vmem_limit),
        name="ffn_down",
    )(hid, wd_bf, h, g_post_ffn)


def kernel(x, positions, g_pre_mix, w_in, sinks, a_re, a_im, log_dt, b_re, b_im, c_re, c_im, d_skip,
           w_glu, b_glu, g_attn_out, g_ssm_out, w_o, g_post_mix, g_pre_ffn, w_gate, w_up, w_down,
           g_post_ffn):
    depth = w_in.shape[0]
    T = BATCH * SEQ
    h = x.reshape(T, D_MODEL)
    pos2 = positions.reshape(T, 1)
    for i in range(depth):
        lam_re, lam_im, bblk, cblk, wvt_bf = _ssm_prep(a_re[i], a_im[i], log_dt[i], b_re[i], b_im[i],
                                                       c_re[i], c_im[i], w_in[i])
        w_in_bf = w_in[i].astype(BF16)
        q, k, vt, u, w_o_bf, w_glu_bf = _in_proj(h, pos2, g_pre_mix[i][None, :], w_in_bf, wvt_bf,
                                                 w_o[i], w_glu[i])
        half = BATCH // 2
        attn_lo = _attention(q, k, vt, sinks[i], g_attn_out[i][None, :], batches=(0, half))
        ssm_n, wgu_bf, wd_bf = _ssm(u.reshape(SEQ, BATCH, D_SSM), bblk, cblk, lam_re, lam_im,
                                    d_skip[i].reshape(1, D_SSM), w_glu_bf, b_glu[i][None, :],
                                    g_ssm_out[i][None, :], w_gate[i], w_up[i], w_down[i],
                                    run_after=attn_lo)
        attn_hi = _attention(q, k, vt, sinks[i], g_attn_out[i][None, :], batches=(half, BATCH))
        h, hn = _out_proj(attn_lo, attn_hi, ssm_n.reshape(SEQ, BATCH * D_SSM), h, w_o_bf,
                          g_post_mix[i][None, :], g_pre_ffn[i][None, :])
        h = _ffn(hn, h, wgu_bf, wd_bf, g_post_ffn[i][None, :])
    return h.reshape(BATCH, SEQ, D_MODEL)
```

```python
import math
from typing import NamedTuple

import jax
import jax.numpy as jnp
import numpy as np
from jax import lax
from jax.experimental import pallas as pl
from jax.experimental.pallas import tpu as pltpu

D_MODEL = 2048
BATCH = 8
SEQ = 2048
HEAD_DIM = 64
D_ATTN = 1024
N_KV_HEADS = 4
Q_PER_KV = 4
D_KV = 256
WINDOW = 128
ROPE_THETA = 10000.0
D_SSM = 1024
SSM_GROUP = 16
N_SSM_GROUPS = 64
SSM_STATE = 64
D_IN = D_ATTN + 2 * D_KV + D_SSM
D_FF = 5632
RMS_EPS = 1e-6

LANES = 128
MXU_COLS = 256
VMEM_BYTES = 64 * 2**20
GROUPS_PER_TILE = 16
N_GROUP_TILES = N_SSM_GROUPS // GROUPS_PER_TILE
TILE_CH = GROUPS_PER_TILE * SSM_GROUP
TILE_ST = GROUPS_PER_TILE * SSM_STATE
N_CHUNKS = TILE_ST // LANES
D_KV_DUP = 2 * N_KV_HEADS * HEAD_DIM
ROW_SUB = 256
SCORE_AHEAD = 8


class _Tiles(NamedTuple):
    in_proj_rows: int = 1024
    attention_rows: int = 1024
    ssm_steps: int = 64
    out_proj_rows: int = 512
    ffn_rows: int = 4096
    ffn_cols: int = 512
    ffn_down_rows: int = 512
    vmem_limit: int = VMEM_BYTES * 13 // 16
    ffn_vmem_limit: int = VMEM_BYTES * 29 // 32


TILES = _Tiles()
N_CAST_STEPS = SEQ // TILES.ssm_steps

F32 = jnp.float32
BF16 = jnp.bfloat16
NT_DIMS = (((1,), (1,)), ((), ()))


def _re_cols(c):
    return slice(2 * c * LANES, (2 * c + 1) * LANES)


def _im_cols(c):
    return slice((2 * c + 1) * LANES, (2 * c + 2) * LANES)


def _rms(x, g):
    ms = jnp.mean(x * x, axis=-1, keepdims=True)
    return x * lax.rsqrt(ms + RMS_EPS) * g


def _split3(x):
    x1 = x.astype(BF16)
    r1 = x - x1.astype(F32)
    x2 = r1.astype(BF16)
    x3 = (r1 - x2.astype(F32)).astype(BF16)
    return x1, x2, x3


def _ssm_prep_kernel(ar_ref, ai_ref, ldt_ref, br_ref, bi_ref, cr_ref, ci_ref, ep_ref, en_ref, wv_ref,
                     lre_ref, lim_ref, bblk_ref, cblk_ref, wvt_ref):
    wvt_ref[...] = wv_ref[...].T.astype(BF16)

    ar = ar_ref[0]
    ai = ai_ref[0]
    dt = jnp.exp(ldt_ref[0])
    mag = jnp.exp(ar * dt)
    lam_re = mag * jnp.cos(ai * dt)
    lam_im = mag * jnp.sin(ai * dt)
    den = ar * ar + ai * ai
    nr = lam_re - 1.0
    ni = lam_im
    f_re = (nr * ar + ni * ai) / den
    f_im = (ni * ar - nr * ai) / den
    lre_ref[0] = jnp.broadcast_to(lam_re, (BATCH, TILE_ST))
    lim_ref[0] = jnp.broadcast_to(lam_im, (BATCH, TILE_ST))

    ep = ep_ref[...]

    def spread_b(b):
        return sum(lax.dot_general(ep, piece, NT_DIMS, preferred_element_type=F32)
                   for piece in _split3(b))

    pb_re = spread_b(br_ref[0])
    pb_im = spread_b(bi_ref[0])
    row = lax.broadcasted_iota(jnp.int32, (TILE_CH, TILE_ST), 0)
    col = lax.broadcasted_iota(jnp.int32, (TILE_CH, TILE_ST), 1)
    diag = (row // SSM_GROUP) == (col // SSM_STATE)
    bb_re = jnp.where(diag, f_re * pb_re - f_im * pb_im, 0.0).astype(BF16)
    bb_im = jnp.where(diag, f_re * pb_im + f_im * pb_re, 0.0).astype(BF16)
    for c in range(N_CHUNKS):
        bblk_ref[0, :, _re_cols(c)] = bb_re[:, c * LANES:(c + 1) * LANES]
        bblk_ref[0, :, _im_cols(c)] = bb_im[:, c * LANES:(c + 1) * LANES]

    en = en_ref[...]
    row = lax.broadcasted_iota(jnp.int32, (TILE_ST, TILE_CH), 0)
    col = lax.broadcasted_iota(jnp.int32, (TILE_ST, TILE_CH), 1)
    diag = (row // SSM_STATE) == (col // SSM_GROUP)
    pc_re = lax.dot_general(en, cr_ref[0].astype(BF16), NT_DIMS, preferred_element_type=F32)
    pc_im = lax.dot_general(en, ci_ref[0].astype(BF16), NT_DIMS, preferred_element_type=F32)
    cc_re = jnp.where(diag, pc_re, 0.0).astype(BF16)
    cc_im = jnp.where(diag, -pc_im, 0.0).astype(BF16)
    for c in range(N_CHUNKS):
        cblk_ref[0, _re_cols(c), :] = cc_re[c * LANES:(c + 1) * LANES, :]
        cblk_ref[0, _im_cols(c), :] = cc_im[c * LANES:(c + 1) * LANES, :]


def _ssm_prep(a_re, a_im, log_dt, b_re, b_im, c_re, c_im, w_in):
    nt = N_GROUP_TILES
    wv_rows = D_MODEL // nt
    wv_col_block = (D_ATTN + D_KV) // D_KV
    lanes3 = lambda a: a.reshape(nt, 1, TILE_ST)
    ldt = jnp.broadcast_to(log_dt[:, None], (N_SSM_GROUPS, SSM_STATE))
    ep = jnp.asarray(np.tile(np.eye(SSM_GROUP, dtype=np.float32), (GROUPS_PER_TILE, 1)), BF16)
    en = jnp.asarray(np.tile(np.eye(SSM_STATE, dtype=np.float32), (GROUPS_PER_TILE, 1)), BF16)
    t3 = lambda t: (t, 0, 0)
    c2 = lambda t: (0, 0)
    return pl.pallas_call(
        _ssm_prep_kernel,
        grid=(nt,),
        in_specs=[
            pl.BlockSpec((1, 1, TILE_ST), t3),
            pl.BlockSpec((1, 1, TILE_ST), t3),
            pl.BlockSpec((1, 1, TILE_ST), t3),
            pl.BlockSpec((1, TILE_ST, SSM_GROUP), t3),
            pl.BlockSpec((1, TILE_ST, SSM_GROUP), t3),
            pl.BlockSpec((1, TILE_CH, SSM_STATE), t3),
            pl.BlockSpec((1, TILE_CH, SSM_STATE), t3),
            pl.BlockSpec((TILE_CH, SSM_GROUP), c2),
            pl.BlockSpec((TILE_ST, SSM_STATE), c2),
            pl.BlockSpec((wv_rows, D_KV), lambda t: (t, wv_col_block)),
        ],
        out_specs=[
            pl.BlockSpec((1, BATCH, TILE_ST), t3),
            pl.BlockSpec((1, BATCH, TILE_ST), t3),
            pl.BlockSpec((1, TILE_CH, 2 * TILE_ST), t3),
            pl.BlockSpec((1, 2 * TILE_ST, TILE_CH), t3),
            pl.BlockSpec((D_KV, wv_rows), lambda t: (0, t)),
        ],
        out_shape=[
            jax.ShapeDtypeStruct((nt, BATCH, TILE_ST), F32),
            jax.ShapeDtypeStruct((nt, BATCH, TILE_ST), F32),
            jax.ShapeDtypeStruct((nt, TILE_CH, 2 * TILE_ST), BF16),
            jax.ShapeDtypeStruct((nt, 2 * TILE_ST, TILE_CH), BF16),
            jax.ShapeDtypeStruct((D_KV, D_MODEL), BF16),
        ],
        compiler_params=pltpu.CompilerParams(dimension_semantics=("parallel",)),
        name="ssm_prep",
    )(lanes3(a_re), lanes3(a_im), lanes3(ldt),
      b_re.reshape(nt, TILE_ST, SSM_GROUP), b_im.reshape(nt, TILE_ST, SSM_GROUP),
      c_re.reshape(nt, TILE_CH, SSM_STATE), c_im.reshape(nt, TILE_CH, SSM_STATE), ep, en, w_in)


def _inproj_kernel(x_ref, pos_ref, g_ref, invf_ref, sgn_ref, w_ref, wvt_ref, wo_ref, wglu_ref,
                   q_ref, k_ref, vt_ref, u_ref, wo_bf_ref, wglu_bf_ref):
    tm = x_ref.shape[0]
    wo_bf_ref[...] = wo_ref[...].astype(BF16)
    wglu_bf_ref[...] = wglu_ref[...].astype(BF16)

    lane = lax.broadcasted_iota(jnp.int32, (ROW_SUB, LANES), 1)
    first_half = (lane & (HEAD_DIM // 2)) == 0
    first_head = lane < HEAD_DIM
    scale = 1.0 / math.sqrt(HEAD_DIM)
    for r in range(tm // ROW_SUB):
        rs = slice(r * ROW_SUB, (r + 1) * ROW_SUB)
        xn = _rms(x_ref[rs, :], g_ref[...]).astype(BF16)
        ang = pos_ref[rs, :].astype(F32) * invf_ref[...]
        cos = jnp.cos(ang)
        sin = jnp.sin(ang) * sgn_ref[...]

        def rotary(t):
            partner = jnp.where(first_half,
                                pltpu.roll(t, LANES - HEAD_DIM // 2, 1),
                                pltpu.roll(t, HEAD_DIM // 2, 1))
            return t * cos + partner * sin

        def store_dup(ref, c, t):
            swapped = pltpu.roll(t, HEAD_DIM, 1)
            ref[rs, (2 * c) * LANES:(2 * c + 1) * LANES] = jnp.where(first_head, t, swapped).astype(BF16)
            ref[rs, (2 * c + 1) * LANES:(2 * c + 2) * LANES] = jnp.where(first_head, swapped, t).astype(BF16)

        for j in range(D_ATTN // MXU_COLS):
            p = jnp.dot(xn, w_ref[:, j * MXU_COLS:(j + 1) * MXU_COLS], preferred_element_type=F32)
            for c in range(MXU_COLS // LANES):
                col = j * MXU_COLS + c * LANES
                q_ref[rs, col:col + LANES] = (rotary(p[:, c * LANES:(c + 1) * LANES]) * scale).astype(BF16)
        p = jnp.dot(xn, w_ref[:, D_ATTN:D_ATTN + D_KV], preferred_element_type=F32)
        for c in range(D_KV // LANES):
            store_dup(k_ref, c, rotary(p[:, c * LANES:(c + 1) * LANES]))
        vt_ref[:, rs] = lax.dot_general(wvt_ref[...], xn, NT_DIMS,
                                        preferred_element_type=F32).astype(BF16)
        for j in range(D_SSM // MXU_COLS):
            col = D_ATTN + 2 * D_KV + j * MXU_COLS
            u_ref[rs, j * MXU_COLS:(j + 1) * MXU_COLS] = jnp.dot(xn, w_ref[:, col:col + MXU_COLS],
                                                                 preferred_element_type=F32)


def _in_proj(x2, pos2, g_pre_mix, w_in_bf, wvt_bf, w_o, w_glu):
    T = x2.shape[0]
    tm = TILES.in_proj_rows
    nt = SEQ // tm
    n_steps = BATCH * nt
    wo_slab = (D_MODEL // n_steps, D_MODEL)
    wglu_slab = (D_SSM // n_steps, D_SSM)
    half = HEAD_DIM // 2
    invf = ROPE_THETA ** (-np.arange(half, dtype=np.float32) / half)
    invf = np.tile(invf.astype(np.float32), LANES // half)[None, :]
    sgn = np.tile(np.concatenate([-np.ones(half, np.float32), np.ones(half, np.float32)]),
                  LANES // HEAD_DIM)[None, :]
    row = lambda b, i: (b * nt + i, 0)
    const = lambda b, i: (0, 0)
    return pl.pallas_call(
        _inproj_kernel,
        grid=(BATCH, nt),
        in_specs=[
            pl.BlockSpec((tm, D_MODEL), row),
            pl.BlockSpec((tm, 1), row),
            pl.BlockSpec((1, D_MODEL), const),
            pl.BlockSpec((1, LANES), const),
            pl.BlockSpec((1, LANES), const),
            pl.BlockSpec((D_MODEL, D_IN), const, pipeline_mode=pl.Buffered(1)),
            pl.BlockSpec((D_KV, D_MODEL), const, pipeline_mode=pl.Buffered(1)),
            pl.BlockSpec(wo_slab, row),
            pl.BlockSpec(wglu_slab, row),
        ],
        out_specs=[
            pl.BlockSpec((tm, D_ATTN), row),
            pl.BlockSpec((tm, D_KV_DUP), row),
            pl.BlockSpec((D_KV, tm), lambda b, i: (0, b * nt + i)),
            pl.BlockSpec((tm, D_SSM), lambda b, i: (i, b)),
            pl.BlockSpec(wo_slab, row),
            pl.BlockSpec(wglu_slab, row),
        ],
        out_shape=[
            jax.ShapeDtypeStruct((T, D_ATTN), BF16),
            jax.ShapeDtypeStruct((T, D_KV_DUP), BF16),
            jax.ShapeDtypeStruct((D_KV, T), BF16),
            jax.ShapeDtypeStruct((SEQ, BATCH * D_SSM), F32),
            jax.ShapeDtypeStruct((D_MODEL, D_MODEL), BF16),
            jax.ShapeDtypeStruct((D_SSM, D_SSM), BF16),
        ],
        compiler_params=pltpu.CompilerParams(
            dimension_semantics=("parallel", "parallel"), vmem_limit_bytes=TILES.vmem_limit),
        name="in_proj",
    )(x2, pos2, g_pre_mix, jnp.asarray(invf), jnp.asarray(sgn), w_in_bf, wvt_bf, w_o, w_glu)


def _attn_kernel(sinks_ref, q_ref, k_ref, vt_ref, kh_ref, vth_ref, g_ref, o_ref, acc_ref):
    tq = q_ref.shape[0]
    first_tile = pl.program_id(1) == 0
    two = 2 * WINDOW
    key = lax.broadcasted_iota(jnp.int32, (WINDOW, two), 0)
    qry = lax.broadcasted_iota(jnp.int32, (WINDOW, two), 1) % WINDOW
    use_prev = key > qry
    pair0 = lax.broadcasted_iota(jnp.int32, (1, two), 1) < WINDOW
    low_lanes = lax.broadcasted_iota(jnp.int32, (two, LANES), 1) < HEAD_DIM
    zero = jnp.zeros((), BF16)
    no_values = jnp.zeros((HEAD_DIM, two), BF16)

    def keys_of(j):
        lo, hi = j * WINDOW, (j + 1) * WINDOW
        if j == 0:
            return (jnp.concatenate([kh_ref[...], k_ref[lo:hi, :]], axis=0),
                    jnp.concatenate([vth_ref[...], vt_ref[:, lo:hi]], axis=1))
        return k_ref[lo - WINDOW:hi, :], vt_ref[:, lo - WINDOW:hi]

    def scores(j, kv, side):
        lo, hi = j * WINDOW, (j + 1) * WINDOW
        c0 = kv * Q_PER_KV * HEAD_DIM
        kd = keys_of(j)[0][:, kv * LANES:(kv + 1) * LANES]
        ks = jnp.where(low_lanes if side == 0 else jnp.logical_not(low_lanes), kd, zero)
        q2 = jnp.concatenate([q_ref[lo:hi, c0:c0 + LANES], q_ref[lo:hi, c0 + LANES:c0 + 2 * LANES]],
                             axis=0)
        return lax.dot_general(ks, q2, NT_DIMS, preferred_element_type=F32)

    def probs(j, kv, side, st):
        s_prev = st[:WINDOW, :]
        if j == 0:
            s_prev = jnp.where(first_tile, -jnp.inf, s_prev)
        sf = jnp.where(use_prev, s_prev, st[WINDOW:, :])
        sink = jnp.where(pair0, sinks_ref[kv * Q_PER_KV + side], sinks_ref[kv * Q_PER_KV + 2 + side])
        m = jnp.maximum(jnp.max(sf, axis=0, keepdims=True), sink)
        p = jnp.exp(sf - m)
        den = jnp.sum(p, axis=0, keepdims=True) + jnp.exp(sink - m)
        pn = p * (1.0 / den)
        return jnp.concatenate([jnp.where(use_prev, pn, 0.0), jnp.where(use_prev, 0.0, pn)],
                               axis=0).astype(BF16)

    def weighted_values(j, kv, side, pcat):
        vdt = keys_of(j)[1][kv * HEAD_DIM:(kv + 1) * HEAD_DIM, :]
        vst = jnp.concatenate([vdt, no_values] if side == 0 else [no_values, vdt], axis=0)
        return jnp.dot(vst, pcat, preferred_element_type=F32)

    tasks = [(j, kv, side) for j in range(tq // WINDOW) for kv in range(N_KV_HEADS) for side in range(2)]
    pending = {}
    out = None
    for n in range(len(tasks) + SCORE_AHEAD):
        if n < len(tasks):
            pending[n] = scores(*tasks[n])
        if n < SCORE_AHEAD:
            continue
        j, kv, side = tasks[n - SCORE_AHEAD]
        o = weighted_values(j, kv, side, probs(j, kv, side, pending.pop(n - SCORE_AHEAD)))
        if side == 0:
            out = o
            continue
        out = out + o
        c0 = kv * Q_PER_KV * HEAD_DIM
        acc_ref[c0:c0 + LANES, :] = out[:, :WINDOW]
        acc_ref[c0 + LANES:c0 + 2 * LANES, :] = out[:, WINDOW:]
        if kv == N_KV_HEADS - 1:
            a = acc_ref[...]
            ms = jnp.mean(a * a, axis=0, keepdims=True)
            o_ref[j * WINDOW:(j + 1) * WINDOW, :] = (
                (a * lax.rsqrt(ms + RMS_EPS)).T * g_ref[...]).astype(BF16)


def _attention(q, k, vt, sinks, g_attn_out, batches):
    tq = TILES.attention_rows
    nt = SEQ // tq
    per = tq // WINDOW
    b0, b1 = batches
    row = lambda b, i: ((b0 + b) * nt + i, 0)
    col = lambda b, i: (0, (b0 + b) * nt + i)
    halo = lambda b, i: jnp.maximum((b0 + b) * (SEQ // WINDOW) + i * per - 1, 0)
    return pl.pallas_call(
        _attn_kernel,
        grid=(b1 - b0, nt),
        in_specs=[
            pl.BlockSpec(memory_space=pltpu.SMEM),
            pl.BlockSpec((tq, D_ATTN), row),
            pl.BlockSpec((tq, D_KV_DUP), row),
            pl.BlockSpec((D_KV, tq), col),
            pl.BlockSpec((WINDOW, D_KV_DUP), lambda b, i: (halo(b, i), 0)),
            pl.BlockSpec((D_KV, WINDOW), lambda b, i: (0, halo(b, i))),
            pl.BlockSpec((1, D_ATTN), lambda b, i: (0, 0)),
        ],
        out_specs=pl.BlockSpec((tq, D_ATTN), lambda b, i: (b * nt + i, 0)),
        out_shape=jax.ShapeDtypeStruct(((b1 - b0) * SEQ, D_ATTN), BF16),
        scratch_shapes=[pltpu.VMEM((D_ATTN, WINDOW), F32)],
        compiler_params=pltpu.CompilerParams(dimension_semantics=("parallel", "parallel")),
        name="attention",
    )(sinks, q, k, vt, k, vt, g_attn_out)


def _ssm_kernel(u_ref, bblk_ref, cblk_ref, lre_ref, lim_ref, dskip_ref, wglu_ref, bglu_ref, g_ref,
                wg_ref, wu_ref, wd_ref, after_ref, o_ref, wgu_bf_ref, wd_bf_ref, s_ref, y_ref, carry_ref):
    del after_ref
    tl = u_ref.shape[0]
    rows = tl * BATCH
    for c in range(D_FF // MXU_COLS):
        src = slice(c * MXU_COLS, (c + 1) * MXU_COLS)
        wgu_bf_ref[:, 2 * c * MXU_COLS:(2 * c + 1) * MXU_COLS] = wg_ref[:, src].astype(BF16)
        wgu_bf_ref[:, (2 * c + 1) * MXU_COLS:(2 * c + 2) * MXU_COLS] = wu_ref[:, src].astype(BF16)
    wd_bf_ref[...] = wd_ref[...].astype(BF16)

    @pl.when(pl.program_id(0) == 0)
    def _():
        carry_ref[...] = jnp.zeros_like(carry_ref)

    u = u_ref[...].reshape(rows, D_SSM)
    ub = u.astype(BF16)

    def project_in(gt):
        ch = slice(gt * TILE_CH, (gt + 1) * TILE_CH)
        s_ref[gt % 2] = jnp.dot(ub[:, ch], bblk_ref[gt], preferred_element_type=F32)

    project_in(0)
    for gt in range(N_GROUP_TILES):
        buf = gt % 2
        ch = slice(gt * TILE_CH, (gt + 1) * TILE_CH)
        if gt + 1 < N_GROUP_TILES:
            project_in(gt + 1)
        for c in range(N_CHUNKS):
            lanes = slice(c * LANES, (c + 1) * LANES)
            re_cols, im_cols = _re_cols(c), _im_cols(c)
            lr = lre_ref[gt, :, lanes]
            li = lim_ref[gt, :, lanes]
            sr = carry_ref[gt, :, re_cols]
            si = carry_ref[gt, :, im_cols]
            for t in range(tl):
                r = slice(t * BATCH, (t + 1) * BATCH)
                nr = lr * sr - li * si + s_ref[buf, r, re_cols]
                ni = lr * si + li * sr + s_ref[buf, r, im_cols]
                s_ref[buf, r, re_cols] = nr
                s_ref[buf, r, im_cols] = ni
                sr, si = nr, ni
            carry_ref[gt, :, re_cols] = sr
            carry_ref[gt, :, im_cols] = si
        y = jnp.dot(s_ref[buf].astype(BF16), cblk_ref[gt], preferred_element_type=F32)
        y_ref[:, ch] = y + dskip_ref[:, ch] * u[:, ch]
    z = jax.nn.gelu(y_ref[...])
    gate = jax.nn.sigmoid(jnp.dot(z.astype(BF16), wglu_ref[...], preferred_element_type=F32)
                          + bglu_ref[...])
    o_ref[...] = _rms(z * gate, g_ref[...]).reshape(tl, BATCH, D_SSM)


def _ssm(u3, bblk, cblk, lam_re, lam_im, d_skip, w_glu_bf, b_glu, g_ssm_out, w_gate, w_up, w_down,
         run_after):
    tl = TILES.ssm_steps
    rows = tl * BATCH
    c2 = lambda i: (0, 0)
    c3 = lambda i: (0, 0, 0)
    slab = lambda i: (i, 0)
    up_slab = (D_MODEL // N_CAST_STEPS, D_FF)
    down_slab = (D_FF // N_CAST_STEPS, D_MODEL)
    return pl.pallas_call(
        _ssm_kernel,
        grid=(N_CAST_STEPS,),
        in_specs=[
            pl.BlockSpec((tl, BATCH, D_SSM), lambda i: (i, 0, 0)),
            pl.BlockSpec((N_GROUP_TILES, TILE_CH, 2 * TILE_ST), c3),
            pl.BlockSpec((N_GROUP_TILES, 2 * TILE_ST, TILE_CH), c3),
            pl.BlockSpec((N_GROUP_TILES, BATCH, TILE_ST), c3),
            pl.BlockSpec((N_GROUP_TILES, BATCH, TILE_ST), c3),
            pl.BlockSpec((1, D_SSM), c2),
            pl.BlockSpec((D_SSM, D_SSM), c2),
            pl.BlockSpec((1, D_SSM), c2),
            pl.BlockSpec((1, D_SSM), c2),
            pl.BlockSpec(up_slab, slab),
            pl.BlockSpec(up_slab, slab),
            pl.BlockSpec(down_slab, slab),
            pl.BlockSpec(memory_space=pl.ANY),
        ],
        out_specs=[
            pl.BlockSpec((tl, BATCH, D_SSM), lambda i: (i, 0, 0)),
            pl.BlockSpec((up_slab[0], 2 * D_FF), slab),
            pl.BlockSpec(down_slab, slab),
        ],
        out_shape=[
            jax.ShapeDtypeStruct((SEQ, BATCH, D_SSM), F32),
            jax.ShapeDtypeStruct((D_MODEL, 2 * D_FF), BF16),
            jax.ShapeDtypeStruct((D_FF, D_MODEL), BF16),
        ],
        scratch_shapes=[
            pltpu.VMEM((2, rows, 2 * TILE_ST), F32),
            pltpu.VMEM((rows, D_SSM), F32),
            pltpu.VMEM((N_GROUP_TILES, BATCH, 2 * TILE_ST), F32),
        ],
        compiler_params=pltpu.CompilerParams(
            dimension_semantics=("arbitrary",), vmem_limit_bytes=TILES.vmem_limit),
        name="ssm",
    )(u3, bblk, cblk, lam_re, lam_im, d_skip, w_glu_bf, b_glu, g_ssm_out, w_gate, w_up, w_down, run_after)


def _outproj_kernel(attn_lo_ref, attn_hi_ref, ssm_ref, x_ref, wo_ref, gpost_ref, gpre_ref, h_ref, hn_ref):
    tm = x_ref.shape[0]
    low_batches = pl.program_id(0) < BATCH // 2
    for r in range(tm // ROW_SUB):
        rs = slice(r * ROW_SUB, (r + 1) * ROW_SUB)
        attn = jnp.where(low_batches, attn_lo_ref[rs, :], attn_hi_ref[rs, :])
        mix = jnp.dot(attn, wo_ref[:D_ATTN, :], preferred_element_type=F32)
        mix = mix + jnp.dot(ssm_ref[rs, :].astype(BF16), wo_ref[D_ATTN:, :],
                            preferred_element_type=F32)
        h = x_ref[rs, :] + _rms(mix, gpost_ref[...])
        h_ref[rs, :] = h
        hn_ref[rs, :] = _rms(h, gpre_ref[...]).astype(BF16)


def _out_proj(attn_lo, attn_hi, ssm2, x2, w_o_bf, g_post_mix, g_pre_ffn):
    T = x2.shape[0]
    tm = TILES.out_proj_rows
    nt = SEQ // tm
    half = BATCH // 2
    row = lambda b, i: (b * nt + i, 0)
    const = lambda b, i: (0, 0)
    lo_row = lambda b, i: (jnp.where(b < half, b * nt + i, half * nt - 1), 0)
    hi_row = lambda b, i: (jnp.where(b < half, 0, (b - half) * nt + i), 0)
    return pl.pallas_call(
        _outproj_kernel,
        grid=(BATCH, nt),
        in_specs=[
            pl.BlockSpec((tm, D_ATTN), lo_row),
            pl.BlockSpec((tm, D_ATTN), hi_row),
            pl.BlockSpec((tm, D_SSM), lambda b, i: (i, b)),
            pl.BlockSpec((tm, D_MODEL), row),
            pl.BlockSpec((D_MODEL, D_MODEL), const),
            pl.BlockSpec((1, D_MODEL), const),
            pl.BlockSpec((1, D_MODEL), const),
        ],
        out_specs=[pl.BlockSpec((tm, D_MODEL), row), pl.BlockSpec((tm, D_MODEL), row)],
        out_shape=[jax.ShapeDtypeStruct((T, D_MODEL), F32), jax.ShapeDtypeStruct((T, D_MODEL), BF16)],
        compiler_params=pltpu.CompilerParams(
            dimension_semantics=("parallel", "parallel"), vmem_limit_bytes=TILES.vmem_limit),
        name="out_proj",
    )(attn_lo, attn_hi, ssm2, x2, w_o_bf, g_post_mix, g_pre_ffn)


def _ffn_up_kernel(hn_ref, wgu_ref, hid_ref):
    for r in range(hn_ref.shape[0] // ROW_SUB):
        rs = slice(r * ROW_SUB, (r + 1) * ROW_SUB)
        gu = jnp.dot(hn_ref[rs, :], wgu_ref[...], preferred_element_type=F32)
        hid_ref[rs, :] = jnp.concatenate(
            [jax.nn.silu(gu[:, 2 * c * MXU_COLS:(2 * c + 1) * MXU_COLS])
             * gu[:, (2 * c + 1) * MXU_COLS:(2 * c + 2) * MXU_COLS]
             for c in range(gu.shape[1] // (2 * MXU_COLS))], axis=1).astype(BF16)


def _ffn_down_kernel(hid_ref, wd_ref, h_ref, g_ref, o_ref):
    for r in range(hid_ref.shape[0] // ROW_SUB):
        rs = slice(r * ROW_SUB, (r + 1) * ROW_SUB)
        ff = jnp.dot(hid_ref[rs, :], wd_ref[...], preferred_element_type=F32)
        o_ref[rs, :] = h_ref[rs, :] + _rms(ff, g_ref[...])


def _ffn(hn, h, wgu_bf, wd_bf, g_post_ffn):
    T = h.shape[0]
    tm, tf, td = TILES.ffn_rows, TILES.ffn_cols, TILES.ffn_down_rows
    hid = pl.pallas_call(
        _ffn_up_kernel,
        grid=(D_FF // tf, T // tm),
        in_specs=[
            pl.BlockSpec((tm, D_MODEL), lambda f, i: (i, 0)),
            pl.BlockSpec((D_MODEL, 2 * tf), lambda f, i: (0, f)),
        ],
        out_specs=pl.BlockSpec((tm, tf), lambda f, i: (i, f)),
        out_shape=jax.ShapeDtypeStruct((T, D_FF), BF16),
        compiler_params=pltpu.CompilerParams(
            dimension_semantics=("parallel", "parallel"), vmem_limit_bytes=TILES.vmem_limit),
        name="ffn_up",
    )(hn, wgu_bf)
    return pl.pallas_call(
        _ffn_down_kernel,
        grid=(T // td,),
        in_specs=[
            pl.BlockSpec((td, D_FF), lambda i: (i, 0)),
            pl.BlockSpec((D_FF, D_MODEL), lambda i: (0, 0), pipeline_mode=pl.Buffered(1)),
            pl.BlockSpec((td, D_MODEL), lambda i: (i, 0)),
            pl.BlockSpec((1, D_MODEL), lambda i: (0, 0)),
        ],
        out_specs=pl.BlockSpec((td, D_MODEL), lambda i: (i, 0)),
        out_shape=jax.ShapeDtypeStruct((T, D_MODEL), F32),
        compiler_params=pltpu.CompilerParams(
            dimension_semantics=("parallel",), vmem_limit_bytes=TILES.ffn_vmem_limit),
        name="ffn_down",
    )(hid, wd_bf, h, g_post_ffn)


def kernel(x, positions, g_pre_mix, w_in, sinks, a_re, a_im, log_dt, b_re, b_im, c_re, c_im, d_skip,
           w_glu, b_glu, g_attn_out, g_ssm_out, w_o, g_post_mix, g_pre_ffn, w_gate, w_up, w_down,
           g_post_ffn):
    depth = w_in.shape[0]
    T = BATCH * SEQ
    h = x.reshape(T, D_MODEL)
    pos2 = positions.reshape(T, 1)
    for i in range(depth):
        lam_re, lam_im, bblk, cblk, wvt_bf = _ssm_prep(a_re[i], a_im[i], log_dt[i], b_re[i], b_im[i],
                                                       c_re[i], c_im[i], w_in[i])
        w_in_bf = w_in[i].astype(BF16)
        q, k, vt, u, w_o_bf, w_glu_bf = _in_proj(h, pos2, g_pre_mix[i][None, :], w_in_bf, wvt_bf,
                                                 w_o[i], w_glu[i])
        half = BATCH // 2
        attn_lo = _attention(q, k, vt, sinks[i], g_attn_out[i][None, :], batches=(0, half))
        ssm_n, wgu_bf, wd_bf = _ssm(u.reshape(SEQ, BATCH, D_SSM), bblk, cblk, lam_re, lam_im,
                                    d_skip[i].reshape(1, D_SSM), w_glu_bf, b_glu[i][None, :],
                                    g_ssm_out[i][None, :], w_gate[i], w_up[i], w_down[i],
                                    run_after=attn_lo)
        attn_hi = _attention(q, k, vt, sinks[i], g_attn_out[i][None, :], batches=(half, BATCH))
        h, hn = _out_proj(attn_lo, attn_hi, ssm_n.reshape(SEQ, BATCH * D_SSM), h, w_o_bf,
                          g_post_mix[i][None, :], g_pre_ffn[i][None, :])
        h = _ffn(hn, h, wgu_bf, wd_bf, g_post_ffn[i][None, :])
    return h.reshape(BATCH, SEQ, D_MODEL)
```

```python
import math
from typing import NamedTuple

import jax
import jax.numpy as jnp
import numpy as np
from jax import lax
from jax.experimental import pallas as pl
from jax.experimental.pallas import tpu as pltpu

D_MODEL = 2048
BATCH = 8
SEQ = 2048
HEAD_DIM = 64
D_ATTN = 1024
N_KV_HEADS = 4
Q_PER_KV = 4
D_KV = 256
WINDOW = 128
ROPE_THETA = 10000.0
D_SSM = 1024
SSM_GROUP = 16
N_SSM_GROUPS = 64
SSM_STATE = 64
D_IN = D_ATTN + 2 * D_KV + D_SSM
D_FF = 5632
RMS_EPS = 1e-6

LANES = 128
MXU_COLS = 256
VMEM_BYTES = 64 * 2**20
GROUPS_PER_TILE = 16
N_GROUP_TILES = N_SSM_GROUPS // GROUPS_PER_TILE
TILE_CH = GROUPS_PER_TILE * SSM_GROUP
TILE_ST = GROUPS_PER_TILE * SSM_STATE
N_CHUNKS = TILE_ST // LANES
D_KV_DUP = 2 * N_KV_HEADS * HEAD_DIM
ROW_SUB = 256
SCORE_AHEAD = 8


class _Tiles(NamedTuple):
    in_proj_rows: int = 1024
    attention_rows: int = 1024
    ssm_steps: int = 64
    ssm_glu_steps: int = 128
    out_proj_rows: int = 512
    ffn_rows: int = 4096
    ffn_cols: int = 512
    ffn_down_rows: int = 512
    vmem_limit: int = VMEM_BYTES * 13 // 16
    ffn_vmem_limit: int = VMEM_BYTES * 29 // 32


TILES = _Tiles()
N_CAST_STEPS = SEQ // TILES.ssm_steps

F32 = jnp.float32
BF16 = jnp.bfloat16
NT_DIMS = (((1,), (1,)), ((), ()))


def _re_cols(c):
    return slice(2 * c * LANES, (2 * c + 1) * LANES)


def _im_cols(c):
    return slice((2 * c + 1) * LANES, (2 * c + 2) * LANES)


def _rms(x, g):
    ms = jnp.mean(x * x, axis=-1, keepdims=True)
    return x * lax.rsqrt(ms + RMS_EPS) * g


def _split3(x):
    x1 = x.astype(BF16)
    r1 = x - x1.astype(F32)
    x2 = r1.astype(BF16)
    x3 = (r1 - x2.astype(F32)).astype(BF16)
    return x1, x2, x3


def _ssm_prep_kernel(ar_ref, ai_ref, ldt_ref, br_ref, bi_ref, cr_ref, ci_ref, ep_ref, en_ref, wv_ref,
                     lre_ref, lim_ref, bblk_ref, cblk_ref, wvt_ref):
    wvt_ref[...] = wv_ref[...].T.astype(BF16)

    ar = ar_ref[0]
    ai = ai_ref[0]
    dt = jnp.exp(ldt_ref[0])
    mag = jnp.exp(ar * dt)
    lam_re = mag * jnp.cos(ai * dt)
    lam_im = mag * jnp.sin(ai * dt)
    den = ar * ar + ai * ai
    nr = lam_re - 1.0
    ni = lam_im
    f_re = (nr * ar + ni * ai) / den
    f_im = (ni * ar - nr * ai) / den
    lre_ref[0] = jnp.broadcast_to(lam_re, (BATCH, TILE_ST))
    lim_ref[0] = jnp.broadcast_to(lam_im, (BATCH, TILE_ST))

    ep = ep_ref[...]

    def spread_b(b):
        return sum(lax.dot_general(ep, piece, NT_DIMS, preferred_element_type=F32)
                   for piece in _split3(b))

    pb_re = spread_b(br_ref[0])
    pb_im = spread_b(bi_ref[0])
    row = lax.broadcasted_iota(jnp.int32, (TILE_CH, TILE_ST), 0)
    col = lax.broadcasted_iota(jnp.int32, (TILE_CH, TILE_ST), 1)
    diag = (row // SSM_GROUP) == (col // SSM_STATE)
    bb_re = jnp.where(diag, f_re * pb_re - f_im * pb_im, 0.0).astype(BF16)
    bb_im = jnp.where(diag, f_re * pb_im + f_im * pb_re, 0.0).astype(BF16)
    for c in range(N_CHUNKS):
        bblk_ref[0, :, _re_cols(c)] = bb_re[:, c * LANES:(c + 1) * LANES]
        bblk_ref[0, :, _im_cols(c)] = bb_im[:, c * LANES:(c + 1) * LANES]

    en = en_ref[...]
    row = lax.broadcasted_iota(jnp.int32, (TILE_ST, TILE_CH), 0)
    col = lax.broadcasted_iota(jnp.int32, (TILE_ST, TILE_CH), 1)
    diag = (row // SSM_STATE) == (col // SSM_GROUP)
    pc_re = lax.dot_general(en, cr_ref[0].astype(BF16), NT_DIMS, preferred_element_type=F32)
    pc_im = lax.dot_general(en, ci_ref[0].astype(BF16), NT_DIMS, preferred_element_type=F32)
    cc_re = jnp.where(diag, pc_re, 0.0).astype(BF16)
    cc_im = jnp.where(diag, -pc_im, 0.0).astype(BF16)
    for c in range(N_CHUNKS):
        cblk_ref[0, _re_cols(c), :] = cc_re[c * LANES:(c + 1) * LANES, :]
        cblk_ref[0, _im_cols(c), :] = cc_im[c * LANES:(c + 1) * LANES, :]


def _ssm_prep(a_re, a_im, log_dt, b_re, b_im, c_re, c_im, w_in):
    nt = N_GROUP_TILES
    wv_rows = D_MODEL // nt
    wv_col_block = (D_ATTN + D_KV) // D_KV
    lanes3 = lambda a: a.reshape(nt, 1, TILE_ST)
    ldt = jnp.broadcast_to(log_dt[:, None], (N_SSM_GROUPS, SSM_STATE))
    ep = jnp.asarray(np.tile(np.eye(SSM_GROUP, dtype=np.float32), (GROUPS_PER_TILE, 1)), BF16)
    en = jnp.asarray(np.tile(np.eye(SSM_STATE, dtype=np.float32), (GROUPS_PER_TILE, 1)), BF16)
    t3 = lambda t: (t, 0, 0)
    c2 = lambda t: (0, 0)
    return pl.pallas_call(
        _ssm_prep_kernel,
        grid=(nt,),
        in_specs=[
            pl.BlockSpec((1, 1, TILE_ST), t3),
            pl.BlockSpec((1, 1, TILE_ST), t3),
            pl.BlockSpec((1, 1, TILE_ST), t3),
            pl.BlockSpec((1, TILE_ST, SSM_GROUP), t3),
            pl.BlockSpec((1, TILE_ST, SSM_GROUP), t3),
            pl.BlockSpec((1, TILE_CH, SSM_STATE), t3),
            pl.BlockSpec((1, TILE_CH, SSM_STATE), t3),
            pl.BlockSpec((TILE_CH, SSM_GROUP), c2),
            pl.BlockSpec((TILE_ST, SSM_STATE), c2),
            pl.BlockSpec((wv_rows, D_KV), lambda t: (t, wv_col_block)),
        ],
        out_specs=[
            pl.BlockSpec((1, BATCH, TILE_ST), t3),
            pl.BlockSpec((1, BATCH, TILE_ST), t3),
            pl.BlockSpec((1, TILE_CH, 2 * TILE_ST), t3),
            pl.BlockSpec((1, 2 * TILE_ST, TILE_CH), t3),
            pl.BlockSpec((D_KV, wv_rows), lambda t: (0, t)),
        ],
        out_shape=[
            jax.ShapeDtypeStruct((nt, BATCH, TILE_ST), F32),
            jax.ShapeDtypeStruct((nt, BATCH, TILE_ST), F32),
            jax.ShapeDtypeStruct((nt, TILE_CH, 2 * TILE_ST), BF16),
            jax.ShapeDtypeStruct((nt, 2 * TILE_ST, TILE_CH), BF16),
            jax.ShapeDtypeStruct((D_KV, D_MODEL), BF16),
        ],
        compiler_params=pltpu.CompilerParams(dimension_semantics=("parallel",)),
        name="ssm_prep",
    )(lanes3(a_re), lanes3(a_im), lanes3(ldt),
      b_re.reshape(nt, TILE_ST, SSM_GROUP), b_im.reshape(nt, TILE_ST, SSM_GROUP),
      c_re.reshape(nt, TILE_CH, SSM_STATE), c_im.reshape(nt, TILE_CH, SSM_STATE), ep, en, w_in)


def _inproj_kernel(x_ref, pos_ref, g_ref, invf_ref, sgn_ref, w_ref, wvt_ref, wo_ref, wglu_ref,
                   q_ref, k_ref, vt_ref, u_ref, wo_bf_ref, wglu_bf_ref):
    tm = x_ref.shape[0]
    wo_bf_ref[...] = wo_ref[...].astype(BF16)
    wglu_bf_ref[...] = wglu_ref[...].astype(BF16)

    lane = lax.broadcasted_iota(jnp.int32, (ROW_SUB, LANES), 1)
    first_half = (lane & (HEAD_DIM // 2)) == 0
    first_head = lane < HEAD_DIM
    scale = 1.0 / math.sqrt(HEAD_DIM)
    for r in range(tm // ROW_SUB):
        rs = slice(r * ROW_SUB, (r + 1) * ROW_SUB)
        xn = _rms(x_ref[rs, :], g_ref[...]).astype(BF16)
        ang = pos_ref[rs, :].astype(F32) * invf_ref[...]
        cos = jnp.cos(ang)
        sin = jnp.sin(ang) * sgn_ref[...]

        def rotary(t):
            partner = jnp.where(first_half,
                                pltpu.roll(t, LANES - HEAD_DIM // 2, 1),
                                pltpu.roll(t, HEAD_DIM // 2, 1))
            return t * cos + partner * sin

        def store_dup(ref, c, t):
            swapped = pltpu.roll(t, HEAD_DIM, 1)
            ref[rs, (2 * c) * LANES:(2 * c + 1) * LANES] = jnp.where(first_head, t, swapped).astype(BF16)
            ref[rs, (2 * c + 1) * LANES:(2 * c + 2) * LANES] = jnp.where(first_head, swapped, t).astype(BF16)

        for j in range(D_ATTN // MXU_COLS):
            p = jnp.dot(xn, w_ref[:, j * MXU_COLS:(j + 1) * MXU_COLS], preferred_element_type=F32)
            for c in range(MXU_COLS // LANES):
                col = j * MXU_COLS + c * LANES
                q_ref[rs, col:col + LANES] = (rotary(p[:, c * LANES:(c + 1) * LANES]) * scale).astype(BF16)
        p = jnp.dot(xn, w_ref[:, D_ATTN:D_ATTN + D_KV], preferred_element_type=F32)
        for c in range(D_KV // LANES):
            store_dup(k_ref, c, rotary(p[:, c * LANES:(c + 1) * LANES]))
        vt_ref[:, rs] = lax.dot_general(wvt_ref[...], xn, NT_DIMS,
                                        preferred_element_type=F32).astype(BF16)
        for j in range(D_SSM // MXU_COLS):
            col = D_ATTN + 2 * D_KV + j * MXU_COLS
            u_ref[rs, j * MXU_COLS:(j + 1) * MXU_COLS] = jnp.dot(xn, w_ref[:, col:col + MXU_COLS],
                                                                 preferred_element_type=F32)


def _in_proj(x2, pos2, g_pre_mix, w_in_bf, wvt_bf, w_o, w_glu):
    T = x2.shape[0]
    tm = TILES.in_proj_rows
    nt = SEQ // tm
    n_steps = BATCH * nt
    wo_slab = (D_MODEL // n_steps, D_MODEL)
    wglu_slab = (D_SSM // n_steps, D_SSM)
    half = HEAD_DIM // 2
    invf = ROPE_THETA ** (-np.arange(half, dtype=np.float32) / half)
    invf = np.tile(invf.astype(np.float32), LANES // half)[None, :]
    sgn = np.tile(np.concatenate([-np.ones(half, np.float32), np.ones(half, np.float32)]),
                  LANES // HEAD_DIM)[None, :]
    row = lambda b, i: (b * nt + i, 0)
    const = lambda b, i: (0, 0)
    return pl.pallas_call(
        _inproj_kernel,
        grid=(BATCH, nt),
        in_specs=[
            pl.BlockSpec((tm, D_MODEL), row),
            pl.BlockSpec((tm, 1), row),
            pl.BlockSpec((1, D_MODEL), const),
            pl.BlockSpec((1, LANES), const),
            pl.BlockSpec((1, LANES), const),
            pl.BlockSpec((D_MODEL, D_IN), const, pipeline_mode=pl.Buffered(1)),
            pl.BlockSpec((D_KV, D_MODEL), const, pipeline_mode=pl.Buffered(1)),
            pl.BlockSpec(wo_slab, row),
            pl.BlockSpec(wglu_slab, row),
        ],
        out_specs=[
            pl.BlockSpec((tm, D_ATTN), row),
            pl.BlockSpec((tm, D_KV_DUP), row),
            pl.BlockSpec((D_KV, tm), lambda b, i: (0, b * nt + i)),
            pl.BlockSpec((tm, D_SSM), lambda b, i: (i, b)),
            pl.BlockSpec(wo_slab, row),
            pl.BlockSpec(wglu_slab, row),
        ],
        out_shape=[
            jax.ShapeDtypeStruct((T, D_ATTN), BF16),
            jax.ShapeDtypeStruct((T, D_KV_DUP), BF16),
            jax.ShapeDtypeStruct((D_KV, T), BF16),
            jax.ShapeDtypeStruct((SEQ, BATCH * D_SSM), F32),
            jax.ShapeDtypeStruct((D_MODEL, D_MODEL), BF16),
            jax.ShapeDtypeStruct((D_SSM, D_SSM), BF16),
        ],
        compiler_params=pltpu.CompilerParams(
            dimension_semantics=("parallel", "parallel"), vmem_limit_bytes=TILES.vmem_limit),
        name="in_proj",
    )(x2, pos2, g_pre_mix, jnp.asarray(invf), jnp.asarray(sgn), w_in_bf, wvt_bf, w_o, w_glu)


def _attn_kernel(sinks_ref, q_ref, k_ref, vt_ref, kh_ref, vth_ref, g_ref, o_ref, acc_ref):
    tq = q_ref.shape[0]
    first_tile = pl.program_id(1) == 0
    two = 2 * WINDOW
    key = lax.broadcasted_iota(jnp.int32, (WINDOW, two), 0)
    qry = lax.broadcasted_iota(jnp.int32, (WINDOW, two), 1) % WINDOW
    use_prev = key > qry
    pair0 = lax.broadcasted_iota(jnp.int32, (1, two), 1) < WINDOW
    low_lanes = lax.broadcasted_iota(jnp.int32, (two, LANES), 1) < HEAD_DIM
    zero = jnp.zeros((), BF16)
    no_values = jnp.zeros((HEAD_DIM, two), BF16)

    def keys_of(j):
        lo, hi = j * WINDOW, (j + 1) * WINDOW
        if j == 0:
            return (jnp.concatenate([kh_ref[...], k_ref[lo:hi, :]], axis=0),
                    jnp.concatenate([vth_ref[...], vt_ref[:, lo:hi]], axis=1))
        return k_ref[lo - WINDOW:hi, :], vt_ref[:, lo - WINDOW:hi]

    def scores(j, kv, side):
        lo, hi = j * WINDOW, (j + 1) * WINDOW
        c0 = kv * Q_PER_KV * HEAD_DIM
        kd = keys_of(j)[0][:, kv * LANES:(kv + 1) * LANES]
        ks = jnp.where(low_lanes if side == 0 else jnp.logical_not(low_lanes), kd, zero)
        q2 = jnp.concatenate([q_ref[lo:hi, c0:c0 + LANES], q_ref[lo:hi, c0 + LANES:c0 + 2 * LANES]],
                             axis=0)
        return lax.dot_general(ks, q2, NT_DIMS, preferred_element_type=F32)

    def probs(j, kv, side, st):
        s_prev = st[:WINDOW, :]
        if j == 0:
            s_prev = jnp.where(first_tile, -jnp.inf, s_prev)
        sf = jnp.where(use_prev, s_prev, st[WINDOW:, :])
        sink = jnp.where(pair0, sinks_ref[kv * Q_PER_KV + side], sinks_ref[kv * Q_PER_KV + 2 + side])
        m = jnp.maximum(jnp.max(sf, axis=0, keepdims=True), sink)
        p = jnp.exp(sf - m)
        den = jnp.sum(p, axis=0, keepdims=True) + jnp.exp(sink - m)
        pn = p * (1.0 / den)
        return jnp.concatenate([jnp.where(use_prev, pn, 0.0), jnp.where(use_prev, 0.0, pn)],
                               axis=0).astype(BF16)

    def weighted_values(j, kv, side, pcat):
        vdt = keys_of(j)[1][kv * HEAD_DIM:(kv + 1) * HEAD_DIM, :]
        vst = jnp.concatenate([vdt, no_values] if side == 0 else [no_values, vdt], axis=0)
        return jnp.dot(vst, pcat, preferred_element_type=F32)

    tasks = [(j, kv, side) for j in range(tq // WINDOW) for kv in range(N_KV_HEADS) for side in range(2)]
    pending = {}
    out = None
    for n in range(len(tasks) + SCORE_AHEAD):
        if n < len(tasks):
            pending[n] = scores(*tasks[n])
        if n < SCORE_AHEAD:
            continue
        j, kv, side = tasks[n - SCORE_AHEAD]
        o = weighted_values(j, kv, side, probs(j, kv, side, pending.pop(n - SCORE_AHEAD)))
        if side == 0:
            out = o
            continue
        out = out + o
        c0 = kv * Q_PER_KV * HEAD_DIM
        acc_ref[c0:c0 + LANES, :] = out[:, :WINDOW]
        acc_ref[c0 + LANES:c0 + 2 * LANES, :] = out[:, WINDOW:]
        if kv == N_KV_HEADS - 1:
            a = acc_ref[...]
            ms = jnp.mean(a * a, axis=0, keepdims=True)
            o_ref[j * WINDOW:(j + 1) * WINDOW, :] = (
                (a * lax.rsqrt(ms + RMS_EPS)).T * g_ref[...]).astype(BF16)


def _attention(q, k, vt, sinks, g_attn_out, batches):
    tq = TILES.attention_rows
    nt = SEQ // tq
    per = tq // WINDOW
    b0, b1 = batches
    row = lambda b, i: ((b0 + b) * nt + i, 0)
    col = lambda b, i: (0, (b0 + b) * nt + i)
    halo = lambda b, i: jnp.maximum((b0 + b) * (SEQ // WINDOW) + i * per - 1, 0)
    return pl.pallas_call(
        _attn_kernel,
        grid=(b1 - b0, nt),
        in_specs=[
            pl.BlockSpec(memory_space=pltpu.SMEM),
            pl.BlockSpec((tq, D_ATTN), row),
            pl.BlockSpec((tq, D_KV_DUP), row),
            pl.BlockSpec((D_KV, tq), col),
            pl.BlockSpec((WINDOW, D_KV_DUP), lambda b, i: (halo(b, i), 0)),
            pl.BlockSpec((D_KV, WINDOW), lambda b, i: (0, halo(b, i))),
            pl.BlockSpec((1, D_ATTN), lambda b, i: (0, 0)),
        ],
        out_specs=pl.BlockSpec((tq, D_ATTN), lambda b, i: (b * nt + i, 0)),
        out_shape=jax.ShapeDtypeStruct(((b1 - b0) * SEQ, D_ATTN), BF16),
        scratch_shapes=[pltpu.VMEM((D_ATTN, WINDOW), F32)],
        compiler_params=pltpu.CompilerParams(dimension_semantics=("parallel", "parallel")),
        name="attention",
    )(sinks, q, k, vt, k, vt, g_attn_out)


def _ssm_kernel(u_ref, bblk_ref, cblk_ref, lre_ref, lim_ref, dskip_ref,
                wg_ref, wu_ref, wd_ref, after_ref, o_ref, wgu_bf_ref, wd_bf_ref, s_ref, y_ref, carry_ref):
    del after_ref
    tl = u_ref.shape[0]
    rows = tl * BATCH
    for c in range(D_FF // MXU_COLS):
        src = slice(c * MXU_COLS, (c + 1) * MXU_COLS)
        wgu_bf_ref[:, 2 * c * MXU_COLS:(2 * c + 1) * MXU_COLS] = wg_ref[:, src].astype(BF16)
        wgu_bf_ref[:, (2 * c + 1) * MXU_COLS:(2 * c + 2) * MXU_COLS] = wu_ref[:, src].astype(BF16)
    wd_bf_ref[...] = wd_ref[...].astype(BF16)

    @pl.when(pl.program_id(0) == 0)
    def _():
        carry_ref[...] = jnp.zeros_like(carry_ref)

    u = u_ref[...].reshape(rows, D_SSM)
    ub = u.astype(BF16)

    def project_in(gt):
        ch = slice(gt * TILE_CH, (gt + 1) * TILE_CH)
        s_ref[gt % 2] = jnp.dot(ub[:, ch], bblk_ref[gt], preferred_element_type=F32)

    project_in(0)
    for gt in range(N_GROUP_TILES):
        buf = gt % 2
        ch = slice(gt * TILE_CH, (gt + 1) * TILE_CH)
        if gt + 1 < N_GROUP_TILES:
            project_in(gt + 1)
        for c in range(N_CHUNKS):
            lanes = slice(c * LANES, (c + 1) * LANES)
            re_cols, im_cols = _re_cols(c), _im_cols(c)
            lr = lre_ref[gt, :, lanes]
            li = lim_ref[gt, :, lanes]
            sr = carry_ref[gt, :, re_cols]
            si = carry_ref[gt, :, im_cols]
            for t in range(tl):
                r = slice(t * BATCH, (t + 1) * BATCH)
                nr = lr * sr - li * si + s_ref[buf, r, re_cols]
                ni = lr * si + li * sr + s_ref[buf, r, im_cols]
                s_ref[buf, r, re_cols] = nr
                s_ref[buf, r, im_cols] = ni
                sr, si = nr, ni
            carry_ref[gt, :, re_cols] = sr
            carry_ref[gt, :, im_cols] = si
        y = jnp.dot(s_ref[buf].astype(BF16), cblk_ref[gt], preferred_element_type=F32)
        y_ref[:, ch] = y + dskip_ref[:, ch] * u[:, ch]
    o_ref[...] = y_ref[...].reshape(tl, BATCH, D_SSM)


def _ssm_glu_kernel(y_ref, wglu_ref, bglu_ref, g_ref, o_ref):
    tl = y_ref.shape[0]
    y = y_ref[...].reshape(tl * BATCH, D_SSM)
    for r in range(tl * BATCH // ROW_SUB):
        rs = slice(r * ROW_SUB, (r + 1) * ROW_SUB)
        z = jax.nn.gelu(y[rs, :])
        gate = jax.nn.sigmoid(jnp.dot(z.astype(BF16), wglu_ref[...], preferred_element_type=F32)
                              + bglu_ref[...])
        o_ref[r * (ROW_SUB // BATCH):(r + 1) * (ROW_SUB // BATCH)] = _rms(z * gate, g_ref[...]).reshape(
            ROW_SUB // BATCH, BATCH, D_SSM)


def _ssm_glu(y3, w_glu_bf, b_glu, g_ssm_out):
    tl = TILES.ssm_glu_steps
    c2 = lambda i: (0, 0)
    blk = pl.BlockSpec((tl, BATCH, D_SSM), lambda i: (i, 0, 0))
    return pl.pallas_call(
        _ssm_glu_kernel,
        grid=(SEQ // tl,),
        in_specs=[blk, pl.BlockSpec((D_SSM, D_SSM), c2), pl.BlockSpec((1, D_SSM), c2),
                  pl.BlockSpec((1, D_SSM), c2)],
        out_specs=blk,
        out_shape=jax.ShapeDtypeStruct((SEQ, BATCH, D_SSM), F32),
        compiler_params=pltpu.CompilerParams(dimension_semantics=("parallel",)),
        name="ssm_glu",
    )(y3, w_glu_bf, b_glu, g_ssm_out)


def _ssm(u3, bblk, cblk, lam_re, lam_im, d_skip, w_gate, w_up, w_down, run_after):
    tl = TILES.ssm_steps
    rows = tl * BATCH
    c2 = lambda i: (0, 0)
    c3 = lambda i: (0, 0, 0)
    slab = lambda i: (i, 0)
    up_slab = (D_MODEL // N_CAST_STEPS, D_FF)
    down_slab = (D_FF // N_CAST_STEPS, D_MODEL)
    return pl.pallas_call(
        _ssm_kernel,
        grid=(N_CAST_STEPS,),
        in_specs=[
            pl.BlockSpec((tl, BATCH, D_SSM), lambda i: (i, 0, 0)),
            pl.BlockSpec((N_GROUP_TILES, TILE_CH, 2 * TILE_ST), c3),
            pl.BlockSpec((N_GROUP_TILES, 2 * TILE_ST, TILE_CH), c3),
            pl.BlockSpec((N_GROUP_TILES, BATCH, TILE_ST), c3),
            pl.BlockSpec((N_GROUP_TILES, BATCH, TILE_ST), c3),
            pl.BlockSpec((1, D_SSM), c2),
            pl.BlockSpec(up_slab, slab),
            pl.BlockSpec(up_slab, slab),
            pl.BlockSpec(down_slab, slab),
            pl.BlockSpec(memory_space=pl.ANY),
        ],
        out_specs=[
            pl.BlockSpec((tl, BATCH, D_SSM), lambda i: (i, 0, 0)),
            pl.BlockSpec((up_slab[0], 2 * D_FF), slab),
            pl.BlockSpec(down_slab, slab),
        ],
        out_shape=[
            jax.ShapeDtypeStruct((SEQ, BATCH, D_SSM), F32),
            jax.ShapeDtypeStruct((D_MODEL, 2 * D_FF), BF16),
            jax.ShapeDtypeStruct((D_FF, D_MODEL), BF16),
        ],
        scratch_shapes=[
            pltpu.VMEM((2, rows, 2 * TILE_ST), F32),
            pltpu.VMEM((rows, D_SSM), F32),
            pltpu.VMEM((N_GROUP_TILES, BATCH, 2 * TILE_ST), F32),
        ],
        compiler_params=pltpu.CompilerParams(
            dimension_semantics=("arbitrary",), vmem_limit_bytes=TILES.vmem_limit),
        name="ssm",
    )(u3, bblk, cblk, lam_re, lam_im, d_skip, w_gate, w_up, w_down, run_after)


def _outproj_kernel(attn_lo_ref, attn_hi_ref, ssm_ref, x_ref, wo_ref, gpost_ref, gpre_ref, h_ref, hn_ref):
    tm = x_ref.shape[0]
    low_batches = pl.program_id(0) < BATCH // 2
    for r in range(tm // ROW_SUB):
        rs = slice(r * ROW_SUB, (r + 1) * ROW_SUB)
        attn = jnp.where(low_batches, attn_lo_ref[rs, :], attn_hi_ref[rs, :])
        mix = jnp.dot(attn, wo_ref[:D_ATTN, :], preferred_element_type=F32)
        mix = mix + jnp.dot(ssm_ref[rs, :].astype(BF16), wo_ref[D_ATTN:, :],
                            preferred_element_type=F32)
        h = x_ref[rs, :] + _rms(mix, gpost_ref[...])
        h_ref[rs, :] = h
        hn_ref[rs, :] = _rms(h, gpre_ref[...]).astype(BF16)


def _out_proj(attn_lo, attn_hi, ssm2, x2, w_o_bf, g_post_mix, g_pre_ffn):
    T = x2.shape[0]
    tm = TILES.out_proj_rows
    nt = SEQ // tm
    half = BATCH // 2
    row = lambda b, i: (b * nt + i, 0)
    const = lambda b, i: (0, 0)
    lo_row = lambda b, i: (jnp.where(b < half, b * nt + i, half * nt - 1), 0)
    hi_row = lambda b, i: (jnp.where(b < half, 0, (b - half) * nt + i), 0)
    return pl.pallas_call(
        _outproj_kernel,
        grid=(BATCH, nt),
        in_specs=[
            pl.BlockSpec((tm, D_ATTN), lo_row),
            pl.BlockSpec((tm, D_ATTN), hi_row),
            pl.BlockSpec((tm, D_SSM), lambda b, i: (i, b)),
            pl.BlockSpec((tm, D_MODEL), row),
            pl.BlockSpec((D_MODEL, D_MODEL), const),
            pl.BlockSpec((1, D_MODEL), const),
            pl.BlockSpec((1, D_MODEL), const),
        ],
        out_specs=[pl.BlockSpec((tm, D_MODEL), row), pl.BlockSpec((tm, D_MODEL), row)],
        out_shape=[jax.ShapeDtypeStruct((T, D_MODEL), F32), jax.ShapeDtypeStruct((T, D_MODEL), BF16)],
        compiler_params=pltpu.CompilerParams(
            dimension_semantics=("parallel", "parallel"), vmem_limit_bytes=TILES.vmem_limit),
        name="out_proj",
    )(attn_lo, attn_hi, ssm2, x2, w_o_bf, g_post_mix, g_pre_ffn)


def _ffn_up_kernel(hn_ref, wgu_ref, hid_ref):
    for r in range(hn_ref.shape[0] // ROW_SUB):
        rs = slice(r * ROW_SUB, (r + 1) * ROW_SUB)
        gu = jnp.dot(hn_ref[rs, :], wgu_ref[...], preferred_element_type=F32)
        hid_ref[rs, :] = jnp.concatenate(
            [jax.nn.silu(gu[:, 2 * c * MXU_COLS:(2 * c + 1) * MXU_COLS])
             * gu[:, (2 * c + 1) * MXU_COLS:(2 * c + 2) * MXU_COLS]
             for c in range(gu.shape[1] // (2 * MXU_COLS))], axis=1).astype(BF16)


def _ffn_down_kernel(hid_ref, wd_ref, h_ref, g_ref, o_ref):
    for r in range(hid_ref.shape[0] // ROW_SUB):
        rs = slice(r * ROW_SUB, (r + 1) * ROW_SUB)
        ff = jnp.dot(hid_ref[rs, :], wd_ref[...], preferred_element_type=F32)
        o_ref[rs, :] = h_ref[rs, :] + _rms(ff, g_ref[...])


def _ffn(hn, h, wgu_bf, wd_bf, g_post_ffn):
    T = h.shape[0]
    tm, tf, td = TILES.ffn_rows, TILES.ffn_cols, TILES.ffn_down_rows
    hid = pl.pallas_call(
        _ffn_up_kernel,
        grid=(D_FF // tf, T // tm),
        in_specs=[
            pl.BlockSpec((tm, D_MODEL), lambda f, i: (i, 0)),
            pl.BlockSpec((D_MODEL, 2 * tf), lambda f, i: (0, f)),
        ],
        out_specs=pl.BlockSpec((tm, tf), lambda f, i: (i, f)),
        out_shape=jax.ShapeDtypeStruct((T, D_FF), BF16),
        compiler_params=pltpu.CompilerParams(
            dimension_semantics=("parallel", "parallel"), vmem_limit_bytes=TILES.vmem_limit),
        name="ffn_up",
    )(hn, wgu_bf)
    return pl.pallas_call(
        _ffn_down_kernel,
        grid=(T // td,),
        in_specs=[
            pl.BlockSpec((td, D_FF), lambda i: (i, 0)),
            pl.BlockSpec((D_FF, D_MODEL), lambda i: (0, 0), pipeline_mode=pl.Buffered(1)),
            pl.BlockSpec((td, D_MODEL), lambda i: (i, 0)),
            pl.BlockSpec((1, D_MODEL), lambda i: (0, 0)),
        ],
        out_specs=pl.BlockSpec((td, D_MODEL), lambda i: (i, 0)),
        out_shape=jax.ShapeDtypeStruct((T, D_MODEL), F32),
        compiler_params=pltpu.CompilerParams(
            dimension_semantics=("parallel",), vmem_limit_bytes=TILES.ffn_vmem_limit),
        name="ffn_down",
    )(hid, wd_bf, h, g_post_ffn)


def kernel(x, positions, g_pre_mix, w_in, sinks, a_re, a_im, log_dt, b_re, b_im, c_re, c_im, d_skip,
           w_glu, b_glu, g_attn_out, g_ssm_out, w_o, g_post_mix, g_pre_ffn, w_gate, w_up, w_down,
           g_post_ffn):
    depth = w_in.shape[0]
    T = BATCH * SEQ
    h = x.reshape(T, D_MODEL)
    pos2 = positions.reshape(T, 1)
    for i in range(depth):
        lam_re, lam_im, bblk, cblk, wvt_bf = _ssm_prep(a_re[i], a_im[i], log_dt[i], b_re[i], b_im[i],
                                                       c_re[i], c_im[i], w_in[i])
        w_in_bf = w_in[i].astype(BF16)
        q, k, vt, u, w_o_bf, w_glu_bf = _in_proj(h, pos2, g_pre_mix[i][None, :], w_in_bf, wvt_bf,
                                                 w_o[i], w_glu[i])
        half = BATCH // 2
        attn_lo = _attention(q, k, vt, sinks[i], g_attn_out[i][None, :], batches=(0, half))
        y, wgu_bf, wd_bf = _ssm(u.reshape(SEQ, BATCH, D_SSM), bblk, cblk, lam_re, lam_im,
                                d_skip[i].reshape(1, D_SSM), w_gate[i], w_up[i], w_down[i],
                                run_after=attn_lo)
        ssm_n = _ssm_glu(y, w_glu_bf, b_glu[i][None, :], g_ssm_out[i][None, :])
        attn_hi = _attention(q, k, vt, sinks[i], g_attn_out[i][None, :], batches=(half, BATCH))
        h, hn = _out_proj(attn_lo, attn_hi, ssm_n.reshape(SEQ, BATCH * D_SSM), h, w_o_bf,
                          g_post_mix[i][None, :], g_pre_ffn[i][None, :])
        h = _ffn(hn, h, wgu_bf, wd_bf, g_post_ffn[i][None, :])
    return h.reshape(BATCH, SEQ, D_MODEL)
```
